```python
import jax, jax.numpy as jnp
from jax import lax
import numpy as np

D_MODEL = 1024
BATCH = 8
SEQ = 2048
DEPTH = 4
DEC_BATCH = 128
DEC_SEQ = 1
PAST_LEN = 8192
PAGE_SIZE = 128

PLE_DIM = 256
N_MIXERS = 2
N_A = (DEPTH + 1) // 2
N_B = DEPTH // 2
EPS = 1e-6

ML_HEADS = 4
ML_DV = D_MODEL // ML_HEADS
ML_DK = ML_DV // 2
ML_CHUNK = 64
ML_HK = ML_HEADS * ML_DK
ML_IN = 2 * ML_HK + 2 * D_MODEL + 2 * ML_HEADS
FORGET_BIAS = 3.0

SW_HEADS = 16
SW_KV_HEADS = 4
SW_HD = D_MODEL // SW_HEADS
SW_GROUP = SW_HEADS // SW_KV_HEADS
WINDOW = 128
SW_BLOCK = WINDOW
SW_IN = (SW_HEADS + 2 * SW_KV_HEADS) * SW_HD
ROPE_THETA = 10000.0

D_FF = 2816
CONV_W = 3

kernel_name = "hybrid_mlstm_swa_convffn_decode_step"


def rmsnorm(x, g):
    xf = x.astype(jnp.float32)
    y = xf * lax.rsqrt(jnp.mean(xf * xf, axis=-1, keepdims=True) + EPS)
    return (y * g.astype(jnp.float32)).astype(x.dtype)


def rope(x, pos):
    half = x.shape[-1] // 2
    inv = ROPE_THETA ** (-jnp.arange(half, dtype=jnp.float32) / half)
    ang = pos[:, None] * inv[None, :]
    cos = jnp.cos(ang)[None, :, None, :]
    sin = jnp.sin(ang)[None, :, None, :]
    xf = x.astype(jnp.float32)
    x1, x2 = xf[..., :half], xf[..., half:]
    return jnp.concatenate([x1 * cos - x2 * sin, x2 * cos + x1 * sin], axis=-1).astype(x.dtype)


def mlstm_chunk(carry, inp):
    C, n, m = carry
    q, k, v, ig, lf = inp
    L = q.shape[2]
    b = jnp.cumsum(lf, axis=-1)
    causal = jnp.tril(jnp.ones((L, L), dtype=bool))
    d = jnp.where(causal, b[..., :, None] - b[..., None, :] + ig[..., None, :], -jnp.inf)
    a = b + m[..., None]
    m_t = jnp.maximum(a, jnp.max(d, axis=-1))
    w_inter = jnp.exp(a - m_t)
    s = jnp.einsum('bhtk,bhsk->bhts', q, k) * jnp.exp(d - m_t[..., None])
    num = w_inter[..., None] * jnp.einsum('bhtk,bhkv->bhtv', q, C) + jnp.einsum('bhts,bhsv->bhtv', s, v)
    qn = w_inter * jnp.einsum('bhtk,bhk->bht', q, n) + jnp.sum(s, axis=-1)
    h = num / jnp.maximum(jnp.abs(qn), jnp.exp(-m_t))[..., None]
    bL = b[..., -1]
    g = bL[..., None] - b + ig
    m_new = jnp.maximum(bL + m, jnp.max(g, axis=-1))
    w_old = jnp.exp(bL + m - m_new)
    w_s = jnp.exp(g - m_new[..., None])
    C_new = w_old[..., None, None] * C + jnp.einsum('bhs,bhsk,bhsv->bhkv', w_s, k, v)
    n_new = w_old[..., None] * n + jnp.einsum('bhs,bhsk->bhk', w_s, k)
    return (C_new, n_new, m_new), h


def mlstm_mixer(xn, w_in, b_gate, w_out, state, chunk):
    B, T, _ = xn.shape
    proj = xn @ w_in
    f32 = jnp.float32
    q = proj[..., :ML_HK].astype(f32).reshape(B, T, ML_HEADS, ML_DK).transpose(0, 2, 1, 3) * (ML_DK ** -0.5)
    k = proj[..., ML_HK:2 * ML_HK].astype(f32).reshape(B, T, ML_HEADS, ML_DK).transpose(0, 2, 1, 3)
    v = proj[..., 2 * ML_HK:2 * ML_HK + D_MODEL].astype(f32).reshape(B, T, ML_HEADS, ML_DV).transpose(0, 2, 1, 3)
    o = proj[..., 2 * ML_HK + D_MODEL:2 * ML_HK + 2 * D_MODEL].astype(f32)
    gates = proj[..., 2 * ML_HK + 2 * D_MODEL:].astype(f32) + b_gate.astype(f32)
    ig = gates[..., :ML_HEADS].transpose(0, 2, 1)
    lf = jax.nn.log_sigmoid(gates[..., ML_HEADS:]).transpose(0, 2, 1)
    nc = T // chunk

    def to_chunks(a):
        a = a.reshape(a.shape[:2] + (nc, chunk) + a.shape[3:])
        return jnp.moveaxis(a, 2, 0)

    init = tuple(s.astype(f32) for s in state)
    (C, n, m), h = lax.scan(mlstm_chunk, init, (to_chunks(q), to_chunks(k), to_chunks(v), to_chunks(ig), to_chunks(lf)))
    h = jnp.moveaxis(h, 0, 2).reshape(B, ML_HEADS, T, ML_DV).transpose(0, 2, 1, 3).reshape(B, T, D_MODEL)
    y = (jax.nn.sigmoid(o) * h).astype(xn.dtype) @ w_out
    return y, (C.astype(xn.dtype), n.astype(xn.dtype), m.astype(xn.dtype))


def sink_probs(s, sinks):
    sk = sinks.astype(jnp.float32).reshape(SW_KV_HEADS, SW_GROUP)[..., None, None]
    mx = jnp.maximum(jnp.max(s, axis=-1, keepdims=True), sk)
    p = jnp.exp(s - mx)
    return p / (jnp.sum(p, axis=-1, keepdims=True) + jnp.exp(sk - mx))


def swa_qkv(xn, w_in, pos):
    B, T, _ = xn.shape
    proj = xn @ w_in
    q = proj[..., :SW_HEADS * SW_HD].reshape(B, T, SW_HEADS, SW_HD)
    k = proj[..., SW_HEADS * SW_HD:(SW_HEADS + SW_KV_HEADS) * SW_HD].reshape(B, T, SW_KV_HEADS, SW_HD)
    v = proj[..., (SW_HEADS + SW_KV_HEADS) * SW_HD:].reshape(B, T, SW_KV_HEADS, SW_HD)
    return rope(q, pos), rope(k, pos), v


def swa_prompt(xn, w_in, sinks, w_out):
    B, T, _ = xn.shape
    q, k, v = swa_qkv(xn, w_in, jnp.arange(T, dtype=jnp.float32))
    nb = T // SW_BLOCK
    qb = q.reshape(B, nb, SW_BLOCK, SW_KV_HEADS, SW_GROUP, SW_HD)
    kb = k.reshape(B, nb, SW_BLOCK, SW_KV_HEADS, SW_HD)
    vb = v.reshape(B, nb, SW_BLOCK, SW_KV_HEADS, SW_HD)
    pad = ((0, 0), (1, 0), (0, 0), (0, 0), (0, 0))
    kc = jnp.concatenate([jnp.pad(kb[:, :-1], pad), kb], axis=2)
    vc = jnp.concatenate([jnp.pad(vb[:, :-1], pad), vb], axis=2)
    s = jnp.einsum('bnqkgd,bnskd->bnkgqs', qb, kc).astype(jnp.float32) * (SW_HD ** -0.5)
    qi = jnp.arange(SW_BLOCK)[:, None] + SW_BLOCK
    si = jnp.arange(2 * SW_BLOCK)[None, :]
    diff = qi - si
    band = (diff >= 0) & (diff <= WINDOW)
    blk_ok = (jnp.arange(nb)[:, None, None] > 0) | (si[None] >= SW_BLOCK)
    mask = band[None] & blk_ok
    s = jnp.where(mask[None, :, None, None], s, -jnp.inf)
    p = sink_probs(s, sinks).astype(v.dtype)
    o = jnp.einsum('bnkgqs,bnskd->bnqkgd', p, vc).reshape(B, T, SW_HEADS * SW_HD)
    nkeep = min(WINDOW, T)
    return o @ w_out, k[:, T - nkeep:], v[:, T - nkeep:]


def swa_sample(xn, w_in, sinks, w_out, kbuf, vbuf):
    B, T, _ = xn.shape
    nbuf = kbuf.shape[1]
    qpos = PAST_LEN + jnp.arange(T, dtype=jnp.int32)
    q, k, v = swa_qkv(xn, w_in, qpos.astype(jnp.float32))
    kc = jnp.concatenate([kbuf.astype(k.dtype), k], axis=1)
    vc = jnp.concatenate([vbuf.astype(v.dtype), v], axis=1)
    kpos = jnp.concatenate([PAST_LEN - nbuf + jnp.arange(nbuf, dtype=jnp.int32), qpos])
    diff = qpos[:, None] - kpos[None, :]
    mask = (diff >= 0) & (diff <= WINDOW)
    qg = q.reshape(B, T, SW_KV_HEADS, SW_GROUP, SW_HD)
    s = jnp.einsum('btkgd,bskd->bkgts', qg, kc).astype(jnp.float32) * (SW_HD ** -0.5)
    s = jnp.where(mask[None, None, None], s, -jnp.inf)
    p = sink_probs(s, sinks).astype(v.dtype)
    o = jnp.einsum('bkgts,bskd->btkgd', p, vc).reshape(B, T, SW_HEADS * SW_HD)
    return o @ w_out, kc[:, -nbuf:], vc[:, -nbuf:]


def conv_ffn(xn, w_up, conv_w, conv_b, w_down, prev):
    u = xn @ w_up
    T = u.shape[1]
    up = jnp.concatenate([prev.astype(u.dtype), u], axis=1)
    c = conv_b
    for j in range(CONV_W):
        c = c + up[:, j:j + T] * conv_w[j]
    h = jax.nn.silu(c[..., :D_FF]) * c[..., D_FF:]
    return h @ w_down, up[:, -(CONV_W - 1):]


def ple_term(h, p, g, w_gate, w_proj):
    return jax.nn.sigmoid(rmsnorm(h, g) @ w_gate) * (p @ w_proj)


def setup_inputs(seed: int = 0) -> dict:
    key = jax.random.key(seed)
    ks = iter(jax.random.split(key, 32))

    def nrm(shape, scale=1.0):
        return jax.random.normal(next(ks), shape, jnp.float32) * scale

    nbuf = min(WINDOW, PAST_LEN)
    D = D_MODEL
    return {
        "x_prompt": nrm((BATCH, SEQ, D)),
        "x_sample": nrm((DEC_BATCH, DEC_SEQ, D)),
        "p_prompt": nrm((DEPTH, BATCH, SEQ, PLE_DIM)),
        "p_sample": nrm((DEPTH, DEC_BATCH, DEC_SEQ, PLE_DIM)),
        "state_mlstm_C": nrm((N_A, DEC_BATCH, ML_HEADS, ML_DK, ML_DV)),
        "state_mlstm_n": nrm((N_A, DEC_BATCH, ML_HEADS, ML_DK)),
        "state_mlstm_m": nrm((N_A, DEC_BATCH, ML_HEADS)),
        "cache_swa_k": nrm((N_B, DEC_BATCH, nbuf, SW_KV_HEADS, SW_HD)),
        "cache_swa_v": nrm((N_B, DEC_BATCH, nbuf, SW_KV_HEADS, SW_HD)),
        "state_conv": nrm((DEPTH, DEC_BATCH, CONV_W - 1, 2 * D_FF)),
        "g_pre_mix": 1.0 + nrm((DEPTH, D), 0.02),
        "g_post_mix": 1.0 + nrm((DEPTH, D), 0.02),
        "g_pre_ffn": 1.0 + nrm((DEPTH, D), 0.02),
        "g_post_ffn": 1.0 + nrm((DEPTH, D), 0.02),
        "g_ple": 1.0 + nrm((DEPTH, D), 0.02),
        "ml_w_in": nrm((N_A, D, ML_IN), D ** -0.5),
        "ml_b_gate": nrm((N_A, 2 * ML_HEADS), 0.1) + jnp.repeat(jnp.array([0.0, FORGET_BIAS], jnp.float32), ML_HEADS),
        "ml_w_out": nrm((N_A, D, D), D ** -0.5),
        "sw_w_in": nrm((N_B, D, SW_IN), D ** -0.5),
        "sw_sinks": nrm((N_B, SW_HEADS), 0.5),
        "sw_w_out": nrm((N_B, SW_HEADS * SW_HD, D), (SW_HEADS * SW_HD) ** -0.5),
        "ffn_w_up": nrm((DEPTH, D, 2 * D_FF), D ** -0.5),
        "ffn_conv_w": nrm((DEPTH, CONV_W, 2 * D_FF), CONV_W ** -0.5),
        "ffn_conv_b": nrm((DEPTH, 2 * D_FF), 0.02),
        "ffn_w_down": nrm((DEPTH, D_FF, D), D_FF ** -0.5),
        "ple_w_proj": nrm((DEPTH, PLE_DIM, D), PLE_DIM ** -0.5),
        "ple_w_gate": nrm((DEPTH, D, D), D ** -0.5),
    }


def reference(x_prompt, x_sample, p_prompt, p_sample, state_mlstm_C, state_mlstm_n, state_mlstm_m,
              cache_swa_k, cache_swa_v, state_conv, g_pre_mix, g_post_mix, g_pre_ffn, g_post_ffn, g_ple,
              ml_w_in, ml_b_gate, ml_w_out, sw_w_in, sw_sinks, sw_w_out,
              ffn_w_up, ffn_conv_w, ffn_conv_b, ffn_w_down, ple_w_proj, ple_w_gate):
    hp, hs = x_prompt, x_sample
    Bp = hp.shape[0]
    Cp, Np, Mp, Cs, Ns, Ms = [], [], [], [], [], []
    Kp, Vp, Ks, Vs = [], [], [], []
    convp, convs = [], []
    for i in range(DEPTH):
        j = i // N_MIXERS
        ap = rmsnorm(hp, g_pre_mix[i])
        a_s = rmsnorm(hs, g_pre_mix[i])
        if i % N_MIXERS == 0:
            zero = (jnp.zeros((Bp, ML_HEADS, ML_DK, ML_DV), jnp.float32),
                    jnp.zeros((Bp, ML_HEADS, ML_DK), jnp.float32),
                    jnp.zeros((Bp, ML_HEADS), jnp.float32))
            yp, (c1, n1, m1) = mlstm_mixer(ap, ml_w_in[j], ml_b_gate[j], ml_w_out[j], zero, ML_CHUNK)
            ys, (c2, n2, m2) = mlstm_mixer(a_s, ml_w_in[j], ml_b_gate[j], ml_w_out[j],
                                           (state_mlstm_C[j], state_mlstm_n[j], state_mlstm_m[j]), a_s.shape[1])
            Cp.append(c1); Np.append(n1); Mp.append(m1)
            Cs.append(c2); Ns.append(n2); Ms.append(m2)
        else:
            yp, k1, v1 = swa_prompt(ap, sw_w_in[j], sw_sinks[j], sw_w_out[j])
            ys, k2, v2 = swa_sample(a_s, sw_w_in[j], sw_sinks[j], sw_w_out[j], cache_swa_k[j], cache_swa_v[j])
            Kp.append(k1); Vp.append(v1); Ks.append(k2); Vs.append(v2)
        hp = hp + rmsnorm(yp, g_post_mix[i])
        hs = hs + rmsnorm(ys, g_post_mix[i])
        fp, cvp = conv_ffn(rmsnorm(hp, g_pre_ffn[i]), ffn_w_up[i], ffn_conv_w[i], ffn_conv_b[i], ffn_w_down[i],
                           jnp.zeros((Bp, CONV_W - 1, 2 * D_FF), hp.dtype))
        fs, cvs = conv_ffn(rmsnorm(hs, g_pre_ffn[i]), ffn_w_up[i], ffn_conv_w[i], ffn_conv_b[i], ffn_w_down[i],
                           state_conv[i])
        convp.append(cvp); convs.append(cvs)
        hp = hp + rmsnorm(fp, g_post_ffn[i])
        hs = hs + rmsnorm(fs, g_post_ffn[i])
        hp = hp + ple_term(hp, p_prompt[i], g_ple[i], ple_w_gate[i], ple_w_proj[i])
        hs = hs + ple_term(hs, p_sample[i], g_ple[i], ple_w_gate[i], ple_w_proj[i])
    return (hp, hs,
            jnp.stack(Cp), jnp.stack(Np), jnp.stack(Mp), jnp.stack(Kp), jnp.stack(Vp), jnp.stack(convp),
            jnp.stack(Cs), jnp.stack(Ns), jnp.stack(Ms), jnp.stack(Ks), jnp.stack(Vs), jnp.stack(convs))
```

```python
import jax
import jax.numpy as jnp
from jax import lax
from jax.experimental import pallas as pl
from jax.experimental.pallas import tpu as pltpu

F32 = jnp.float32
BF16 = jnp.bfloat16

EPS = 1e-6
PLE_DIM = 256
ML_HEADS = 4
ML_DK = 128
ML_DV = 256
ML_HK = ML_HEADS * ML_DK
SW_HEADS = 16
SW_KV_HEADS = 4
SW_HD = 64
SW_GROUP = SW_HEADS // SW_KV_HEADS
WINDOW = 128
ROPE_THETA = 10000.0
PAST_LEN = 8192
CONV_W = 3

LANES = 128
SUBLANES = 8
FFN_CHUNK = 256
FFN_ROWS = 512
ML_CHUNK_ROWS = 256
SW_ROWS = 256
SAMPLE_STATE_BATCH = 8
VMEM_LIMIT = 56 * 1024 * 1024


def _params(*sem):
    return pltpu.CompilerParams(dimension_semantics=sem, vmem_limit_bytes=VMEM_LIMIT)


def _resident(shape):
    zeros = (0,) * len(shape)
    return pl.BlockSpec(shape, lambda *_: zeros, pipeline_mode=pl.Buffered(1))


def _rms(x, g):
    return x * lax.rsqrt(jnp.mean(x * x, axis=-1, keepdims=True) + EPS) * g


def _sigmoid(x):
    return 1.0 / (1.0 + jnp.exp(-x))


def _log_sigmoid(x):
    return jnp.minimum(x, 0.0) - jnp.log1p(jnp.exp(-jnp.abs(x)))


def _dot(a, b):
    return jnp.dot(a, b, preferred_element_type=F32)


def _dot_nt(a, b):
    return lax.dot_general(a, b, (((1,), (1,)), ((), ())), preferred_element_type=F32)


def _dot_tn(a, b):
    return lax.dot_general(a, b, (((0,), (0,)), ((), ())), preferred_element_type=F32)


def _cumsum(x, axis):
    n = x.shape[axis]
    idx = lax.broadcasted_iota(jnp.int32, x.shape, axis)
    s = 1
    while s < n:
        x = x + jnp.where(idx >= s, pltpu.roll(x, s, axis=axis), 0.0)
        s *= 2
    return x


def _ffn_tail(x, f, p, gpost_ref, gple_ref, wgate_ref, wproj_ref):
    x1 = x + _rms(f, gpost_ref[...])
    gate = _sigmoid(_dot(_rms(x1, gple_ref[...]).astype(BF16), wgate_ref[...]))
    return x1 + gate * _dot(p.astype(BF16), wproj_ref[...])


def _ffn_prompt_kernel(x_ref, p_ref, gpre_ref, gpost_ref, gple_ref, wup_ref, cw_ref, cb_ref, wdown_ref,
                       wgate_ref, wproj_ref, out_ref, conv_ref, carry_ref, acc_ref):
    t = pl.program_id(1)
    rows = x_ref.shape[0]
    nch = wup_ref.shape[0]

    @pl.when(t == 0)
    def _():
        carry_ref[...] = jnp.zeros_like(carry_ref)

    x = x_ref[...]
    xn = _rms(x, gpre_ref[...]).astype(BF16)
    acc_ref[...] = jnp.zeros_like(acc_ref)
    row = lax.broadcasted_iota(jnp.int32, (rows, 2 * FFN_CHUNK), 0)

    def chunk(j, _):
        u = _dot(xn, wup_ref[j])
        prev = carry_ref[j]
        prev2 = prev[SUBLANES - 2:SUBLANES - 1]
        prev1 = prev[SUBLANES - 1:SUBLANES]
        u1 = jnp.where(row == 0, prev1, pltpu.roll(u, 1, axis=0))
        u2 = jnp.where(row == 0, prev2, jnp.where(row == 1, prev1, pltpu.roll(u, 2, axis=0)))
        cw = cw_ref[j]
        c = cb_ref[j] + u2 * cw[0:1] + u1 * cw[1:2] + u * cw[2:3]
        cg = c[:, :FFN_CHUNK]
        h = cg * _sigmoid(cg) * c[:, FFN_CHUNK:]
        acc_ref[...] += _dot(h.astype(BF16), wdown_ref[j])
        carry_ref[j] = u[rows - SUBLANES:]
        return 0

    lax.fori_loop(0, nch, chunk, 0)
    out_ref[...] = _ffn_tail(x, acc_ref[...], p_ref[...], gpost_ref, gple_ref, wgate_ref, wproj_ref)

    @pl.when(t == pl.num_programs(1) - 1)
    def _():
        conv_ref[0] = carry_ref[...]


def _ffn_sample_kernel(x_ref, p_ref, prev_ref, gpre_ref, gpost_ref, gple_ref, wup_ref, cw_ref, cb_ref,
                       wdown_ref, wgate_ref, wproj_ref, out_ref, u_ref, acc_ref):
    nch = wup_ref.shape[0]
    x = x_ref[...]
    xn = _rms(x, gpre_ref[...]).astype(BF16)
    acc_ref[...] = jnp.zeros_like(acc_ref)

    def chunk(j, _):
        u = _dot(xn, wup_ref[j])
        cw = cw_ref[j]
        c = cb_ref[j] + prev_ref[0, j] * cw[0:1] + prev_ref[1, j] * cw[1:2] + u * cw[2:3]
        cg = c[:, :FFN_CHUNK]
        h = cg * _sigmoid(cg) * c[:, FFN_CHUNK:]
        acc_ref[...] += _dot(h.astype(BF16), wdown_ref[j])
        u_ref[j] = u
        return 0

    lax.fori_loop(0, nch, chunk, 0)
    out_ref[...] = _ffn_tail(x, acc_ref[...], p_ref[...], gpost_ref, gple_ref, wgate_ref, wproj_ref)


def _chunk_cols(a, nch):
    lead = a.shape[:-1]
    a = a.reshape(lead + (2, nch, FFN_CHUNK))
    a = jnp.moveaxis(a, -3, -2)
    return a.reshape(lead + (nch, 2 * FFN_CHUNK))


def _unchunk_cols(a):
    lead = a.shape[:-2]
    nch = a.shape[-2]
    a = a.reshape(lead + (nch, 2, FFN_CHUNK))
    a = jnp.moveaxis(a, -3, -2)
    return a.reshape(lead + (2 * nch * FFN_CHUNK,))


def _ffn_weights(w_up, conv_w, conv_b, w_down, g_pre, g_post, g_ple, w_gate, w_proj):
    d, f2 = w_up.shape
    nch = f2 // (2 * FFN_CHUNK)
    wup = jnp.moveaxis(_chunk_cols(w_up.astype(BF16), nch), 1, 0)
    cw = jnp.moveaxis(_chunk_cols(conv_w, nch), 1, 0)
    cb = _chunk_cols(conv_b, nch)[:, None, :]
    wdown = w_down.astype(BF16).reshape(nch, FFN_CHUNK, d)
    return dict(gpre=g_pre[None], gpost=g_post[None], gple=g_ple[None], wup=wup, cw=cw, cb=cb, wdown=wdown,
                wgate=w_gate.astype(BF16), wproj=w_proj.astype(BF16))


def _ffn_weight_specs(w):
    return [_resident(w[k].shape) for k in ("gpre", "gpost", "gple", "wup", "cw", "cb", "wdown", "wgate", "wproj")]


def _ffn_weight_args(w):
    return [w[k] for k in ("gpre", "gpost", "gple", "wup", "cw", "cb", "wdown", "wgate", "wproj")]


def _ffn_prompt(h, p, w):
    b, t, d = h.shape
    nch = w["wup"].shape[0]
    nt = t // FFN_ROWS
    assert t % FFN_ROWS == 0
    out, conv = pl.pallas_call(
        _ffn_prompt_kernel,
        grid=(b, nt),
        in_specs=[pl.BlockSpec((FFN_ROWS, d), lambda i, j: (i * nt + j, 0)),
                  pl.BlockSpec((FFN_ROWS, PLE_DIM), lambda i, j: (i * nt + j, 0))] + _ffn_weight_specs(w),
        out_specs=[pl.BlockSpec((FFN_ROWS, d), lambda i, j: (i * nt + j, 0)),
                   pl.BlockSpec((1, nch, SUBLANES, 2 * FFN_CHUNK), lambda i, j: (i, 0, 0, 0))],
        out_shape=[jax.ShapeDtypeStruct((b * t, d), F32),
                   jax.ShapeDtypeStruct((b, nch, SUBLANES, 2 * FFN_CHUNK), F32)],
        scratch_shapes=[pltpu.VMEM((nch, SUBLANES, 2 * FFN_CHUNK), F32), pltpu.VMEM((FFN_ROWS, d), F32)],
        compiler_params=_params("arbitrary", "arbitrary"),
        name="ffn_prompt",
    )(h.reshape(b * t, d), p.reshape(b * t, PLE_DIM), *_ffn_weight_args(w))
    conv = jnp.moveaxis(conv[:, :, SUBLANES - (CONV_W - 1):, :], 1, 2)
    return out.reshape(b, t, d), _unchunk_cols(conv)


def _ffn_sample(h, p, prev, w):
    b, d = h.shape
    nch = w["wup"].shape[0]
    prev_c = jnp.moveaxis(_chunk_cols(prev, nch), 0, 2)
    out, u = pl.pallas_call(
        _ffn_sample_kernel,
        grid=(1,),
        in_specs=[_resident((b, d)), _resident((b, PLE_DIM)), _resident(prev_c.shape)] + _ffn_weight_specs(w),
        out_specs=[pl.BlockSpec((b, d), lambda i: (0, 0)),
                   pl.BlockSpec((nch, b, 2 * FFN_CHUNK), lambda i: (0, 0, 0))],
        out_shape=[jax.ShapeDtypeStruct((b, d), F32), jax.ShapeDtypeStruct((nch, b, 2 * FFN_CHUNK), F32)],
        scratch_shapes=[pltpu.VMEM((b, d), F32)],
        compiler_params=_params("arbitrary"),
        name="ffn_sample",
    )(h, p, prev_c, *_ffn_weight_args(w))
    u = _unchunk_cols(jnp.moveaxis(u, 0, 1))
    return out, jnp.stack([prev[:, 1], u], axis=1)


def _mlstm_weights(w_in, b_gate, w_out, g_pre, g_post):
    nq = 2 * ML_HK + 2 * ML_HEADS * ML_DV
    wg = w_in[:, nq:]
    wgc = jnp.pad(wg, ((0, 0), (0, LANES - 2 * ML_HEADS))).astype(BF16)
    bgc = jnp.pad(b_gate, (0, LANES - 2 * ML_HEADS))[None]
    wgr = wg.T.astype(BF16)
    bgr = jnp.broadcast_to(b_gate[:, None], (2 * ML_HEADS, LANES))
    return dict(gpre=g_pre[None], gpost=g_post[None], win=w_in[:, :nq].astype(BF16), wgc=wgc, bgc=bgc,
                wgr=wgr, bgr=bgr, wout=w_out.astype(BF16))


_ML_KEYS = ("gpre", "gpost", "win", "wgc", "bgc", "wgr", "bgr", "wout")


def _mlstm_prompt_kernel(x_ref, gpre_ref, gpost_ref, win_ref, wgc_ref, bgc_ref, wgr_ref, bgr_ref, wout_ref,
                         out_ref, c_out_ref, n_out_ref, m_out_ref, c_ref, n_ref, m_ref, h_ref):
    t = pl.program_id(1)
    L = x_ref.shape[0]

    @pl.when(t == 0)
    def _():
        c_ref[...] = jnp.zeros_like(c_ref)
        n_ref[...] = jnp.zeros_like(n_ref)
        m_ref[...] = jnp.zeros_like(m_ref)

    x = x_ref[...]
    xn = _rms(x, gpre_ref[...]).astype(BF16)
    proj = _dot(xn, win_ref[...])
    gc = _dot(xn, wgc_ref[...]) + bgc_ref[...]
    gr = _dot_nt(wgr_ref[...], xn) + bgr_ref[:, 0:1]
    is_forget_r = lax.broadcasted_iota(jnp.int32, gr.shape, 0) >= ML_HEADS
    bc = _cumsum(_log_sigmoid(gc), 0)
    br = _cumsum(jnp.where(is_forget_r, _log_sigmoid(gr), 0.0), 1)
    causal = lax.broadcasted_iota(jnp.int32, (L, L), 0) >= lax.broadcasted_iota(jnp.int32, (L, L), 1)

    for h in range(ML_HEADS):
        q = proj[:, h * ML_DK:(h + 1) * ML_DK] * (ML_DK ** -0.5)
        k = proj[:, ML_HK + h * ML_DK:ML_HK + (h + 1) * ML_DK]
        v = proj[:, 2 * ML_HK + h * ML_DV:2 * ML_HK + (h + 1) * ML_DV]
        qb, kb, vb = q.astype(BF16), k.astype(BF16), v.astype(BF16)
        ig_r = gr[h:h + 1, :]
        b_r = br[ML_HEADS + h:ML_HEADS + h + 1, :]
        ig_c = gc[:, h:h + 1]
        b_c = bc[:, ML_HEADS + h:ML_HEADS + h + 1]
        m_prev = m_ref[h:h + 1, 0:1]
        c_prev = c_ref[h]
        n_prev = n_ref[h:h + 1, :]

        d = jnp.where(causal, b_c + (ig_r - b_r), -jnp.inf)
        a = b_c + m_prev
        m_t = jnp.maximum(a, jnp.max(d, axis=1, keepdims=True))
        w_inter = jnp.exp(a - m_t)
        s = _dot_nt(qb, kb) * jnp.exp(d - m_t)
        num = w_inter * _dot(qb, c_prev.astype(BF16)) + _dot(s.astype(BF16), vb)
        qn = w_inter * jnp.sum(q * n_prev, axis=1, keepdims=True) + jnp.sum(s, axis=1, keepdims=True)
        h_ref[:, h * ML_DV:(h + 1) * ML_DV] = num / jnp.maximum(jnp.abs(qn), jnp.exp(-m_t))

        b_last = b_r[:, L - 1:L]
        g_r = b_last - b_r + ig_r
        g_c = b_last - b_c + ig_c
        m_new = jnp.maximum(b_last + m_prev, jnp.max(g_r, axis=1, keepdims=True))
        w_old = jnp.exp(b_last + m_prev - m_new)
        wk = jnp.exp(g_c - m_new) * k
        c_ref[h] = w_old * c_prev + _dot_tn(wk.astype(BF16), vb)
        n_ref[h:h + 1, :] = w_old * n_prev + jnp.sum(wk, axis=0, keepdims=True)
        m_ref[h:h + 1, :] = jnp.broadcast_to(m_new, (1, LANES))

    o = proj[:, 2 * ML_HK + ML_HEADS * ML_DV:]
    y = _dot((_sigmoid(o) * h_ref[...]).astype(BF16), wout_ref[...])
    out_ref[...] = x + _rms(y, gpost_ref[...])

    @pl.when(t == pl.num_programs(1) - 1)
    def _():
        c_out_ref[0] = c_ref[...]
        n_out_ref[0] = n_ref[0:ML_HEADS, :]
        m_out_ref[0] = m_ref[...]


def _mlstm_prompt(h, w):
    b, t, d = h.shape
    L = ML_CHUNK_ROWS
    nt = t // L
    assert t % L == 0
    out, c, n, m = pl.pallas_call(
        _mlstm_prompt_kernel,
        grid=(b, nt),
        in_specs=[pl.BlockSpec((L, d), lambda i, j: (i * nt + j, 0))] + [_resident(w[k].shape) for k in _ML_KEYS],
        out_specs=[pl.BlockSpec((L, d), lambda i, j: (i * nt + j, 0)),
                   pl.BlockSpec((1, ML_HEADS, ML_DK, ML_DV), lambda i, j: (i, 0, 0, 0)),
                   pl.BlockSpec((1, ML_HEADS, ML_DK), lambda i, j: (i, 0, 0)),
                   pl.BlockSpec((1, SUBLANES, LANES), lambda i, j: (i, 0, 0))],
        out_shape=[jax.ShapeDtypeStruct((b * t, d), F32),
                   jax.ShapeDtypeStruct((b, ML_HEADS, ML_DK, ML_DV), F32),
                   jax.ShapeDtypeStruct((b, ML_HEADS, ML_DK), F32),
                   jax.ShapeDtypeStruct((b, SUBLANES, LANES), F32)],
        scratch_shapes=[pltpu.VMEM((ML_HEADS, ML_DK, ML_DV), F32), pltpu.VMEM((SUBLANES, ML_DK), F32),
                        pltpu.VMEM((SUBLANES, LANES), F32), pltpu.VMEM((L, ML_HEADS * ML_DV), F32)],
        compiler_params=_params("arbitrary", "arbitrary"),
        name="mlstm_prompt",
    )(h.reshape(b * t, d), *[w[k] for k in _ML_KEYS])
    return out.reshape(b, t, d), c, n, m[:, :ML_HEADS, 0]


def _mlstm_sample_pre_kernel(x_ref, n_ref, m_ref, gpre_ref, win_ref, wgc_ref, bgc_ref,
                             q_ref, kw_ref, v_ref, o_ref, wold_ref, scal_ref, nnew_ref):
    H = ML_HEADS
    xn = _rms(x_ref[...], gpre_ref[...]).astype(BF16)
    proj = _dot(xn, win_ref[...])
    gates = _dot(xn, wgc_ref[...]) + bgc_ref[...]
    v_ref[...] = proj[:, 2 * ML_HK:2 * ML_HK + H * ML_DV]
    o_ref[...] = proj[:, 2 * ML_HK + H * ML_DV:]
    scal_ref[...] = jnp.zeros_like(scal_ref)
    for h in range(H):
        q = proj[:, h * ML_DK:(h + 1) * ML_DK] * (ML_DK ** -0.5)
        k = proj[:, ML_HK + h * ML_DK:ML_HK + (h + 1) * ML_DK]
        n_prev = n_ref[:, h * ML_DK:(h + 1) * ML_DK]
        ig = gates[:, h:h + 1]
        lf = _log_sigmoid(gates[:, H + h:H + h + 1])
        a = lf + m_ref[:, h:h + 1]
        m_t = jnp.maximum(a, ig)
        w_old = jnp.exp(a - m_t)
        w_new = jnp.exp(ig - m_t)
        s = jnp.sum(q * k, axis=1, keepdims=True) * w_new
        qn = w_old * jnp.sum(q * n_prev, axis=1, keepdims=True) + s
        q_ref[:, h * ML_DK:(h + 1) * ML_DK] = q
        kw_ref[:, h * ML_DK:(h + 1) * ML_DK] = w_new * k
        nnew_ref[:, h * ML_DK:(h + 1) * ML_DK] = w_old * n_prev + w_new * k
        wold_ref[:, h * ML_DV:(h + 1) * ML_DV] = jnp.broadcast_to(w_old, (w_old.shape[0], ML_DV))
        scal_ref[:, h:h + 1] = w_old
        scal_ref[:, H + h:H + h + 1] = s
        scal_ref[:, 2 * H + h:2 * H + h + 1] = jnp.maximum(jnp.abs(qn), jnp.exp(-m_t))
        scal_ref[:, 3 * H + h:3 * H + h + 1] = m_t


def _mlstm_sample_state_kernel(c_ref, qt_ref, kt_ref, v_ref, wold_ref, cnew_ref, qc_ref):
    for b in range(c_ref.shape[0]):
        for h in range(ML_HEADS):
            c = c_ref[b, h]
            qc = qt_ref[0, h * ML_DK:(h + 1) * ML_DK, b:b + 1]
            kc = kt_ref[0, h * ML_DK:(h + 1) * ML_DK, b:b + 1]
            vr = v_ref[b:b + 1, h * ML_DV:(h + 1) * ML_DV]
            wo = wold_ref[b:b + 1, h * ML_DV:(h + 1) * ML_DV]
            qc_ref[b:b + 1, h * ML_DV:(h + 1) * ML_DV] = jnp.sum(qc * c, axis=0, keepdims=True)
            cnew_ref[b, h] = c * wo + kc * vr


def _mlstm_sample_post_kernel(x_ref, qc_ref, v_ref, o_ref, scal_ref, gpost_ref, wout_ref, out_ref, h_ref):
    H = ML_HEADS
    for h in range(H):
        sl = slice(h * ML_DV, (h + 1) * ML_DV)
        num = scal_ref[:, h:h + 1] * qc_ref[:, sl] + scal_ref[:, H + h:H + h + 1] * v_ref[:, sl]
        h_ref[:, sl] = num / scal_ref[:, 2 * H + h:2 * H + h + 1]
    y = _dot((_sigmoid(o_ref[...]) * h_ref[...]).astype(BF16), wout_ref[...])
    out_ref[...] = x_ref[...] + _rms(y, gpost_ref[...])


def _whole(shape):
    zeros = (0,) * len(shape)
    return pl.BlockSpec(shape, lambda *_: zeros)


def _mlstm_sample(h, c, n, m, w):
    b, d = h.shape
    H = ML_HEADS
    bb = SAMPLE_STATE_BATCH
    nb = b // bb
    assert b % bb == 0
    pre_in = [h, n.reshape(b, H * ML_DK), jnp.pad(m, ((0, 0), (0, LANES - H))),
              w["gpre"], w["win"], w["wgc"], w["bgc"]]
    pre_out = [(b, ML_HK), (b, ML_HK), (b, H * ML_DV), (b, H * ML_DV), (b, H * ML_DV), (b, LANES), (b, ML_HK)]
    q, kw, v, o, wold, scal, nnew = pl.pallas_call(
        _mlstm_sample_pre_kernel,
        grid=(1,),
        in_specs=[_resident(a.shape) for a in pre_in],
        out_specs=[_whole(s) for s in pre_out],
        out_shape=[jax.ShapeDtypeStruct(s, F32) for s in pre_out],
        compiler_params=_params("arbitrary"),
        name="mlstm_sample_pre",
    )(*pre_in)

    def cols(a):
        return a.reshape(nb, bb, ML_HK).transpose(0, 2, 1)

    cnew, qc = pl.pallas_call(
        _mlstm_sample_state_kernel,
        grid=(nb,),
        in_specs=[pl.BlockSpec((bb, H, ML_DK, ML_DV), lambda i: (i, 0, 0, 0)),
                  pl.BlockSpec((1, ML_HK, bb), lambda i: (i, 0, 0)),
                  pl.BlockSpec((1, ML_HK, bb), lambda i: (i, 0, 0)),
                  pl.BlockSpec((bb, H * ML_DV), lambda i: (i, 0)),
                  pl.BlockSpec((bb, H * ML_DV), lambda i: (i, 0))],
        out_specs=[pl.BlockSpec((bb, H, ML_DK, ML_DV), lambda i: (i, 0, 0, 0)),
                   pl.BlockSpec((bb, H * ML_DV), lambda i: (i, 0))],
        out_shape=[jax.ShapeDtypeStruct(c.shape, F32), jax.ShapeDtypeStruct((b, H * ML_DV), F32)],
        compiler_params=_params("arbitrary"),
        name="mlstm_sample_state",
    )(c, cols(q), cols(kw), v, wold)

    post_in = [h, qc, v, o, scal, w["gpost"], w["wout"]]
    out = pl.pallas_call(
        _mlstm_sample_post_kernel,
        grid=(1,),
        in_specs=[_resident(a.shape) for a in post_in],
        out_specs=_whole((b, d)),
        out_shape=jax.ShapeDtypeStruct((b, d), F32),
        scratch_shapes=[pltpu.VMEM((b, H * ML_DV), F32)],
        compiler_params=_params("arbitrary"),
        name="mlstm_sample_post",
    )(*post_in)
    return out, cnew, nnew.reshape(b, H, ML_DK), scal[:, 3 * H:4 * H]


def _rope_tables(pos):
    half = SW_HD // 2
    inv = ROPE_THETA ** (-jnp.arange(half, dtype=F32) / half)
    ang = pos[:, None] * inv[None, :]
    cos = jnp.tile(jnp.cos(ang), (1, LANES // half))
    sin = jnp.tile(jnp.concatenate([-jnp.sin(ang), jnp.sin(ang)], axis=1), (1, LANES // SW_HD))
    return cos, sin


def _first_half_lanes(rows):
    lane = lax.broadcasted_iota(jnp.int32, (rows, LANES), 1)
    return jnp.bitwise_and(lane, SW_HD - 1) < SW_HD // 2


def _rope_block(x, cos, sin, first_half):
    partner = jnp.where(first_half, pltpu.roll(x, LANES - SW_HD // 2, axis=1), pltpu.roll(x, SW_HD // 2, axis=1))
    return x * cos + partner * sin


def _swa_weights(w_in, sinks, w_out, g_pre, g_post):
    return dict(gpre=g_pre[None], gpost=g_post[None], win=w_in.astype(BF16), wout=w_out.astype(BF16), sinks=sinks)


def _swa_prompt_kernel(sink_ref, x_ref, cos_ref, sin_ref, gpre_ref, gpost_ref, win_ref, wout_ref,
                       out_ref, kout_ref, vout_ref, q_ref, k_ref, v_ref, o_ref, kf_ref):
    t = pl.program_id(1)
    rows = x_ref.shape[0]
    nsub = rows // WINDOW
    nq = SW_HEADS * SW_HD
    nkv = SW_KV_HEADS * SW_HD

    @pl.when(t == 0)
    def _():
        k_ref[0:WINDOW, :] = jnp.zeros((WINDOW, nkv), BF16)
        v_ref[0:WINDOW, :] = jnp.zeros((WINDOW, nkv), BF16)

    x = x_ref[...]
    xn = _rms(x, gpre_ref[...]).astype(BF16)
    proj = _dot(xn, win_ref[...])
    cos = cos_ref[...]
    sin = sin_ref[...]
    first_half = _first_half_lanes(rows)
    for c in range(nq // LANES):
        blk = _rope_block(proj[:, c * LANES:(c + 1) * LANES], cos, sin, first_half)
        q_ref[:, c * LANES:(c + 1) * LANES] = (blk * (SW_HD ** -0.5)).astype(BF16)
    for c in range(nkv // LANES):
        blk = _rope_block(proj[:, nq + c * LANES:nq + (c + 1) * LANES], cos, sin, first_half)
        kf_ref[:, c * LANES:(c + 1) * LANES] = blk
        k_ref[WINDOW:, c * LANES:(c + 1) * LANES] = blk.astype(BF16)
    v = proj[:, nq + nkv:]
    v_ref[WINDOW:, :] = v.astype(BF16)

    qi = lax.broadcasted_iota(jnp.int32, (WINDOW, 2 * WINDOW), 0)
    ki = lax.broadcasted_iota(jnp.int32, (WINDOW, 2 * WINDOW), 1)
    band = (ki >= qi) & (ki <= qi + WINDOW)
    for i in range(nsub):
        first_key = jnp.where(t * nsub + i > 0, 0, WINDOW)
        mask = band & (ki >= first_key)
        for h in range(SW_HEADS):
            kv = h // SW_GROUP
            qh = q_ref[i * WINDOW:(i + 1) * WINDOW, h * SW_HD:(h + 1) * SW_HD]
            kh = k_ref[i * WINDOW:(i + 2) * WINDOW, kv * SW_HD:(kv + 1) * SW_HD]
            vh = v_ref[i * WINDOW:(i + 2) * WINDOW, kv * SW_HD:(kv + 1) * SW_HD]
            s = jnp.where(mask, _dot_nt(qh, kh), -jnp.inf)
            sk = sink_ref[h]
            mx = jnp.maximum(jnp.max(s, axis=1, keepdims=True), sk)
            p = jnp.exp(s - mx)
            p = p / (jnp.sum(p, axis=1, keepdims=True) + jnp.exp(sk - mx))
            o_ref[i * WINDOW:(i + 1) * WINDOW, h * SW_HD:(h + 1) * SW_HD] = _dot(p.astype(BF16), vh)

    y = _dot(o_ref[...].astype(BF16), wout_ref[...])
    out_ref[...] = x + _rms(y, gpost_ref[...])
    k_ref[0:WINDOW, :] = k_ref[rows:rows + WINDOW, :]
    v_ref[0:WINDOW, :] = v_ref[rows:rows + WINDOW, :]

    @pl.when(t == pl.num_programs(1) - 1)
    def _():
        kout_ref[0] = kf_ref[rows - WINDOW:, :]
        vout_ref[0] = v[rows - WINDOW:, :]


def _swa_prompt(h, w):
    b, t, d = h.shape
    rows = SW_ROWS
    nt = t // rows
    assert t % rows == 0 and rows % WINDOW == 0 and t >= WINDOW
    nq = SW_HEADS * SW_HD
    nkv = SW_KV_HEADS * SW_HD
    cos, sin = _rope_tables(jnp.arange(t, dtype=F32))
    out, k, v = pl.pallas_call(
        _swa_prompt_kernel,
        grid_spec=pltpu.PrefetchScalarGridSpec(
            num_scalar_prefetch=1,
            grid=(b, nt),
            in_specs=[pl.BlockSpec((rows, d), lambda i, j, s: (i * nt + j, 0)),
                      pl.BlockSpec((rows, LANES), lambda i, j, s: (j, 0)),
                      pl.BlockSpec((rows, LANES), lambda i, j, s: (j, 0))]
            + [_resident(w[k_].shape) for k_ in ("gpre", "gpost", "win", "wout")],
            out_specs=[pl.BlockSpec((rows, d), lambda i, j, s: (i * nt + j, 0)),
                       pl.BlockSpec((1, WINDOW, nkv), lambda i, j, s: (i, 0, 0)),
                       pl.BlockSpec((1, WINDOW, nkv), lambda i, j, s: (i, 0, 0))],
            scratch_shapes=[pltpu.VMEM((rows, nq), BF16), pltpu.VMEM((WINDOW + rows, nkv), BF16),
                            pltpu.VMEM((WINDOW + rows, nkv), BF16), pltpu.VMEM((rows, nq), F32),
                            pltpu.VMEM((rows, nkv), F32)]),
        out_shape=[jax.ShapeDtypeStruct((b * t, d), F32),
                   jax.ShapeDtypeStruct((b, WINDOW, nkv), F32),
                   jax.ShapeDtypeStruct((b, WINDOW, nkv), F32)],
        compiler_params=_params("arbitrary", "arbitrary"),
        name="swa_prompt",
    )(w["sinks"], h.reshape(b * t, d), cos, sin, *[w[k_] for k_ in ("gpre", "gpost", "win", "wout")])
    shp = (b, WINDOW, SW_KV_HEADS, SW_HD)
    return out.reshape(b, t, d), k.reshape(shp), v.reshape(shp)


def _swa_sample_pre_kernel(x_ref, cos_ref, sin_ref, gpre_ref, win_ref, q_ref, k_ref, v_ref):
    nq = SW_HEADS * SW_HD
    nkv = SW_KV_HEADS * SW_HD
    rows = x_ref.shape[0]
    xn = _rms(x_ref[...], gpre_ref[...]).astype(BF16)
    proj = _dot(xn, win_ref[...])
    cos = cos_ref[...]
    sin = sin_ref[...]
    first_half = _first_half_lanes(rows)
    for c in range(nq // LANES):
        blk = _rope_block(proj[:, c * LANES:(c + 1) * LANES], cos, sin, first_half)
        q_ref[:, c * LANES:(c + 1) * LANES] = blk * (SW_HD ** -0.5)
    for c in range(nkv // LANES):
        k_ref[:, c * LANES:(c + 1) * LANES] = _rope_block(
            proj[:, nq + c * LANES:nq + (c + 1) * LANES], cos, sin, first_half)
    v_ref[...] = proj[:, nq + nkv:]


def _swa_sample_attn_kernel(q_ref, kbuf_ref, vbuf_ref, knew_ref, vnew_ref, sink_ref, o_ref):
    nbuf = kbuf_ref.shape[1]
    grp = lax.shift_right_logical(lax.broadcasted_iota(jnp.int32, (SW_HEADS, 1), 0),
                                  jnp.int32(SW_GROUP.bit_length() - 1))
    sk = sink_ref[:, 0:1]
    dist = nbuf - lax.broadcasted_iota(jnp.int32, (1, nbuf), 1)
    valid = (dist >= 0) & (dist <= WINDOW)
    for b in range(q_ref.shape[0]):
        q = q_ref[b]
        qb = q.astype(BF16)
        s = jnp.zeros((SW_HEADS, nbuf), F32)
        s_new = jnp.zeros((SW_HEADS, 1), F32)
        for kv in range(SW_KV_HEADS):
            kh = kbuf_ref[b, :, kv * SW_HD:(kv + 1) * SW_HD].astype(BF16)
            s = jnp.where(grp == kv, _dot_nt(qb, kh), s)
            kn = knew_ref[b, :, kv * SW_HD:(kv + 1) * SW_HD]
            s_new = jnp.where(grp == kv, jnp.sum(q * kn, axis=1, keepdims=True), s_new)
        s = jnp.where(valid, s, -jnp.inf)
        mx = jnp.maximum(jnp.maximum(jnp.max(s, axis=1, keepdims=True), s_new), sk)
        p = jnp.exp(s - mx)
        p_new = jnp.exp(s_new - mx)
        den = jnp.sum(p, axis=1, keepdims=True) + p_new + jnp.exp(sk - mx)
        pb = p.astype(BF16)
        o = jnp.zeros((SW_HEADS, SW_HD), F32)
        for kv in range(SW_KV_HEADS):
            vh = vbuf_ref[b, :, kv * SW_HD:(kv + 1) * SW_HD].astype(BF16)
            vn = vnew_ref[b, :, kv * SW_HD:(kv + 1) * SW_HD]
            o = jnp.where(grp == kv, _dot(pb, vh) + p_new * vn, o)
        o_ref[b] = o / den


def _out_proj_kernel(x_ref, a_ref, gpost_ref, wout_ref, out_ref):
    y = _dot(a_ref[...].astype(BF16), wout_ref[...])
    out_ref[...] = x_ref[...] + _rms(y, gpost_ref[...])


def _swa_sample(h, kbuf, vbuf, w):
    b, d = h.shape
    nbuf = kbuf.shape[1]
    nq = SW_HEADS * SW_HD
    nkv = SW_KV_HEADS * SW_HD
    bb = SAMPLE_STATE_BATCH
    assert b % bb == 0
    cos, sin = _rope_tables(jnp.full((1,), PAST_LEN, F32))
    pre_in = [h, cos, sin, w["gpre"], w["win"]]
    pre_out = [(b, nq), (b, nkv), (b, nkv)]
    q, knew, vnew = pl.pallas_call(
        _swa_sample_pre_kernel,
        grid=(1,),
        in_specs=[_resident(a.shape) for a in pre_in],
        out_specs=[_whole(s) for s in pre_out],
        out_shape=[jax.ShapeDtypeStruct(s, F32) for s in pre_out],
        compiler_params=_params("arbitrary"),
        name="swa_sample_pre",
    )(*pre_in)

    sinks = jnp.broadcast_to(w["sinks"][:, None], (SW_HEADS, LANES))
    o = pl.pallas_call(
        _swa_sample_attn_kernel,
        grid=(b // bb,),
        in_specs=[pl.BlockSpec((bb, SW_HEADS, SW_HD), lambda i: (i, 0, 0)),
                  pl.BlockSpec((bb, nbuf, nkv), lambda i: (i, 0, 0)),
                  pl.BlockSpec((bb, nbuf, nkv), lambda i: (i, 0, 0)),
                  pl.BlockSpec((bb, 1, nkv), lambda i: (i, 0, 0)),
                  pl.BlockSpec((bb, 1, nkv), lambda i: (i, 0, 0)),
                  _resident((SW_HEADS, LANES))],
        out_specs=pl.BlockSpec((bb, SW_HEADS, SW_HD), lambda i: (i, 0, 0)),
        out_shape=jax.ShapeDtypeStruct((b, SW_HEADS, SW_HD), F32),
        compiler_params=_params("arbitrary"),
        name="swa_sample_attn",
    )(q.reshape(b, SW_HEADS, SW_HD), kbuf.reshape(b, nbuf, nkv), vbuf.reshape(b, nbuf, nkv),
      knew.reshape(b, 1, nkv), vnew.reshape(b, 1, nkv), sinks)

    post_in = [h, o.reshape(b, nq), w["gpost"], w["wout"]]
    out = pl.pallas_call(
        _out_proj_kernel,
        grid=(1,),
        in_specs=[_resident(a.shape) for a in post_in],
        out_specs=_whole((b, d)),
        out_shape=jax.ShapeDtypeStruct((b, d), F32),
        compiler_params=_params("arbitrary"),
        name="swa_sample_post",
    )(*post_in)
    shp = (b, 1, SW_KV_HEADS, SW_HD)
    k_all = jnp.concatenate([kbuf, knew.reshape(shp)], axis=1)[:, -nbuf:]
    v_all = jnp.concatenate([vbuf, vnew.reshape(shp)], axis=1)[:, -nbuf:]
    return out, k_all, v_all


def kernel(x_prompt, x_sample, p_prompt, p_sample, state_mlstm_C, state_mlstm_n, state_mlstm_m, cache_swa_k, cache_swa_v, state_conv, g_pre_mix, g_post_mix, g_pre_ffn, g_post_ffn, g_ple, ml_w_in, ml_b_gate, ml_w_out, sw_w_in, sw_sinks, sw_w_out, ffn_w_up, ffn_conv_w, ffn_conv_b, ffn_w_down, ple_w_proj, ple_w_gate):
    depth = g_pre_mix.shape[0]
    hp = x_prompt
    hs = x_sample[:, 0, :]
    cp, np_, mp, cs, ns, ms = [], [], [], [], [], []
    kp, vp, ks, vs = [], [], [], []
    convp, convs = [], []
    for i in range(depth):
        j = i // 2
        if i % 2 == 0:
            w = _mlstm_weights(ml_w_in[j], ml_b_gate[j], ml_w_out[j], g_pre_mix[i], g_post_mix[i])
            hp, c1, n1, m1 = _mlstm_prompt(hp, w)
            hs, c2, n2, m2 = _mlstm_sample(hs, state_mlstm_C[j], state_mlstm_n[j], state_mlstm_m[j], w)
            cp.append(c1); np_.append(n1); mp.append(m1)
            cs.append(c2); ns.append(n2); ms.append(m2)
        else:
            w = _swa_weights(sw_w_in[j], sw_sinks[j], sw_w_out[j], g_pre_mix[i], g_post_mix[i])
            hp, k1, v1 = _swa_prompt(hp, w)
            hs, k2, v2 = _swa_sample(hs, cache_swa_k[j], cache_swa_v[j], w)
            kp.append(k1); vp.append(v1); ks.append(k2); vs.append(v2)
        w = _ffn_weights(ffn_w_up[i], ffn_conv_w[i], ffn_conv_b[i], ffn_w_down[i], g_pre_ffn[i], g_post_ffn[i],
                         g_ple[i], ple_w_gate[i], ple_w_proj[i])
        hp, cvp = _ffn_prompt(hp, p_prompt[i], w)
        hs, cvs = _ffn_sample(hs, p_sample[i][:, 0, :], state_conv[i], w)
        convp.append(cvp); convs.append(cvs)
    return (hp, hs[:, None, :],
            jnp.stack(cp), jnp.stack(np_), jnp.stack(mp), jnp.stack(kp), jnp.stack(vp), jnp.stack(convp),
            jnp.stack(cs), jnp.stack(ns), jnp.stack(ms), jnp.stack(ks), jnp.stack(vs), jnp.stack(convs))
```

```python
import math

import jax
import jax.numpy as jnp
from jax import lax
from jax.experimental import pallas as pl
from jax.experimental.pallas import tpu as pltpu

F32 = jnp.float32
BF16 = jnp.bfloat16

EPS = 1e-6
PLE_DIM = 256
ML_HEADS = 4
ML_DK = 128
ML_DV = 256
ML_HK = ML_HEADS * ML_DK
ML_QKVO = 2 * ML_HK + 2 * ML_HEADS * ML_DV
SW_HEADS = 16
SW_KV_HEADS = 4
SW_HD = 64
SW_GROUP = SW_HEADS // SW_KV_HEADS
SW_NQ = SW_HEADS * SW_HD
SW_NKV = SW_KV_HEADS * SW_HD
WINDOW = 128
ROPE_THETA = 10000.0
PAST_LEN = 8192
LOG2E = math.log2(math.e)

LANES = 128
SUBLANES = 8
FFN_CHUNK = 256
FFN_ROWS = 512
ML_CHUNK_ROWS = 256
SW_ROWS = 256
SAMPLE_STATE_BATCH = 8
VMEM_LIMIT = 56 * 1024 * 1024

assert SW_KV_HEADS == 4 and SW_GROUP == 4 and 2 * SW_HD == LANES and WINDOW == LANES


def _params(*sem):
    return pltpu.CompilerParams(dimension_semantics=sem, vmem_limit_bytes=VMEM_LIMIT)


def _resident(shape):
    zeros = (0,) * len(shape)
    return pl.BlockSpec(shape, lambda *_: zeros, pipeline_mode=pl.Buffered(1))


def _layer(a, layer):
    zeros = (0,) * (a.ndim - 1)
    return pl.BlockSpec((None,) + a.shape[1:], lambda *_: (layer,) + zeros, pipeline_mode=pl.Buffered(1))


def _whole(shape):
    zeros = (0,) * len(shape)
    return pl.BlockSpec(shape, lambda *_: zeros)


def _rms(x, g):
    return x * lax.rsqrt(jnp.mean(x * x, axis=-1, keepdims=True) + EPS) * g


def _sigmoid(x):
    return 1.0 / (1.0 + jnp.exp(-x))


def _log_sigmoid(x):
    return jnp.minimum(x, 0.0) - jnp.log1p(jnp.exp(-jnp.abs(x)))


def _dot(a, b):
    return jnp.dot(a, b, preferred_element_type=F32)


def _dot_nt(a, b):
    return lax.dot_general(a, b, (((1,), (1,)), ((), ())), preferred_element_type=F32)


def _dot_tn(a, b):
    return lax.dot_general(a, b, (((0,), (0,)), ((), ())), preferred_element_type=F32)


def _cumsum(x, axis):
    n = x.shape[axis]
    idx = lax.broadcasted_iota(jnp.int32, x.shape, axis)
    s = 1
    while s < n:
        x = x + jnp.where(idx >= s, pltpu.roll(x, s, axis=axis), 0.0)
        s *= 2
    return x


def _ffn_tail(x, f, p, gpost_ref, gple_ref, wgate_ref, wproj_ref):
    x1 = x + _rms(f, gpost_ref[...])
    gate = _sigmoid(_dot(_rms(x1, gple_ref[...]).astype(BF16), wgate_ref[...]))
    return x1 + gate * _dot(p.astype(BF16), wproj_ref[...])


def _ffn_chunk_cols(j, hidden):
    return (slice(j * FFN_CHUNK, (j + 1) * FFN_CHUNK),
            slice(hidden + j * FFN_CHUNK, hidden + (j + 1) * FFN_CHUNK))


def _ffn_prompt_kernel(x_ref, p_ref, gpre_ref, gpost_ref, gple_ref, wup_ref, cw_ref, cb_ref, wdown_ref,
                       wgate_ref, wproj_ref, out_ref, conv_ref, carry_ref, xp_ref, pp_ref, acc_ref):
    t = pl.program_id(1)
    rows = x_ref.shape[0]
    hidden = wdown_ref.shape[0]
    nch = hidden // FFN_CHUNK
    S = SUBLANES
    G = rows // S

    @pl.when(t == 0)
    def _():
        carry_ref[...] = jnp.zeros_like(carry_ref)

    def interleaved(i):
        return pl.ds(((S * i) % G) * S + (S * i) // G, S, stride=S)

    for i in range(G):
        for c in range(xp_ref.shape[0]):
            xp_ref[c, interleaved(i), :] = x_ref[S * i:S * (i + 1), c * LANES:(c + 1) * LANES]
        for c in range(pp_ref.shape[0]):
            pp_ref[c, interleaved(i), :] = p_ref[S * i:S * (i + 1), c * LANES:(c + 1) * LANES]
    x = jnp.concatenate([xp_ref[c] for c in range(xp_ref.shape[0])], axis=1)
    p = jnp.concatenate([pp_ref[c] for c in range(pp_ref.shape[0])], axis=1)
    xn = _rms(x, gpre_ref[...]).astype(BF16)
    first = lax.broadcasted_iota(jnp.int32, (S, FFN_CHUNK), 0) == 0

    def conv(u, cols):
        prev = carry_ref[:, cols]
        back1 = jnp.where(first, prev[2 * S - 1:2 * S], pltpu.roll(u[rows - S:], 1, axis=0))
        back2 = jnp.where(first, prev[S - 1:S], pltpu.roll(u[rows - 2 * S:rows - S], 1, axis=0))
        u1 = jnp.concatenate([back1, u[:rows - S]], axis=0)
        u2 = jnp.concatenate([back2, back1, u[:rows - 2 * S]], axis=0)
        carry_ref[:, cols] = u[rows - 2 * S:]
        cw = cw_ref[:, cols]
        return cb_ref[:, cols] + u2 * cw[0:1] + u1 * cw[1:2] + u * cw[2:3]

    def up(j):
        return [_dot(xn, wup_ref[:, cols]) for cols in _ffn_chunk_cols(j, hidden)]

    u = up(0)
    for j in range(nch):
        u_next = up(j + 1) if j + 1 < nch else None
        gcols, vcols = _ffn_chunk_cols(j, hidden)
        cg = conv(u[0], gcols)
        h = cg * _sigmoid(cg) * conv(u[1], vcols)
        d = _dot(h.astype(BF16), wdown_ref[j * FFN_CHUNK:(j + 1) * FFN_CHUNK, :])
        if j == 0:
            acc_ref[...] = d
        else:
            acc_ref[...] += d
        u = u_next

    res = _ffn_tail(x, acc_ref[...], p, gpost_ref, gple_ref, wgate_ref, wproj_ref)
    for c in range(xp_ref.shape[0]):
        xp_ref[c] = res[:, c * LANES:(c + 1) * LANES]
    for i in range(G):
        for c in range(xp_ref.shape[0]):
            out_ref[S * i:S * (i + 1), c * LANES:(c + 1) * LANES] = xp_ref[c, interleaved(i), :]

    @pl.when(t == pl.num_programs(1) - 1)
    def _():
        conv_ref[0] = carry_ref[...]


def _ffn_sample_kernel(x_ref, p_ref, prev0_ref, prev1_ref, gpre_ref, gpost_ref, gple_ref, wup_ref, cw_ref,
                       cb_ref, wdown_ref, wgate_ref, wproj_ref, out_ref, u_ref, acc_ref):
    hidden = wdown_ref.shape[0]
    x = x_ref[...]
    xn = _rms(x, gpre_ref[...]).astype(BF16)

    def conv(cols):
        u = _dot(xn, wup_ref[:, cols])
        u_ref[:, cols] = u
        cw = cw_ref[:, cols]
        return cb_ref[:, cols] + prev0_ref[:, cols] * cw[0:1] + prev1_ref[:, cols] * cw[1:2] + u * cw[2:3]

    for j in range(hidden // FFN_CHUNK):
        gcols, vcols = _ffn_chunk_cols(j, hidden)
        cg = conv(gcols)
        h = cg * _sigmoid(cg) * conv(vcols)
        d = _dot(h.astype(BF16), wdown_ref[j * FFN_CHUNK:(j + 1) * FFN_CHUNK, :])
        if j == 0:
            acc_ref[...] = d
        else:
            acc_ref[...] += d
    out_ref[...] = _ffn_tail(x, acc_ref[...], p_ref[...], gpost_ref, gple_ref, wgate_ref, wproj_ref)


_FFN_KEYS = ("g_pre_ffn", "g_post_ffn", "g_ple", "ffn_wup", "ffn_cw", "ffn_cb", "ffn_wdown", "ple_wgate",
             "ple_wproj")


def _ffn_prompt(h, p_all, w, layer):
    b, t, d = h.shape
    f2 = w["ffn_wup"].shape[-1]
    nt = t // FFN_ROWS
    assert t % FFN_ROWS == 0 and (FFN_ROWS // SUBLANES) % SUBLANES == 0
    out, conv = pl.pallas_call(
        _ffn_prompt_kernel,
        grid=(b, nt),
        in_specs=[pl.BlockSpec((FFN_ROWS, d), lambda i, j: (i * nt + j, 0)),
                  pl.BlockSpec((FFN_ROWS, PLE_DIM), lambda i, j: ((layer * b + i) * nt + j, 0))]
        + [_layer(w[k], layer) for k in _FFN_KEYS],
        out_specs=[pl.BlockSpec((FFN_ROWS, d), lambda i, j: (i * nt + j, 0)),
                   pl.BlockSpec((1, 2 * SUBLANES, f2), lambda i, j: (i, 0, 0))],
        out_shape=[jax.ShapeDtypeStruct((b * t, d), F32),
                   jax.ShapeDtypeStruct((b, 2 * SUBLANES, f2), F32)],
        scratch_shapes=[pltpu.VMEM((2 * SUBLANES, f2), F32),
                        pltpu.VMEM((d // LANES, FFN_ROWS, LANES), F32),
                        pltpu.VMEM((PLE_DIM // LANES, FFN_ROWS, LANES), F32), pltpu.VMEM((FFN_ROWS, d), F32)],
        compiler_params=_params("arbitrary", "arbitrary"),
        name="ffn_prompt",
    )(h.reshape(b * t, d), p_all.reshape(-1, PLE_DIM), *[w[k] for k in _FFN_KEYS])
    return out.reshape(b, t, d), conv[:, SUBLANES - 1::SUBLANES, :]


def _ffn_sample(h, p, prev, w, layer):
    b, d = h.shape
    f2 = w["ffn_wup"].shape[-1]
    args = [h, p, prev[:, 0], prev[:, 1]]
    out, u = pl.pallas_call(
        _ffn_sample_kernel,
        grid=(1,),
        in_specs=[_resident(a.shape) for a in args] + [_layer(w[k], layer) for k in _FFN_KEYS],
        out_specs=[_whole((b, d)), _whole((b, f2))],
        out_shape=[jax.ShapeDtypeStruct((b, d), F32), jax.ShapeDtypeStruct((b, f2), F32)],
        scratch_shapes=[pltpu.VMEM((b, d), F32)],
        compiler_params=_params("arbitrary"),
        name="ffn_sample",
    )(*args, *[w[k] for k in _FFN_KEYS])
    return out, jnp.stack([prev[:, 1], u], axis=1)


_ML_KEYS = ("g_pre_mix", "g_post_mix", "ml_win", "ml_wgc", "ml_bgc", "ml_wgr", "ml_bgr", "ml_wout")


def _mlstm_prompt_kernel(x_ref, gpre_ref, gpost_ref, win_ref, wgc_ref, bgc_ref, wgr_ref, bgr_ref, wout_ref,
                         out_ref, c_out_ref, n_out_ref, m_out_ref, c_ref, n_ref, m_ref, h_ref):
    t = pl.program_id(1)
    L = x_ref.shape[0]

    @pl.when(t == 0)
    def _():
        c_ref[...] = jnp.zeros_like(c_ref)
        n_ref[...] = jnp.zeros_like(n_ref)
        m_ref[...] = jnp.zeros_like(m_ref)

    x = x_ref[...]
    xn = _rms(x, gpre_ref[...]).astype(BF16)
    proj = _dot(xn, win_ref[...])
    gc = _dot(xn, wgc_ref[...]) + bgc_ref[...]
    gr = _dot_nt(wgr_ref[...], xn) + bgr_ref[:, 0:1]
    is_forget_r = lax.broadcasted_iota(jnp.int32, gr.shape, 0) >= ML_HEADS
    bc = _cumsum(_log_sigmoid(gc), 0)
    br = _cumsum(jnp.where(is_forget_r, _log_sigmoid(gr), 0.0), 1)
    causal = lax.broadcasted_iota(jnp.int32, (L, L), 0) >= lax.broadcasted_iota(jnp.int32, (L, L), 1)

    def head_inputs(h):
        q = proj[:, h * ML_DK:(h + 1) * ML_DK] * (ML_DK ** -0.5)
        k = proj[:, ML_HK + h * ML_DK:ML_HK + (h + 1) * ML_DK]
        v = proj[:, 2 * ML_HK + h * ML_DV:2 * ML_HK + (h + 1) * ML_DV]
        return q, k, v.astype(BF16)

    def head_matmuls(h):
        q, k, _ = head_inputs(h)
        qb = q.astype(BF16)
        return _dot_nt(qb, k.astype(BF16)), _dot(qb, c_ref[h].astype(BF16))

    ahead = head_matmuls(0)
    for h in range(ML_HEADS):
        q, k, vb = head_inputs(h)
        qk, qc = ahead
        if h + 1 < ML_HEADS:
            ahead = head_matmuls(h + 1)
        ig_r = gr[h:h + 1, :]
        b_r = br[ML_HEADS + h:ML_HEADS + h + 1, :]
        ig_c = gc[:, h:h + 1]
        b_c = bc[:, ML_HEADS + h:ML_HEADS + h + 1]
        m_prev = m_ref[h:h + 1, 0:1]
        c_prev = c_ref[h]
        n_prev = n_ref[h:h + 1, :]

        d = jnp.where(causal, b_c + (ig_r - b_r), -jnp.inf)
        a = b_c + m_prev
        m_t = jnp.maximum(a, jnp.max(d, axis=1, keepdims=True))
        w_inter = jnp.exp(a - m_t)
        s = qk * jnp.exp(d - m_t)
        num = w_inter * qc + _dot(s.astype(BF16), vb)
        qn = w_inter * jnp.sum(q * n_prev, axis=1, keepdims=True) + jnp.sum(s, axis=1, keepdims=True)
        h_ref[:, h * ML_DV:(h + 1) * ML_DV] = num / jnp.maximum(jnp.abs(qn), jnp.exp(-m_t))

        b_last = b_r[:, L - 1:L]
        g_r = b_last - b_r + ig_r
        g_c = b_last - b_c + ig_c
        m_new = jnp.maximum(b_last + m_prev, jnp.max(g_r, axis=1, keepdims=True))
        w_old = jnp.exp(b_last + m_prev - m_new)
        wk = jnp.exp(g_c - m_new) * k
        c_ref[h] = w_old * c_prev + _dot_tn(wk.astype(BF16), vb)
        n_ref[h:h + 1, :] = w_old * n_prev + jnp.sum(wk, axis=0, keepdims=True)
        m_ref[h:h + 1, :] = jnp.broadcast_to(m_new, (1, LANES))

    o = proj[:, 2 * ML_HK + ML_HEADS * ML_DV:]
    y = _dot((_sigmoid(o) * h_ref[...]).astype(BF16), wout_ref[...])
    out_ref[...] = x + _rms(y, gpost_ref[...])

    @pl.when(t == pl.num_programs(1) - 1)
    def _():
        c_out_ref[0] = c_ref[...]
        n_out_ref[0] = n_ref[0:ML_HEADS, :]
        m_out_ref[0] = m_ref[...]


def _ml_layers(layer):
    return [2 * layer, 2 * layer] + [layer] * (len(_ML_KEYS) - 2)


def _mlstm_prompt(h, w, layer):
    b, t, d = h.shape
    L = ML_CHUNK_ROWS
    nt = t // L
    assert t % L == 0
    out, c, n, m = pl.pallas_call(
        _mlstm_prompt_kernel,
        grid=(b, nt),
        in_specs=[pl.BlockSpec((L, d), lambda i, j: (i * nt + j, 0))]
        + [_layer(w[k], l) for k, l in zip(_ML_KEYS, _ml_layers(layer))],
        out_specs=[pl.BlockSpec((L, d), lambda i, j: (i * nt + j, 0)),
                   pl.BlockSpec((1, ML_HEADS, ML_DK, ML_DV), lambda i, j: (i, 0, 0, 0)),
                   pl.BlockSpec((1, ML_HEADS, ML_DK), lambda i, j: (i, 0, 0)),
                   pl.BlockSpec((1, SUBLANES, LANES), lambda i, j: (i, 0, 0))],
        out_shape=[jax.ShapeDtypeStruct((b * t, d), F32),
                   jax.ShapeDtypeStruct((b, ML_HEADS, ML_DK, ML_DV), F32),
                   jax.ShapeDtypeStruct((b, ML_HEADS, ML_DK), F32),
                   jax.ShapeDtypeStruct((b, SUBLANES, LANES), F32)],
        scratch_shapes=[pltpu.VMEM((ML_HEADS, ML_DK, ML_DV), F32), pltpu.VMEM((SUBLANES, ML_DK), F32),
                        pltpu.VMEM((SUBLANES, LANES), F32), pltpu.VMEM((L, ML_HEADS * ML_DV), F32)],
        compiler_params=_params("arbitrary", "arbitrary"),
        name="mlstm_prompt",
    )(h.reshape(b * t, d), *[w[k] for k in _ML_KEYS])
    return out.reshape(b, t, d), c, n, m[:, :ML_HEADS, 0]


def _mlstm_sample_pre_kernel(x_ref, n_ref, m_ref, gpre_ref, win_ref, wgc_ref, bgc_ref,
                             q_ref, kw_ref, v_ref, o_ref, wold_ref, scal_ref, nnew_ref):
    H = ML_HEADS
    xn = _rms(x_ref[...], gpre_ref[...]).astype(BF16)
    proj = _dot(xn, win_ref[...])
    gates = _dot(xn, wgc_ref[...]) + bgc_ref[...]
    v_ref[...] = proj[:, 2 * ML_HK:2 * ML_HK + H * ML_DV]
    o_ref[...] = proj[:, 2 * ML_HK + H * ML_DV:]
    scal_ref[...] = jnp.zeros_like(scal_ref)
    for h in range(H):
        q = proj[:, h * ML_DK:(h + 1) * ML_DK] * (ML_DK ** -0.5)
        k = proj[:, ML_HK + h * ML_DK:ML_HK + (h + 1) * ML_DK]
        n_prev = n_ref[:, h * ML_DK:(h + 1) * ML_DK]
        ig = gates[:, h:h + 1]
        lf = _log_sigmoid(gates[:, H + h:H + h + 1])
        a = lf + m_ref[:, h:h + 1]
        m_t = jnp.maximum(a, ig)
        w_old = jnp.exp(a - m_t)
        w_new = jnp.exp(ig - m_t)
        s = jnp.sum(q * k, axis=1, keepdims=True) * w_new
        qn = w_old * jnp.sum(q * n_prev, axis=1, keepdims=True) + s
        q_ref[:, h * ML_DK:(h + 1) * ML_DK] = q
        kw_ref[:, h * ML_DK:(h + 1) * ML_DK] = w_new * k
        nnew_ref[:, h * ML_DK:(h + 1) * ML_DK] = w_old * n_prev + w_new * k
        wold_ref[:, h * ML_DV:(h + 1) * ML_DV] = jnp.broadcast_to(w_old, (w_old.shape[0], ML_DV))
        scal_ref[:, h:h + 1] = w_old
        scal_ref[:, H + h:H + h + 1] = s
        scal_ref[:, 2 * H + h:2 * H + h + 1] = jnp.maximum(jnp.abs(qn), jnp.exp(-m_t))
        scal_ref[:, 3 * H + h:3 * H + h + 1] = m_t


def _mlstm_sample_state_kernel(c_ref, qt_ref, kt_ref, v_ref, wold_ref, *rest):
    cnew_ref, qc_ref = rest[-2:]
    for b in range(c_ref.shape[0]):
        for h in range(ML_HEADS):
            c = c_ref[b, h]
            qc = qt_ref[0, h * ML_DK:(h + 1) * ML_DK, b:b + 1]
            kc = kt_ref[0, h * ML_DK:(h + 1) * ML_DK, b:b + 1]
            vr = v_ref[b:b + 1, h * ML_DV:(h + 1) * ML_DV]
            wo = wold_ref[b:b + 1, h * ML_DV:(h + 1) * ML_DV]
            qc_ref[b:b + 1, h * ML_DV:(h + 1) * ML_DV] = jnp.sum(qc * c, axis=0, keepdims=True)
            cnew_ref[b, h] = c * wo + kc * vr


def _mlstm_sample_post_kernel(x_ref, qc_ref, v_ref, o_ref, scal_ref, gpost_ref, wout_ref, out_ref, h_ref):
    H = ML_HEADS
    for h in range(H):
        sl = slice(h * ML_DV, (h + 1) * ML_DV)
        num = scal_ref[:, h:h + 1] * qc_ref[:, sl] + scal_ref[:, H + h:H + h + 1] * v_ref[:, sl]
        h_ref[:, sl] = num / scal_ref[:, 2 * H + h:2 * H + h + 1]
    y = _dot((_sigmoid(o_ref[...]) * h_ref[...]).astype(BF16), wout_ref[...])
    out_ref[...] = x_ref[...] + _rms(y, gpost_ref[...])


def _aliased(stack_prev, args, in_specs):
    if stack_prev is None:
        return {}
    args.append(stack_prev)
    in_specs.append(pl.BlockSpec(memory_space=pl.ANY))
    return {len(args) - 1: 0}


def _mlstm_sample(h, c_all, n, m, w, layer, c_stack):
    b, d = h.shape
    H = ML_HEADS
    bb = SAMPLE_STATE_BATCH
    nb = b // bb
    assert b % bb == 0
    pre_in = [h, n.reshape(b, H * ML_DK), jnp.pad(m, ((0, 0), (0, LANES - H)))]
    pre_keys = ("g_pre_mix", "ml_win", "ml_wgc", "ml_bgc")
    pre_layers = (2 * layer, layer, layer, layer)
    pre_out = [(b, ML_HK), (b, ML_HK), (b, H * ML_DV), (b, H * ML_DV), (b, H * ML_DV), (b, LANES), (b, ML_HK)]
    q, kw, v, o, wold, scal, nnew = pl.pallas_call(
        _mlstm_sample_pre_kernel,
        grid=(1,),
        in_specs=[_resident(a.shape) for a in pre_in] + [_layer(w[k], l) for k, l in zip(pre_keys, pre_layers)],
        out_specs=[_whole(s) for s in pre_out],
        out_shape=[jax.ShapeDtypeStruct(s, F32) for s in pre_out],
        compiler_params=_params("arbitrary"),
        name="mlstm_sample_pre",
    )(*pre_in, *[w[k] for k in pre_keys])

    def cols(a):
        return a.reshape(nb, bb, ML_HK).transpose(0, 2, 1)

    state_args = [c_all, cols(q), cols(kw), v, wold]
    state_specs = [pl.BlockSpec((None, bb, H, ML_DK, ML_DV), lambda i: (layer, i, 0, 0, 0)),
                   pl.BlockSpec((1, ML_HK, bb), lambda i: (i, 0, 0)),
                   pl.BlockSpec((1, ML_HK, bb), lambda i: (i, 0, 0)),
                   pl.BlockSpec((bb, H * ML_DV), lambda i: (i, 0)),
                   pl.BlockSpec((bb, H * ML_DV), lambda i: (i, 0))]
    aliases = _aliased(c_stack, state_args, state_specs)
    c_stack, qc = pl.pallas_call(
        _mlstm_sample_state_kernel,
        grid=(nb,),
        in_specs=state_specs,
        out_specs=[pl.BlockSpec((None, bb, H, ML_DK, ML_DV), lambda i: (layer, i, 0, 0, 0)),
                   pl.BlockSpec((bb, H * ML_DV), lambda i: (i, 0))],
        out_shape=[jax.ShapeDtypeStruct(c_all.shape, F32), jax.ShapeDtypeStruct((b, H * ML_DV), F32)],
        input_output_aliases=aliases,
        compiler_params=_params("arbitrary"),
        name="mlstm_sample_state",
    )(*state_args)

    post_in = [h, qc, v, o, scal]
    out = pl.pallas_call(
        _mlstm_sample_post_kernel,
        grid=(1,),
        in_specs=[_resident(a.shape) for a in post_in]
        + [_layer(w["g_post_mix"], 2 * layer), _layer(w["ml_wout"], layer)],
        out_specs=_whole((b, d)),
        out_shape=jax.ShapeDtypeStruct((b, d), F32),
        scratch_shapes=[pltpu.VMEM((b, H * ML_DV), F32)],
        compiler_params=_params("arbitrary"),
        name="mlstm_sample_post",
    )(*post_in, w["g_post_mix"], w["ml_wout"])
    return out, c_stack, nnew.reshape(b, H, ML_DK), scal[:, 3 * H:4 * H]


def _rope_tables(pos):
    half = SW_HD // 2
    inv = ROPE_THETA ** (-jnp.arange(half, dtype=F32) / half)
    ang = pos[:, None] * inv[None, :]
    cos = jnp.tile(jnp.cos(ang), (1, LANES // half))
    sin = jnp.tile(jnp.concatenate([-jnp.sin(ang), jnp.sin(ang)], axis=1), (1, LANES // SW_HD))
    return cos, sin


def _first_half_lanes(rows):
    lane = lax.broadcasted_iota(jnp.int32, (rows, LANES), 1)
    return jnp.bitwise_and(lane, SW_HD - 1) < SW_HD // 2


def _rope_block(x, cos, sin, first_half):
    partner = jnp.where(first_half, pltpu.roll(x, LANES - SW_HD // 2, axis=1), pltpu.roll(x, SW_HD // 2, axis=1))
    return x * cos + partner * sin


def _pair_heads(a, axis):
    shp = a.shape
    a = a.reshape(shp[:axis] + (2, 2, SW_GROUP) + shp[axis + 1:])
    a = jnp.swapaxes(a, axis + 1, axis + 2)
    return a.reshape(shp)


_SW_KEYS = ("g_pre_mix", "g_post_mix", "sw_win", "sw_wout")


def _swa_prompt_kernel(sink_ref, x_ref, cos_ref, sin_ref, gpre_ref, gpost_ref, win_ref, wout_ref,
                       out_ref, kout_ref, vout_ref, q_ref, k_ref, v_ref, o_ref, kf_ref):
    t = pl.program_id(1)
    rows = x_ref.shape[0]
    nsub = rows // WINDOW
    W = WINDOW

    @pl.when(t == 0)
    def _():
        k_ref[0:W, :] = jnp.zeros((W, SW_NKV), BF16)
        v_ref[0:W, :] = jnp.zeros((W, SW_NKV), BF16)

    x = x_ref[...]
    xn = _rms(x, gpre_ref[...]).astype(BF16)
    proj = _dot(xn, win_ref[...])
    cos = cos_ref[...]
    sin = sin_ref[...]
    first_half = _first_half_lanes(rows)
    for c in range(SW_NQ // LANES):
        blk = _rope_block(proj[:, c * LANES:(c + 1) * LANES], cos, sin, first_half)
        q_ref[c] = (blk * (SW_HD ** -0.5 * LOG2E)).astype(BF16)
    for c in range(SW_NKV // LANES):
        blk = _rope_block(proj[:, SW_NQ + c * LANES:SW_NQ + (c + 1) * LANES], cos, sin, first_half)
        kf_ref[:, c * LANES:(c + 1) * LANES] = blk
        k_ref[W:, c * LANES:(c + 1) * LANES] = blk.astype(BF16)
    v = proj[:, SW_NQ + SW_NKV:]
    v_ref[W:, :] = v.astype(BF16)

    G = SW_GROUP
    qi = jnp.bitwise_and(lax.broadcasted_iota(jnp.int32, (G * W, 2 * W), 0), W - 1)
    ki = lax.broadcasted_iota(jnp.int32, (G * W, 2 * W), 1)
    band = (ki >= qi) & (ki <= qi + W)
    lane = lax.broadcasted_iota(jnp.int32, (1, LANES), 1)
    low = lane < SW_HD
    zeros = jnp.zeros((1, LANES), BF16)
    ones_low = jnp.where(lane == 0, 1.0, 0.0).astype(BF16)
    ones_high = jnp.where(lane == SW_HD, 1.0, 0.0).astype(BF16)
    member = lax.shift_right_logical(lax.broadcasted_iota(jnp.int32, (G * W, 1), 0),
                                     jnp.int32(W.bit_length() - 1))

    def sink_column(pair, half):
        col = jnp.full((G * W, 1), sink_ref[2 * (pair * G) + half], F32)
        for g in range(1, G):
            col = jnp.where(member == g, sink_ref[2 * (pair * G + g) + half], col)
        return col

    sinks = [[sink_column(pair, half) for half in range(2)] for pair in range(SW_KV_HEADS // 2)]
    chains = [(i, pair, half) for i in range(nsub) for pair in range(SW_KV_HEADS // 2) for half in range(2)]

    def scores(i, pair, half):
        q4 = q_ref[pair * G:(pair + 1) * G, i * W:(i + 1) * W, :].reshape(G * W, LANES)
        k2 = k_ref[i * W:(i + 2) * W, pair * LANES:(pair + 1) * LANES]
        return _dot_nt(q4, jnp.where(low, k2, zeros) if half == 0 else jnp.where(low, zeros, k2))

    s_next = scores(*chains[0])
    low_half_out = {}
    for n, (i, pair, half) in enumerate(chains):
        s = s_next
        if n + 1 < len(chains):
            s_next = scores(*chains[n + 1])
        first_key = jnp.where(t * nsub + i > 0, 0, W)
        s = jnp.where(band & (ki >= first_key), s, -jnp.inf)
        sk = sinks[pair][half]
        mx = jnp.maximum(jnp.max(s, axis=1, keepdims=True), sk)
        v2 = v_ref[i * W:(i + 2) * W, pair * LANES:(pair + 1) * LANES]
        vz = jnp.where(low, v2, ones_high) if half == 0 else jnp.where(low, ones_low, v2)
        o = _dot(jnp.exp2(s - mx).astype(BF16), vz)
        sum_lane = SW_HD if half == 0 else 0
        o = o / (o[:, sum_lane:sum_lane + 1] + jnp.exp2(sk - mx))
        if half == 0:
            low_half_out[(i, pair)] = o
        else:
            both = jnp.where(low, low_half_out.pop((i, pair)), o).astype(BF16)
            o_ref[pair * G:(pair + 1) * G, i * W:(i + 1) * W, :] = both.reshape(G, W, LANES)

    y = _dot(jnp.concatenate([o_ref[c] for c in range(SW_NQ // LANES)], axis=1), wout_ref[...])
    out_ref[...] = x + _rms(y, gpost_ref[...])
    k_ref[0:W, :] = k_ref[rows:rows + W, :]
    v_ref[0:W, :] = v_ref[rows:rows + W, :]

    @pl.when(t == pl.num_programs(1) - 1)
    def _():
        kout_ref[0] = kf_ref[rows - W:, :].T
        vout_ref[0] = v[rows - W:, :].T


def _swa_prompt(h, w, layer):
    b, t, d = h.shape
    rows = SW_ROWS
    nt = t // rows
    assert t % rows == 0 and rows % WINDOW == 0 and t >= WINDOW
    cos, sin = _rope_tables(jnp.arange(t, dtype=F32))
    layers = (2 * layer + 1, 2 * layer + 1, layer, layer)
    out, k, v = pl.pallas_call(
        _swa_prompt_kernel,
        grid_spec=pltpu.PrefetchScalarGridSpec(
            num_scalar_prefetch=1,
            grid=(b, nt),
            in_specs=[pl.BlockSpec((rows, d), lambda i, j, s: (i * nt + j, 0)),
                      pl.BlockSpec((rows, LANES), lambda i, j, s: (j, 0)),
                      pl.BlockSpec((rows, LANES), lambda i, j, s: (j, 0))]
            + [_layer(w[k_], l) for k_, l in zip(_SW_KEYS, layers)],
            out_specs=[pl.BlockSpec((rows, d), lambda i, j, s: (i * nt + j, 0)),
                       pl.BlockSpec((1, SW_NKV, WINDOW), lambda i, j, s: (i, 0, 0)),
                       pl.BlockSpec((1, SW_NKV, WINDOW), lambda i, j, s: (i, 0, 0))],
            scratch_shapes=[pltpu.VMEM((SW_NQ // LANES, rows, LANES), BF16),
                            pltpu.VMEM((WINDOW + rows, SW_NKV), BF16), pltpu.VMEM((WINDOW + rows, SW_NKV), BF16),
                            pltpu.VMEM((SW_NQ // LANES, rows, LANES), BF16), pltpu.VMEM((rows, SW_NKV), F32)]),
        out_shape=[jax.ShapeDtypeStruct((b * t, d), F32),
                   jax.ShapeDtypeStruct((b, SW_NKV, WINDOW), F32),
                   jax.ShapeDtypeStruct((b, SW_NKV, WINDOW), F32)],
        compiler_params=_params("arbitrary", "arbitrary"),
        name="swa_prompt",
    )(w["sw_sinks_log2"][layer], h.reshape(b * t, d), cos, sin, *[w[k_] for k_ in _SW_KEYS])
    return out.reshape(b, t, d), k, v


def _swa_sample_pre_kernel(x_ref, cos_ref, sin_ref, gpre_ref, win_ref, q_ref, k_ref, v_ref, kt_ref, vt_ref):
    rows = x_ref.shape[0]
    xn = _rms(x_ref[...], gpre_ref[...]).astype(BF16)
    proj = _dot(xn, win_ref[...])
    cos = cos_ref[...]
    sin = sin_ref[...]
    first_half = _first_half_lanes(rows)
    for c in range(SW_NQ // LANES):
        blk = _rope_block(proj[:, c * LANES:(c + 1) * LANES], cos, sin, first_half)
        q_ref[:, c * LANES:(c + 1) * LANES] = blk * (SW_HD ** -0.5)
    for c in range(SW_NKV // LANES):
        k_ref[:, c * LANES:(c + 1) * LANES] = _rope_block(
            proj[:, SW_NQ + c * LANES:SW_NQ + (c + 1) * LANES], cos, sin, first_half)
    v = proj[:, SW_NQ + SW_NKV:]
    v_ref[...] = v
    kt_ref[...] = k_ref[...].T
    vt_ref[...] = v.T


def _swa_sample_attn_kernel(q_ref, kc_ref, vc_ref, knew_ref, vnew_ref, kcol_ref, vcol_ref, sink_ref, *rest):
    o_ref, kout_ref, vout_ref = rest[-3:]
    nbuf = kc_ref.shape[-1]
    pos = lax.broadcasted_iota(jnp.int32, (SW_HEADS, 1), 0)
    grp = 2 * lax.shift_right_logical(pos, jnp.int32(3)) + jnp.bitwise_and(pos, 1)
    sk = sink_ref[:, 0:1]
    lane = lax.broadcasted_iota(jnp.int32, (1, nbuf), 1)
    dist = nbuf - lane
    valid = (dist >= 0) & (dist <= WINDOW)
    newest = lane == nbuf - 1
    for b in range(q_ref.shape[0]):
        q = q_ref[b]
        qb = q.astype(BF16)
        s = jnp.zeros((SW_HEADS, nbuf), F32)
        s_new = jnp.zeros((SW_HEADS, 1), F32)
        for kv in range(SW_KV_HEADS):
            sl = slice(kv * SW_HD, (kv + 1) * SW_HD)
            kt = kc_ref[b, kv]
            s = jnp.where(grp == kv, _dot(qb, kt.astype(BF16)), s)
            s_new = jnp.where(grp == kv, jnp.sum(q * knew_ref[b, :, sl], axis=1, keepdims=True), s_new)
            kout_ref[b, kv] = jnp.where(newest, kcol_ref[0, sl, b:b + 1], pltpu.roll(kt, nbuf - 1, axis=1))
        s = jnp.where(valid, s, -jnp.inf)
        mx = jnp.maximum(jnp.maximum(jnp.max(s, axis=1, keepdims=True), s_new), sk)
        p = jnp.exp(s - mx)
        p_new = jnp.exp(s_new - mx)
        den = jnp.sum(p, axis=1, keepdims=True) + p_new + jnp.exp(sk - mx)
        pb = p.astype(BF16)
        o = jnp.zeros((SW_HEADS, SW_HD), F32)
        for kv in range(SW_KV_HEADS):
            sl = slice(kv * SW_HD, (kv + 1) * SW_HD)
            vt = vc_ref[b, kv]
            o = jnp.where(grp == kv, _dot_nt(pb, vt.astype(BF16)) + p_new * vnew_ref[b, :, sl], o)
            vout_ref[b, kv] = jnp.where(newest, vcol_ref[0, sl, b:b + 1], pltpu.roll(vt, nbuf - 1, axis=1))
        o_ref[b] = o / den


def _out_proj_kernel(x_ref, a_ref, gpost_ref, wout_ref, out_ref):
    y = _dot(a_ref[...].astype(BF16), wout_ref[...])
    out_ref[...] = x_ref[...] + _rms(y, gpost_ref[...])


def _swa_sample(h, kc_all, vc_all, w, layer, k_stack, v_stack):
    b, d = h.shape
    nbuf = kc_all.shape[-1]
    bb = SAMPLE_STATE_BATCH
    nb = b // bb
    assert b % bb == 0 and nbuf == LANES
    cos, sin = _rope_tables(jnp.full((1,), PAST_LEN, F32))
    pre_in = [h, cos, sin]
    pre_out = [(b, SW_NQ), (b, SW_NKV), (b, SW_NKV), (SW_NKV, b), (SW_NKV, b)]
    q, knew, vnew, kt, vt = pl.pallas_call(
        _swa_sample_pre_kernel,
        grid=(1,),
        in_specs=[_resident(a.shape) for a in pre_in]
        + [_layer(w["g_pre_mix"], 2 * layer + 1), _layer(w["sw_win"], layer)],
        out_specs=[_whole(s) for s in pre_out],
        out_shape=[jax.ShapeDtypeStruct(s, F32) for s in pre_out],
        compiler_params=_params("arbitrary"),
        name="swa_sample_pre",
    )(*pre_in, w["g_pre_mix"], w["sw_win"])

    def cols(a):
        return a.reshape(SW_NKV, nb, bb).transpose(1, 0, 2)

    cache_spec = pl.BlockSpec((None, bb, SW_KV_HEADS, SW_HD, nbuf), lambda i: (layer, i, 0, 0, 0))
    args = [q.reshape(b, SW_HEADS, SW_HD), kc_all, vc_all, knew.reshape(b, 1, SW_NKV), vnew.reshape(b, 1, SW_NKV),
            cols(kt), cols(vt), jnp.broadcast_to(w["sw_sinks"][layer][:, None], (SW_HEADS, LANES))]
    specs = [pl.BlockSpec((bb, SW_HEADS, SW_HD), lambda i: (i, 0, 0)), cache_spec, cache_spec,
             pl.BlockSpec((bb, 1, SW_NKV), lambda i: (i, 0, 0)),
             pl.BlockSpec((bb, 1, SW_NKV), lambda i: (i, 0, 0)),
             pl.BlockSpec((1, SW_NKV, bb), lambda i: (i, 0, 0)),
             pl.BlockSpec((1, SW_NKV, bb), lambda i: (i, 0, 0)),
             _resident((SW_HEADS, LANES))]
    aliases = {}
    if k_stack is not None:
        args += [k_stack, v_stack]
        specs += [pl.BlockSpec(memory_space=pl.ANY), pl.BlockSpec(memory_space=pl.ANY)]
        aliases = {len(args) - 2: 1, len(args) - 1: 2}
    o, k_stack, v_stack = pl.pallas_call(
        _swa_sample_attn_kernel,
        grid=(nb,),
        in_specs=specs,
        out_specs=[pl.BlockSpec((bb, SW_HEADS, SW_HD), lambda i: (i, 0, 0)), cache_spec, cache_spec],
        out_shape=[jax.ShapeDtypeStruct((b, SW_HEADS, SW_HD), F32),
                   jax.ShapeDtypeStruct(kc_all.shape, F32), jax.ShapeDtypeStruct(vc_all.shape, F32)],
        input_output_aliases=aliases,
        compiler_params=_params("arbitrary"),
        name="swa_sample_attn",
    )(*args)

    post_in = [h, o.reshape(b, SW_NQ)]
    out = pl.pallas_call(
        _out_proj_kernel,
        grid=(1,),
        in_specs=[_resident(a.shape) for a in post_in]
        + [_layer(w["g_post_mix"], 2 * layer + 1), _layer(w["sw_wout"], layer)],
        out_specs=_whole((b, d)),
        out_shape=jax.ShapeDtypeStruct((b, d), F32),
        compiler_params=_params("arbitrary"),
        name="swa_sample_post",
    )(*post_in, w["g_post_mix"], w["sw_wout"])
    return out, k_stack, v_stack


def _prepare(g_pre_mix, g_post_mix, g_pre_ffn, g_post_ffn, g_ple, ml_w_in, ml_b_gate, ml_w_out, sw_w_in,
             sw_sinks, sw_w_out, ffn_w_up, ffn_conv_w, ffn_conv_b, ffn_w_down, ple_w_proj, ple_w_gate):
    gate_w = ml_w_in[:, :, ML_QKVO:]
    q_cols = _pair_heads(sw_w_in[:, :, :SW_NQ].reshape(sw_w_in.shape[:2] + (SW_HEADS, SW_HD)), 2)
    sw_win = jnp.concatenate([q_cols.reshape(sw_w_in.shape[:2] + (SW_NQ,)), sw_w_in[:, :, SW_NQ:]], axis=-1)
    sw_wout = _pair_heads(sw_w_out.reshape((sw_w_out.shape[0], SW_HEADS, SW_HD, sw_w_out.shape[-1])), 1)
    sinks = _pair_heads(sw_sinks, 1)
    return dict(
        g_pre_mix=g_pre_mix[:, None], g_post_mix=g_post_mix[:, None], g_pre_ffn=g_pre_ffn[:, None],
        g_post_ffn=g_post_ffn[:, None], g_ple=g_ple[:, None],
        ffn_wup=ffn_w_up.astype(BF16), ffn_cw=ffn_conv_w, ffn_cb=ffn_conv_b[:, None],
        ffn_wdown=ffn_w_down.astype(BF16), ple_wgate=ple_w_gate.astype(BF16), ple_wproj=ple_w_proj.astype(BF16),
        ml_win=ml_w_in[:, :, :ML_QKVO].astype(BF16),
        ml_wgc=jnp.pad(gate_w, ((0, 0), (0, 0), (0, LANES - 2 * ML_HEADS))).astype(BF16),
        ml_bgc=jnp.pad(ml_b_gate, ((0, 0), (0, LANES - 2 * ML_HEADS)))[:, None],
        ml_wgr=jnp.swapaxes(gate_w, 1, 2).astype(BF16),
        ml_bgr=jnp.broadcast_to(ml_b_gate[:, :, None], ml_b_gate.shape + (LANES,)),
        ml_wout=ml_w_out.astype(BF16),
        sw_win=sw_win.astype(BF16), sw_wout=sw_wout.reshape(sw_w_out.shape).astype(BF16),
        sw_sinks=sinks, sw_sinks_log2=sinks * LOG2E)


def kernel(x_prompt, x_sample, p_prompt, p_sample, state_mlstm_C, state_mlstm_n, state_mlstm_m, cache_swa_k, cache_swa_v, state_conv, g_pre_mix, g_post_mix, g_pre_ffn, g_post_ffn, g_ple, ml_w_in, ml_b_gate, ml_w_out, sw_w_in, sw_sinks, sw_w_out, ffn_w_up, ffn_conv_w, ffn_conv_b, ffn_w_down, ple_w_proj, ple_w_gate):
    depth = g_pre_mix.shape[0]
    w = _prepare(g_pre_mix, g_post_mix, g_pre_ffn, g_post_ffn, g_ple, ml_w_in, ml_b_gate, ml_w_out, sw_w_in,
                 sw_sinks, sw_w_out, ffn_w_up, ffn_conv_w, ffn_conv_b, ffn_w_down, ple_w_proj, ple_w_gate)
    kc_all = jnp.transpose(cache_swa_k, (0, 1, 3, 4, 2))
    vc_all = jnp.transpose(cache_swa_v, (0, 1, 3, 4, 2))
    hp = x_prompt
    hs = x_sample[:, 0, :]
    cp, np_, mp, ns, ms = [], [], [], [], []
    kp, vp = [], []
    convp, convs = [], []
    c_stack = k_stack = v_stack = None
    for i in range(depth):
        j = i // 2
        if i % 2 == 0:
            hp, c1, n1, m1 = _mlstm_prompt(hp, w, j)
            hs, c_stack, n2, m2 = _mlstm_sample(hs, state_mlstm_C, state_mlstm_n[j], state_mlstm_m[j], w, j, c_stack)
            cp.append(c1); np_.append(n1); mp.append(m1)
            ns.append(n2); ms.append(m2)
        else:
            hp, k1, v1 = _swa_prompt(hp, w, j)
            hs, k_stack, v_stack = _swa_sample(hs, kc_all, vc_all, w, j, k_stack, v_stack)
            kp.append(k1); vp.append(v1)
        hp, cvp = _ffn_prompt(hp, p_prompt, w, i)
        hs, cvs = _ffn_sample(hs, p_sample[i][:, 0, :], state_conv[i], w, i)
        convp.append(cvp); convs.append(cvs)

    def cache_layout(a):
        return jnp.transpose(a, (0, 1, 4, 2, 3))

    kv_shape = (len(kp), x_prompt.shape[0], SW_KV_HEADS, SW_HD, WINDOW)
    return (hp, hs[:, None, :],
            jnp.stack(cp), jnp.stack(np_), jnp.stack(mp),
            cache_layout(jnp.stack(kp).reshape(kv_shape)), cache_layout(jnp.stack(vp).reshape(kv_shape)),
            jnp.stack(convp),
            c_stack, jnp.stack(ns), jnp.stack(ms), cache_layout(k_stack), cache_layout(v_stack), jnp.stack(convs))
```

```python
import math

import jax
import jax.numpy as jnp
from jax import lax
from jax.experimental import pallas as pl
from jax.experimental.pallas import tpu as pltpu

F32 = jnp.float32
BF16 = jnp.bfloat16

EPS = 1e-6
PLE_DIM = 256
ML_HEADS = 4
ML_DK = 128
ML_DV = 256
ML_HK = ML_HEADS * ML_DK
ML_QKVO = 2 * ML_HK + 2 * ML_HEADS * ML_DV
SW_HEADS = 16
SW_KV_HEADS = 4
SW_HD = 64
SW_GROUP = SW_HEADS // SW_KV_HEADS
SW_NQ = SW_HEADS * SW_HD
SW_NKV = SW_KV_HEADS * SW_HD
WINDOW = 128
ROPE_THETA = 10000.0
PAST_LEN = 8192
LOG2E = math.log2(math.e)

LANES = 128
SUBLANES = 8
FFN_CHUNK = 256
FFN_ROWS = 512
ML_CHUNK_ROWS = 256
SW_ROWS = 512
SAMPLE_STATE_BATCH = 8
VMEM_LIMIT = 56 * 1024 * 1024

assert SW_KV_HEADS == 4 and SW_GROUP == 4 and 2 * SW_HD == LANES and WINDOW == LANES


def _params(*sem):
    return pltpu.CompilerParams(dimension_semantics=sem, vmem_limit_bytes=VMEM_LIMIT)


def _resident(shape):
    zeros = (0,) * len(shape)
    return pl.BlockSpec(shape, lambda *_: zeros, pipeline_mode=pl.Buffered(1))


def _layer(a, layer):
    zeros = (0,) * (a.ndim - 1)
    return pl.BlockSpec((None,) + a.shape[1:], lambda *_: (layer,) + zeros, pipeline_mode=pl.Buffered(1))


def _whole(shape):
    zeros = (0,) * len(shape)
    return pl.BlockSpec(shape, lambda *_: zeros)


def _rms(x, g):
    return x * lax.rsqrt(jnp.mean(x * x, axis=-1, keepdims=True) + EPS) * g


def _sigmoid(x):
    return 1.0 / (1.0 + jnp.exp(-x))


def _log_sigmoid(x):
    return jnp.minimum(x, 0.0) - jnp.log1p(jnp.exp(-jnp.abs(x)))


def _dot(a, b):
    return jnp.dot(a, b, preferred_element_type=F32)


def _dot_nt(a, b):
    return lax.dot_general(a, b, (((1,), (1,)), ((), ())), preferred_element_type=F32)


def _dot_tn(a, b):
    return lax.dot_general(a, b, (((0,), (0,)), ((), ())), preferred_element_type=F32)


def _cumsum(x, axis):
    n = x.shape[axis]
    idx = lax.broadcasted_iota(jnp.int32, x.shape, axis)
    s = 1
    while s < n:
        x = x + jnp.where(idx >= s, pltpu.roll(x, s, axis=axis), 0.0)
        s *= 2
    return x


def _ffn_tail(x, f, p, gpost_ref, gple_ref, wgate_ref, wproj_ref):
    x1 = x + _rms(f, gpost_ref[...])
    gate = _sigmoid(_dot(_rms(x1, gple_ref[...]).astype(BF16), wgate_ref[...]))
    return x1 + gate * _dot(p.astype(BF16), wproj_ref[...])


def _ffn_chunk_cols(j, hidden):
    return (slice(j * FFN_CHUNK, (j + 1) * FFN_CHUNK),
            slice(hidden + j * FFN_CHUNK, hidden + (j + 1) * FFN_CHUNK))


def _ffn_prompt_kernel(x_ref, p_ref, gpre_ref, gpost_ref, gple_ref, wup_ref, cw_ref, cb_ref, wdown_ref,
                       wgate_ref, wproj_ref, out_ref, conv_ref, carry_ref, xp_ref, pp_ref, acc_ref):
    t = pl.program_id(1)
    rows = x_ref.shape[0]
    hidden = wdown_ref.shape[0]
    nch = hidden // FFN_CHUNK
    S = SUBLANES
    G = rows // S

    @pl.when(t == 0)
    def _():
        carry_ref[...] = jnp.zeros_like(carry_ref)

    def interleaved(i):
        return pl.ds(((S * i) % G) * S + (S * i) // G, S, stride=S)

    for i in range(G):
        for c in range(xp_ref.shape[0]):
            xp_ref[c, interleaved(i), :] = x_ref[S * i:S * (i + 1), c * LANES:(c + 1) * LANES]
        for c in range(pp_ref.shape[0]):
            pp_ref[c, interleaved(i), :] = p_ref[S * i:S * (i + 1), c * LANES:(c + 1) * LANES]
    x = jnp.concatenate([xp_ref[c] for c in range(xp_ref.shape[0])], axis=1)
    p = jnp.concatenate([pp_ref[c] for c in range(pp_ref.shape[0])], axis=1)
    ple = _dot(p.astype(BF16), wproj_ref[...])
    xn = _rms(x, gpre_ref[...]).astype(BF16)
    first = lax.broadcasted_iota(jnp.int32, (S, FFN_CHUNK), 0) == 0

    def conv(u, cols):
        prev = carry_ref[:, cols]
        back1 = jnp.where(first, prev[2 * S - 1:2 * S], pltpu.roll(u[rows - S:], 1, axis=0))
        back2 = jnp.where(first, prev[S - 1:S], pltpu.roll(u[rows - 2 * S:rows - S], 1, axis=0))
        u1 = jnp.concatenate([back1, u[:rows - S]], axis=0)
        u2 = jnp.concatenate([back2, back1, u[:rows - 2 * S]], axis=0)
        carry_ref[:, cols] = u[rows - 2 * S:]
        cw = cw_ref[:, cols]
        return cb_ref[:, cols] + u2 * cw[0:1] + u1 * cw[1:2] + u * cw[2:3]

    def up(j):
        return [_dot(xn, wup_ref[:, cols]) for cols in _ffn_chunk_cols(j, hidden)]

    u = up(0)
    for j in range(nch):
        u_next = up(j + 1) if j + 1 < nch else None
        gcols, vcols = _ffn_chunk_cols(j, hidden)
        cg = conv(u[0], gcols)
        h = cg * _sigmoid(cg) * conv(u[1], vcols)
        d = _dot(h.astype(BF16), wdown_ref[j * FFN_CHUNK:(j + 1) * FFN_CHUNK, :])
        if j == 0:
            acc_ref[...] = d
        else:
            acc_ref[...] += d
        u = u_next

    half_rows = rows // 2
    for part in range(2):
        r = slice(part * half_rows, (part + 1) * half_rows)
        x1 = x[r] + _rms(acc_ref[r, :], gpost_ref[...])
        gate = _sigmoid(_dot(_rms(x1, gple_ref[...]).astype(BF16), wgate_ref[...]))
        res = x1 + gate * ple[r]
        for c in range(xp_ref.shape[0]):
            xp_ref[c, r, :] = res[:, c * LANES:(c + 1) * LANES]
    for i in range(G):
        for c in range(xp_ref.shape[0]):
            out_ref[S * i:S * (i + 1), c * LANES:(c + 1) * LANES] = xp_ref[c, interleaved(i), :]

    @pl.when(t == pl.num_programs(1) - 1)
    def _():
        conv_ref[0] = carry_ref[...]


def _ffn_sample_kernel(x_ref, p_ref, prev0_ref, prev1_ref, gpre_ref, gpost_ref, gple_ref, wup_ref, cw_ref,
                       cb_ref, wdown_ref, wgate_ref, wproj_ref, out_ref, u_ref, acc_ref):
    hidden = wdown_ref.shape[0]
    x = x_ref[...]
    xn = _rms(x, gpre_ref[...]).astype(BF16)

    def conv(cols):
        u = _dot(xn, wup_ref[:, cols])
        u_ref[:, cols] = u
        cw = cw_ref[:, cols]
        return cb_ref[:, cols] + prev0_ref[:, cols] * cw[0:1] + prev1_ref[:, cols] * cw[1:2] + u * cw[2:3]

    for j in range(hidden // FFN_CHUNK):
        gcols, vcols = _ffn_chunk_cols(j, hidden)
        cg = conv(gcols)
        h = cg * _sigmoid(cg) * conv(vcols)
        d = _dot(h.astype(BF16), wdown_ref[j * FFN_CHUNK:(j + 1) * FFN_CHUNK, :])
        if j == 0:
            acc_ref[...] = d
        else:
            acc_ref[...] += d
    out_ref[...] = _ffn_tail(x, acc_ref[...], p_ref[...], gpost_ref, gple_ref, wgate_ref, wproj_ref)


_FFN_KEYS = ("g_pre_ffn", "g_post_ffn", "g_ple", "ffn_wup", "ffn_cw", "ffn_cb", "ffn_wdown", "ple_wgate",
             "ple_wproj")


def _ffn_prompt(h, p_all, w, layer):
    b, t, d = h.shape
    f2 = w["ffn_wup"].shape[-1]
    nt = t // FFN_ROWS
    assert t % FFN_ROWS == 0 and (FFN_ROWS // SUBLANES) % SUBLANES == 0
    out, conv = pl.pallas_call(
        _ffn_prompt_kernel,
        grid=(b, nt),
        in_specs=[pl.BlockSpec((FFN_ROWS, d), lambda i, j: (i * nt + j, 0)),
                  pl.BlockSpec((FFN_ROWS, PLE_DIM), lambda i, j: ((layer * b + i) * nt + j, 0))]
        + [_layer(w[k], layer) for k in _FFN_KEYS],
        out_specs=[pl.BlockSpec((FFN_ROWS, d), lambda i, j: (i * nt + j, 0)),
                   pl.BlockSpec((1, 2 * SUBLANES, f2), lambda i, j: (i, 0, 0))],
        out_shape=[jax.ShapeDtypeStruct((b * t, d), F32),
                   jax.ShapeDtypeStruct((b, 2 * SUBLANES, f2), F32)],
        scratch_shapes=[pltpu.VMEM((2 * SUBLANES, f2), F32),
                        pltpu.VMEM((d // LANES, FFN_ROWS, LANES), F32),
                        pltpu.VMEM((PLE_DIM // LANES, FFN_ROWS, LANES), F32), pltpu.VMEM((FFN_ROWS, d), F32)],
        compiler_params=_params("arbitrary", "arbitrary"),
        name="ffn_prompt",
    )(h.reshape(b * t, d), p_all.reshape(-1, PLE_DIM), *[w[k] for k in _FFN_KEYS])
    return out.reshape(b, t, d), conv[:, SUBLANES - 1::SUBLANES, :]


def _ffn_sample(h, p, prev, w, layer):
    b, d = h.shape
    f2 = w["ffn_wup"].shape[-1]
    args = [h, p, prev[:, 0], prev[:, 1]]
    out, u = pl.pallas_call(
        _ffn_sample_kernel,
        grid=(1,),
        in_specs=[_resident(a.shape) for a in args] + [_layer(w[k], layer) for k in _FFN_KEYS],
        out_specs=[_whole((b, d)), _whole((b, f2))],
        out_shape=[jax.ShapeDtypeStruct((b, d), F32), jax.ShapeDtypeStruct((b, f2), F32)],
        scratch_shapes=[pltpu.VMEM((b, d), F32)],
        compiler_params=_params("arbitrary"),
        name="ffn_sample",
    )(*args, *[w[k] for k in _FFN_KEYS])
    return out, jnp.stack([prev[:, 1], u], axis=1)


_ML_KEYS = ("g_pre_mix", "g_post_mix", "ml_win", "ml_wgc", "ml_bgc", "ml_wgr", "ml_bgr", "ml_wout")


def _mlstm_prompt_kernel(x_ref, gpre_ref, gpost_ref, win_ref, wgc_ref, bgc_ref, wgr_ref, bgr_ref, wout_ref,
                         out_ref, c_out_ref, n_out_ref, m_out_ref, c_ref, n_ref, m_ref, h_ref):
    t = pl.program_id(1)
    L = x_ref.shape[0]

    @pl.when(t == 0)
    def _():
        c_ref[...] = jnp.zeros_like(c_ref)
        n_ref[...] = jnp.zeros_like(n_ref)
        m_ref[...] = jnp.zeros_like(m_ref)

    x = x_ref[...]
    xn = _rms(x, gpre_ref[...]).astype(BF16)
    gc = _dot(xn, wgc_ref[...]) + bgc_ref[...]
    gr = _dot_nt(wgr_ref[...], xn) + bgr_ref[:, 0:1]
    proj_qk = _dot(xn, win_ref[:, :2 * ML_HK])
    is_forget_r = lax.broadcasted_iota(jnp.int32, gr.shape, 0) >= ML_HEADS
    bc = _cumsum(_log_sigmoid(gc), 0)
    br = _cumsum(jnp.where(is_forget_r, _log_sigmoid(gr), 0.0), 1)
    causal = lax.broadcasted_iota(jnp.int32, (L, L), 0) >= lax.broadcasted_iota(jnp.int32, (L, L), 1)

    heads = range(ML_HEADS)
    q = [proj_qk[:, h * ML_DK:(h + 1) * ML_DK] * (ML_DK ** -0.5) for h in heads]
    k = [proj_qk[:, ML_HK + h * ML_DK:ML_HK + (h + 1) * ML_DK] for h in heads]
    qb = [a.astype(BF16) for a in q]
    qk = [_dot_nt(qb[h], k[h].astype(BF16)) for h in heads]
    qc = [_dot(qb[h], c_ref[h].astype(BF16)) for h in heads]
    proj_vo = _dot(xn, win_ref[:, 2 * ML_HK:])
    vb = [proj_vo[:, h * ML_DV:(h + 1) * ML_DV].astype(BF16) for h in heads]

    s_b, wk_b, w_inter, inv_den, w_old, n_new, m_new = [], [], [], [], [], [], []
    for h in heads:
        ig_r = gr[h:h + 1, :]
        b_r = br[ML_HEADS + h:ML_HEADS + h + 1, :]
        ig_c = gc[:, h:h + 1]
        b_c = bc[:, ML_HEADS + h:ML_HEADS + h + 1]
        m_prev = m_ref[h:h + 1, 0:1]
        n_prev = n_ref[h:h + 1, :]

        d = jnp.where(causal, b_c + (ig_r - b_r), -jnp.inf)
        a = b_c + m_prev
        m_t = jnp.maximum(a, jnp.max(d, axis=1, keepdims=True))
        wi = jnp.exp(a - m_t)
        s = qk[h] * jnp.exp(d - m_t)
        qn = wi * jnp.sum(q[h] * n_prev, axis=1, keepdims=True) + jnp.sum(s, axis=1, keepdims=True)
        w_inter.append(wi)
        inv_den.append(1.0 / jnp.maximum(jnp.abs(qn), jnp.exp(-m_t)))
        s_b.append(s.astype(BF16))

        b_last = b_r[:, L - 1:L]
        g_r = b_last - b_r + ig_r
        g_c = b_last - b_c + ig_c
        mn = jnp.maximum(b_last + m_prev, jnp.max(g_r, axis=1, keepdims=True))
        wo = jnp.exp(b_last + m_prev - mn)
        wk = jnp.exp(g_c - mn) * k[h]
        wk_b.append(wk.astype(BF16))
        w_old.append(wo)
        n_new.append(wo * n_prev + jnp.sum(wk, axis=0, keepdims=True))
        m_new.append(mn)

    sv = [_dot(s_b[h], vb[h]) for h in heads]
    kv = [_dot_tn(wk_b[h], vb[h]) for h in heads]
    for h in heads:
        h_ref[:, h * ML_DV:(h + 1) * ML_DV] = (w_inter[h] * qc[h] + sv[h]) * inv_den[h]
        c_ref[h] = w_old[h] * c_ref[h] + kv[h]
        n_ref[h:h + 1, :] = n_new[h]
        m_ref[h:h + 1, :] = jnp.broadcast_to(m_new[h], (1, LANES))

    o = proj_vo[:, ML_HEADS * ML_DV:]
    y = _dot((_sigmoid(o) * h_ref[...]).astype(BF16), wout_ref[...])
    out_ref[...] = x + _rms(y, gpost_ref[...])

    @pl.when(t == pl.num_programs(1) - 1)
    def _():
        c_out_ref[0] = c_ref[...]
        n_out_ref[0] = n_ref[0:ML_HEADS, :]
        m_out_ref[0] = m_ref[...]


def _ml_layers(layer):
    return [2 * layer, 2 * layer] + [layer] * (len(_ML_KEYS) - 2)


def _mlstm_prompt(h, w, layer):
    b, t, d = h.shape
    L = ML_CHUNK_ROWS
    nt = t // L
    assert t % L == 0
    out, c, n, m = pl.pallas_call(
        _mlstm_prompt_kernel,
        grid=(b, nt),
        in_specs=[pl.BlockSpec((L, d), lambda i, j: (i * nt + j, 0))]
        + [_layer(w[k], l) for k, l in zip(_ML_KEYS, _ml_layers(layer))],
        out_specs=[pl.BlockSpec((L, d), lambda i, j: (i * nt + j, 0)),
                   pl.BlockSpec((1, ML_HEADS, ML_DK, ML_DV), lambda i, j: (i, 0, 0, 0)),
                   pl.BlockSpec((1, ML_HEADS, ML_DK), lambda i, j: (i, 0, 0)),
                   pl.BlockSpec((1, SUBLANES, LANES), lambda i, j: (i, 0, 0))],
        out_shape=[jax.ShapeDtypeStruct((b * t, d), F32),
                   jax.ShapeDtypeStruct((b, ML_HEADS, ML_DK, ML_DV), F32),
                   jax.ShapeDtypeStruct((b, ML_HEADS, ML_DK), F32),
                   jax.ShapeDtypeStruct((b, SUBLANES, LANES), F32)],
        scratch_shapes=[pltpu.VMEM((ML_HEADS, ML_DK, ML_DV), F32), pltpu.VMEM((SUBLANES, ML_DK), F32),
                        pltpu.VMEM((SUBLANES, LANES), F32), pltpu.VMEM((L, ML_HEADS * ML_DV), F32)],
        compiler_params=_params("arbitrary", "arbitrary"),
        name="mlstm_prompt",
    )(h.reshape(b * t, d), *[w[k] for k in _ML_KEYS])
    return out.reshape(b, t, d), c, n, m[:, :ML_HEADS, 0]


def _mlstm_sample_pre_kernel(x_ref, n_ref, m_ref, gpre_ref, win_ref, wgc_ref, bgc_ref,
                             q_ref, kw_ref, v_ref, o_ref, wold_ref, scal_ref, nnew_ref):
    H = ML_HEADS
    xn = _rms(x_ref[...], gpre_ref[...]).astype(BF16)
    proj = _dot(xn, win_ref[...])
    gates = _dot(xn, wgc_ref[...]) + bgc_ref[...]
    v_ref[...] = proj[:, 2 * ML_HK:2 * ML_HK + H * ML_DV]
    o_ref[...] = proj[:, 2 * ML_HK + H * ML_DV:]
    scal_ref[...] = jnp.zeros_like(scal_ref)
    for h in range(H):
        q = proj[:, h * ML_DK:(h + 1) * ML_DK] * (ML_DK ** -0.5)
        k = proj[:, ML_HK + h * ML_DK:ML_HK + (h + 1) * ML_DK]
        n_prev = n_ref[:, h * ML_DK:(h + 1) * ML_DK]
        ig = gates[:, h:h + 1]
        lf = _log_sigmoid(gates[:, H + h:H + h + 1])
        a = lf + m_ref[:, h:h + 1]
        m_t = jnp.maximum(a, ig)
        w_old = jnp.exp(a - m_t)
        w_new = jnp.exp(ig - m_t)
        s = jnp.sum(q * k, axis=1, keepdims=True) * w_new
        qn = w_old * jnp.sum(q * n_prev, axis=1, keepdims=True) + s
        q_ref[:, h * ML_DK:(h + 1) * ML_DK] = q
        kw_ref[:, h * ML_DK:(h + 1) * ML_DK] = w_new * k
        nnew_ref[:, h * ML_DK:(h + 1) * ML_DK] = w_old * n_prev + w_new * k
        wold_ref[:, h * ML_DV:(h + 1) * ML_DV] = jnp.broadcast_to(w_old, (w_old.shape[0], ML_DV))
        scal_ref[:, h:h + 1] = w_old
        scal_ref[:, H + h:H + h + 1] = s
        scal_ref[:, 2 * H + h:2 * H + h + 1] = jnp.maximum(jnp.abs(qn), jnp.exp(-m_t))
        scal_ref[:, 3 * H + h:3 * H + h + 1] = m_t


def _mlstm_sample_state_kernel(c_ref, qt_ref, kt_ref, v_ref, wold_ref, *rest):
    cnew_ref, qc_ref = rest[-2:]
    for b in range(c_ref.shape[0]):
        for h in range(ML_HEADS):
            c = c_ref[b, h]
            qc = qt_ref[0, h * ML_DK:(h + 1) * ML_DK, b:b + 1]
            kc = kt_ref[0, h * ML_DK:(h + 1) * ML_DK, b:b + 1]
            vr = v_ref[b:b + 1, h * ML_DV:(h + 1) * ML_DV]
            wo = wold_ref[b:b + 1, h * ML_DV:(h + 1) * ML_DV]
            qc_ref[b:b + 1, h * ML_DV:(h + 1) * ML_DV] = jnp.sum(qc * c, axis=0, keepdims=True)
            cnew_ref[b, h] = c * wo + kc * vr


def _mlstm_sample_post_kernel(x_ref, qc_ref, v_ref, o_ref, scal_ref, gpost_ref, wout_ref, out_ref, h_ref):
    H = ML_HEADS
    for h in range(H):
        sl = slice(h * ML_DV, (h + 1) * ML_DV)
        num = scal_ref[:, h:h + 1] * qc_ref[:, sl] + scal_ref[:, H + h:H + h + 1] * v_ref[:, sl]
        h_ref[:, sl] = num / scal_ref[:, 2 * H + h:2 * H + h + 1]
    y = _dot((_sigmoid(o_ref[...]) * h_ref[...]).astype(BF16), wout_ref[...])
    out_ref[...] = x_ref[...] + _rms(y, gpost_ref[...])


def _aliased(stack_prev, args, in_specs):
    if stack_prev is None:
        return {}
    args.append(stack_prev)
    in_specs.append(pl.BlockSpec(memory_space=pl.ANY))
    return {len(args) - 1: 0}


def _mlstm_sample(h, c_all, n, m, w, layer, c_stack):
    b, d = h.shape
    H = ML_HEADS
    bb = SAMPLE_STATE_BATCH
    nb = b // bb
    assert b % bb == 0
    pre_in = [h, n.reshape(b, H * ML_DK), jnp.pad(m, ((0, 0), (0, LANES - H)))]
    pre_keys = ("g_pre_mix", "ml_win", "ml_wgc", "ml_bgc")
    pre_layers = (2 * layer, layer, layer, layer)
    pre_out = [(b, ML_HK), (b, ML_HK), (b, H * ML_DV), (b, H * ML_DV), (b, H * ML_DV), (b, LANES), (b, ML_HK)]
    q, kw, v, o, wold, scal, nnew = pl.pallas_call(
        _mlstm_sample_pre_kernel,
        grid=(1,),
        in_specs=[_resident(a.shape) for a in pre_in] + [_layer(w[k], l) for k, l in zip(pre_keys, pre_layers)],
        out_specs=[_whole(s) for s in pre_out],
        out_shape=[jax.ShapeDtypeStruct(s, F32) for s in pre_out],
        compiler_params=_params("arbitrary"),
        name="mlstm_sample_pre",
    )(*pre_in, *[w[k] for k in pre_keys])

    def cols(a):
        return a.reshape(nb, bb, ML_HK).transpose(0, 2, 1)

    state_args = [c_all, cols(q), cols(kw), v, wold]
    state_specs = [pl.BlockSpec((None, bb, H, ML_DK, ML_DV), lambda i: (layer, i, 0, 0, 0)),
                   pl.BlockSpec((1, ML_HK, bb), lambda i: (i, 0, 0)),
                   pl.BlockSpec((1, ML_HK, bb), lambda i: (i, 0, 0)),
                   pl.BlockSpec((bb, H * ML_DV), lambda i: (i, 0)),
                   pl.BlockSpec((bb, H * ML_DV), lambda i: (i, 0))]
    aliases = _aliased(c_stack, state_args, state_specs)
    c_stack, qc = pl.pallas_call(
        _mlstm_sample_state_kernel,
        grid=(nb,),
        in_specs=state_specs,
        out_specs=[pl.BlockSpec((None, bb, H, ML_DK, ML_DV), lambda i: (layer, i, 0, 0, 0)),
                   pl.BlockSpec((bb, H * ML_DV), lambda i: (i, 0))],
        out_shape=[jax.ShapeDtypeStruct(c_all.shape, F32), jax.ShapeDtypeStruct((b, H * ML_DV), F32)],
        input_output_aliases=aliases,
        compiler_params=_params("arbitrary"),
        name="mlstm_sample_state",
    )(*state_args)

    post_in = [h, qc, v, o, scal]
    out = pl.pallas_call(
        _mlstm_sample_post_kernel,
        grid=(1,),
        in_specs=[_resident(a.shape) for a in post_in]
        + [_layer(w["g_post_mix"], 2 * layer), _layer(w["ml_wout"], layer)],
        out_specs=_whole((b, d)),
        out_shape=jax.ShapeDtypeStruct((b, d), F32),
        scratch_shapes=[pltpu.VMEM((b, H * ML_DV), F32)],
        compiler_params=_params("arbitrary"),
        name="mlstm_sample_post",
    )(*post_in, w["g_post_mix"], w["ml_wout"])
    return out, c_stack, nnew.reshape(b, H, ML_DK), scal[:, 3 * H:4 * H]


def _rope_tables(pos):
    half = SW_HD // 2
    inv = ROPE_THETA ** (-jnp.arange(half, dtype=F32) / half)
    ang = pos[:, None] * inv[None, :]
    cos = jnp.tile(jnp.cos(ang), (1, LANES // half))
    sin = jnp.tile(jnp.concatenate([-jnp.sin(ang), jnp.sin(ang)], axis=1), (1, LANES // SW_HD))
    return cos, sin


def _first_half_lanes(rows):
    lane = lax.broadcasted_iota(jnp.int32, (rows, LANES), 1)
    return jnp.bitwise_and(lane, SW_HD - 1) < SW_HD // 2


def _rope_block(x, cos, sin, first_half):
    partner = jnp.where(first_half, pltpu.roll(x, LANES - SW_HD // 2, axis=1), pltpu.roll(x, SW_HD // 2, axis=1))
    return x * cos + partner * sin


def _pair_heads(a, axis):
    shp = a.shape
    a = a.reshape(shp[:axis] + (2, 2, SW_GROUP) + shp[axis + 1:])
    a = jnp.swapaxes(a, axis + 1, axis + 2)
    return a.reshape(shp)


_SW_KEYS = ("g_pre_mix", "g_post_mix", "sw_win", "sw_wout")


def _swa_prompt_kernel(sink_ref, x_ref, cos_ref, sin_ref, gpre_ref, gpost_ref, win_ref, wout_ref,
                       out_ref, kout_ref, vout_ref, k_ref, v_ref, kf_ref, vf_ref):
    t = pl.program_id(1)
    rows = x_ref.shape[0]
    nsub = rows // WINDOW
    W = WINDOW
    G = SW_GROUP
    K2 = 2 * W
    pairs = range(SW_KV_HEADS // 2)

    @pl.when(t == 0)
    def _():
        k_ref[0:W, :] = jnp.zeros((W, SW_NKV), BF16)
        v_ref[0:W, :] = jnp.zeros((W, SW_NKV), BF16)

    first_half = _first_half_lanes(W)
    qi = jnp.bitwise_and(lax.broadcasted_iota(jnp.int32, (G * W, K2), 0), W - 1)
    ki = lax.broadcasted_iota(jnp.int32, (G * W, K2), 1)
    band = (ki >= qi) & (ki <= qi + W)
    lane = lax.broadcasted_iota(jnp.int32, (1, LANES), 1)
    low = lane < SW_HD
    high = jnp.logical_not(low)
    zeros = jnp.zeros((1, LANES), BF16)
    ones_low = jnp.where(low, 1.0, 0.0).astype(BF16)
    ones_high = jnp.where(high, 1.0, 0.0).astype(BF16)
    member = lax.shift_right_logical(lax.broadcasted_iota(jnp.int32, (G * W, 1), 0),
                                     jnp.int32(W.bit_length() - 1))

    def sink_column(pair, half):
        col = jnp.full((G * W, 1), sink_ref[2 * (pair * G) + half], F32)
        for g in range(1, G):
            col = jnp.where(member == g, sink_ref[2 * (pair * G + g) + half], col)
        return col

    sinks = [[sink_column(pair, half) for half in range(2)] for pair in pairs]

    def project(i):
        r = slice(i * W, (i + 1) * W)
        xn = _rms(x_ref[r, :], gpre_ref[...]).astype(BF16)
        proj = _dot(xn, win_ref[:, :SW_NQ + SW_NKV])
        v = _dot(xn, win_ref[:, SW_NQ + SW_NKV:])
        cos = cos_ref[r, :]
        sin = sin_ref[r, :]
        q = [(_rope_block(proj[:, c * LANES:(c + 1) * LANES], cos, sin, first_half)
              * (SW_HD ** -0.5 * LOG2E)).astype(BF16) for c in range(SW_NQ // LANES)]
        for c in range(SW_NKV // LANES):
            blk = _rope_block(proj[:, SW_NQ + c * LANES:SW_NQ + (c + 1) * LANES], cos, sin, first_half)
            kf_ref[r, c * LANES:(c + 1) * LANES] = blk
            k_ref[W + i * W:W + (i + 1) * W, c * LANES:(c + 1) * LANES] = blk.astype(BF16)
        vf_ref[r, :] = v
        v_ref[W + i * W:W + (i + 1) * W, :] = v.astype(BF16)
        return [jnp.concatenate(q[pair * G:(pair + 1) * G], axis=0) for pair in pairs]

    def scores(i, q4):
        out = []
        for pair in pairs:
            k2 = k_ref[i * W:(i + 2) * W, pair * LANES:(pair + 1) * LANES]
            kz = jnp.concatenate([jnp.where(low, k2, zeros), jnp.where(high, k2, zeros)], axis=0)
            out.append(_dot_nt(q4[pair], kz))
        return out

    def softmax_numerators(i, s):
        first_key = jnp.where(t * nsub + i > 0, 0, W)
        mask = band & (ki >= first_key)
        out = []
        for pair in pairs:
            probs, sink_terms = [], []
            for half in range(2):
                sh = jnp.where(mask, s[pair][:, half * K2:(half + 1) * K2], -jnp.inf)
                sk = sinks[pair][half]
                mx = jnp.maximum(jnp.max(sh, axis=1, keepdims=True), sk)
                probs.append(jnp.exp2(sh - mx).astype(BF16))
                sink_terms.append(jnp.exp2(sk - mx))
            out.append((jnp.concatenate(probs, axis=1), jnp.where(low, sink_terms[0], sink_terms[1])))
        return out

    def weighted_values(i, soft):
        out = []
        for pair in pairs:
            v2 = v_ref[i * W:(i + 2) * W, pair * LANES:(pair + 1) * LANES]
            vz = jnp.concatenate(
                [jnp.concatenate([jnp.where(low, v2, zeros), jnp.broadcast_to(ones_low, v2.shape)], axis=1),
                 jnp.concatenate([jnp.where(high, v2, zeros), jnp.broadcast_to(ones_high, v2.shape)], axis=1)],
                axis=0)
            out.append((_dot(soft[pair][0], vz), soft[pair][1]))
        return out

    def finish(i, weighted):
        blocks = []
        for o, sink_term in weighted:
            both = (o[:, :LANES] / (o[:, LANES:] + sink_term)).astype(BF16)
            blocks += [both[g * W:(g + 1) * W] for g in range(G)]
        y = _dot(jnp.concatenate(blocks, axis=1), wout_ref[...])
        r = slice(i * W, (i + 1) * W)
        out_ref[r, :] = x_ref[r, :] + _rms(y, gpost_ref[...])

    s = scores(0, project(0))
    weighted = None
    for i in range(nsub):
        q_next = project(i + 1) if i + 1 < nsub else None
        soft = softmax_numerators(i, s)
        if i > 0:
            finish(i - 1, weighted)
        weighted = weighted_values(i, soft)
        if i + 1 < nsub:
            s = scores(i + 1, q_next)
    finish(nsub - 1, weighted)

    k_ref[0:W, :] = k_ref[rows:rows + W, :]
    v_ref[0:W, :] = v_ref[rows:rows + W, :]

    @pl.when(t == pl.num_programs(1) - 1)
    def _():
        kout_ref[0] = kf_ref[rows - W:, :].T
        vout_ref[0] = vf_ref[rows - W:, :].T


def _swa_prompt(h, w, layer):
    b, t, d = h.shape
    rows = SW_ROWS
    nt = t // rows
    assert t % rows == 0 and rows % WINDOW == 0 and t >= WINDOW
    cos, sin = _rope_tables(jnp.arange(t, dtype=F32))
    layers = (2 * layer + 1, 2 * layer + 1, layer, layer)
    out, k, v = pl.pallas_call(
        _swa_prompt_kernel,
        grid_spec=pltpu.PrefetchScalarGridSpec(
            num_scalar_prefetch=1,
            grid=(b, nt),
            in_specs=[pl.BlockSpec((rows, d), lambda i, j, s: (i * nt + j, 0)),
                      pl.BlockSpec((rows, LANES), lambda i, j, s: (j, 0)),
                      pl.BlockSpec((rows, LANES), lambda i, j, s: (j, 0))]
            + [_layer(w[k_], l) for k_, l in zip(_SW_KEYS, layers)],
            out_specs=[pl.BlockSpec((rows, d), lambda i, j, s: (i * nt + j, 0)),
                       pl.BlockSpec((1, SW_NKV, WINDOW), lambda i, j, s: (i, 0, 0)),
                       pl.BlockSpec((1, SW_NKV, WINDOW), lambda i, j, s: (i, 0, 0))],
            scratch_shapes=[pltpu.VMEM((WINDOW + rows, SW_NKV), BF16), pltpu.VMEM((WINDOW + rows, SW_NKV), BF16),
                            pltpu.VMEM((rows, SW_NKV), F32), pltpu.VMEM((rows, SW_NKV), F32)]),
        out_shape=[jax.ShapeDtypeStruct((b * t, d), F32),
                   jax.ShapeDtypeStruct((b, SW_NKV, WINDOW), F32),
                   jax.ShapeDtypeStruct((b, SW_NKV, WINDOW), F32)],
        compiler_params=_params("arbitrary", "arbitrary"),
        name="swa_prompt",
    )(w["sw_sinks_log2"][layer], h.reshape(b * t, d), cos, sin, *[w[k_] for k_ in _SW_KEYS])
    return out.reshape(b, t, d), k, v


def _swa_sample_pre_kernel(x_ref, cos_ref, sin_ref, gpre_ref, win_ref, q_ref, k_ref, v_ref, kt_ref, vt_ref):
    rows = x_ref.shape[0]
    xn = _rms(x_ref[...], gpre_ref[...]).astype(BF16)
    proj = _dot(xn, win_ref[...])
    cos = cos_ref[...]
    sin = sin_ref[...]
    first_half = _first_half_lanes(rows)
    for c in range(SW_NQ // LANES):
        blk = _rope_block(proj[:, c * LANES:(c + 1) * LANES], cos, sin, first_half)
        q_ref[:, c * LANES:(c + 1) * LANES] = blk * (SW_HD ** -0.5)
    for c in range(SW_NKV // LANES):
        k_ref[:, c * LANES:(c + 1) * LANES] = _rope_block(
            proj[:, SW_NQ + c * LANES:SW_NQ + (c + 1) * LANES], cos, sin, first_half)
    v = proj[:, SW_NQ + SW_NKV:]
    v_ref[...] = v
    kt_ref[...] = k_ref[...].T
    vt_ref[...] = v.T


def _swa_sample_attn_kernel(q_ref, kc_ref, vc_ref, knew_ref, vnew_ref, kcol_ref, vcol_ref, sink_ref, *rest):
    o_ref, kout_ref, vout_ref = rest[-3:]
    nbuf = kc_ref.shape[-1]
    pos = lax.broadcasted_iota(jnp.int32, (SW_HEADS, 1), 0)
    grp = 2 * lax.shift_right_logical(pos, jnp.int32(3)) + jnp.bitwise_and(pos, 1)
    sk = sink_ref[:, 0:1]
    lane = lax.broadcasted_iota(jnp.int32, (1, nbuf), 1)
    dist = nbuf - lane
    valid = (dist >= 0) & (dist <= WINDOW)
    newest = lane == nbuf - 1
    for b in range(q_ref.shape[0]):
        q = q_ref[b]
        qb = q.astype(BF16)
        s = jnp.zeros((SW_HEADS, nbuf), F32)
        s_new = jnp.zeros((SW_HEADS, 1), F32)
        for kv in range(SW_KV_HEADS):
            sl = slice(kv * SW_HD, (kv + 1) * SW_HD)
            kt = kc_ref[b, kv]
            s = jnp.where(grp == kv, _dot(qb, kt.astype(BF16)), s)
            s_new = jnp.where(grp == kv, jnp.sum(q * knew_ref[b, :, sl], axis=1, keepdims=True), s_new)
            kout_ref[b, kv] = jnp.where(newest, kcol_ref[0, sl, b:b + 1], pltpu.roll(kt, nbuf - 1, axis=1))
        s = jnp.where(valid, s, -jnp.inf)
        mx = jnp.maximum(jnp.maximum(jnp.max(s, axis=1, keepdims=True), s_new), sk)
        p = jnp.exp(s - mx)
        p_new = jnp.exp(s_new - mx)
        den = jnp.sum(p, axis=1, keepdims=True) + p_new + jnp.exp(sk - mx)
        pb = p.astype(BF16)
        o = jnp.zeros((SW_HEADS, SW_HD), F32)
        for kv in range(SW_KV_HEADS):
            sl = slice(kv * SW_HD, (kv + 1) * SW_HD)
            vt = vc_ref[b, kv]
            o = jnp.where(grp == kv, _dot_nt(pb, vt.astype(BF16)) + p_new * vnew_ref[b, :, sl], o)
            vout_ref[b, kv] = jnp.where(newest, vcol_ref[0, sl, b:b + 1], pltpu.roll(vt, nbuf - 1, axis=1))
        o_ref[b] = o / den


def _out_proj_kernel(x_ref, a_ref, gpost_ref, wout_ref, out_ref):
    y = _dot(a_ref[...].astype(BF16), wout_ref[...])
    out_ref[...] = x_ref[...] + _rms(y, gpost_ref[...])


def _swa_sample(h, kc_all, vc_all, w, layer, k_stack, v_stack):
    b, d = h.shape
    nbuf = kc_all.shape[-1]
    bb = SAMPLE_STATE_BATCH
    nb = b // bb
    assert b % bb == 0 and nbuf == LANES
    cos, sin = _rope_tables(jnp.full((1,), PAST_LEN, F32))
    pre_in = [h, cos, sin]
    pre_out = [(b, SW_NQ), (b, SW_NKV), (b, SW_NKV), (SW_NKV, b), (SW_NKV, b)]
    q, knew, vnew, kt, vt = pl.pallas_call(
        _swa_sample_pre_kernel,
        grid=(1,),
        in_specs=[_resident(a.shape) for a in pre_in]
        + [_layer(w["g_pre_mix"], 2 * layer + 1), _layer(w["sw_win"], layer)],
        out_specs=[_whole(s) for s in pre_out],
        out_shape=[jax.ShapeDtypeStruct(s, F32) for s in pre_out],
        compiler_params=_params("arbitrary"),
        name="swa_sample_pre",
    )(*pre_in, w["g_pre_mix"], w["sw_win"])

    def cols(a):
        return a.reshape(SW_NKV, nb, bb).transpose(1, 0, 2)

    cache_spec = pl.BlockSpec((None, bb, SW_KV_HEADS, SW_HD, nbuf), lambda i: (layer, i, 0, 0, 0))
    args = [q.reshape(b, SW_HEADS, SW_HD), kc_all, vc_all, knew.reshape(b, 1, SW_NKV), vnew.reshape(b, 1, SW_NKV),
            cols(kt), cols(vt), jnp.broadcast_to(w["sw_sinks"][layer][:, None], (SW_HEADS, LANES))]
    specs = [pl.BlockSpec((bb, SW_HEADS, SW_HD), lambda i: (i, 0, 0)), cache_spec, cache_spec,
             pl.BlockSpec((bb, 1, SW_NKV), lambda i: (i, 0, 0)),
             pl.BlockSpec((bb, 1, SW_NKV), lambda i: (i, 0, 0)),
             pl.BlockSpec((1, SW_NKV, bb), lambda i: (i, 0, 0)),
             pl.BlockSpec((1, SW_NKV, bb), lambda i: (i, 0, 0)),
             _resident((SW_HEADS, LANES))]
    aliases = {}
    if k_stack is not None:
        args += [k_stack, v_stack]
        specs += [pl.BlockSpec(memory_space=pl.ANY), pl.BlockSpec(memory_space=pl.ANY)]
        aliases = {len(args) - 2: 1, len(args) - 1: 2}
    o, k_stack, v_stack = pl.pallas_call(
        _swa_sample_attn_kernel,
        grid=(nb,),
        in_specs=specs,
        out_specs=[pl.BlockSpec((bb, SW_HEADS, SW_HD), lambda i: (i, 0, 0)), cache_spec, cache_spec],
        out_shape=[jax.ShapeDtypeStruct((b, SW_HEADS, SW_HD), F32),
                   jax.ShapeDtypeStruct(kc_all.shape, F32), jax.ShapeDtypeStruct(vc_all.shape, F32)],
        input_output_aliases=aliases,
        compiler_params=_params("arbitrary"),
        name="swa_sample_attn",
    )(*args)

    post_in = [h, o.reshape(b, SW_NQ)]
    out = pl.pallas_call(
        _out_proj_kernel,
        grid=(1,),
        in_specs=[_resident(a.shape) for a in post_in]
        + [_layer(w["g_post_mix"], 2 * layer + 1), _layer(w["sw_wout"], layer)],
        out_specs=_whole((b, d)),
        out_shape=jax.ShapeDtypeStruct((b, d), F32),
        compiler_params=_params("arbitrary"),
        name="swa_sample_post",
    )(*post_in, w["g_post_mix"], w["sw_wout"])
    return out, k_stack, v_stack


def _prepare(g_pre_mix, g_post_mix, g_pre_ffn, g_post_ffn, g_ple, ml_w_in, ml_b_gate, ml_w_out, sw_w_in,
             sw_sinks, sw_w_out, ffn_w_up, ffn_conv_w, ffn_conv_b, ffn_w_down, ple_w_proj, ple_w_gate):
    gate_w = ml_w_in[:, :, ML_QKVO:]
    q_cols = _pair_heads(sw_w_in[:, :, :SW_NQ].reshape(sw_w_in.shape[:2] + (SW_HEADS, SW_HD)), 2)
    sw_win = jnp.concatenate([q_cols.reshape(sw_w_in.shape[:2] + (SW_NQ,)), sw_w_in[:, :, SW_NQ:]], axis=-1)
    sw_wout = _pair_heads(sw_w_out.reshape((sw_w_out.shape[0], SW_HEADS, SW_HD, sw_w_out.shape[-1])), 1)
    sinks = _pair_heads(sw_sinks, 1)
    return dict(
        g_pre_mix=g_pre_mix[:, None], g_post_mix=g_post_mix[:, None], g_pre_ffn=g_pre_ffn[:, None],
        g_post_ffn=g_post_ffn[:, None], g_ple=g_ple[:, None],
        ffn_wup=ffn_w_up.astype(BF16), ffn_cw=ffn_conv_w, ffn_cb=ffn_conv_b[:, None],
        ffn_wdown=ffn_w_down.astype(BF16), ple_wgate=ple_w_gate.astype(BF16), ple_wproj=ple_w_proj.astype(BF16),
        ml_win=ml_w_in[:, :, :ML_QKVO].astype(BF16),
        ml_wgc=jnp.pad(gate_w, ((0, 0), (0, 0), (0, LANES - 2 * ML_HEADS))).astype(BF16),
        ml_bgc=jnp.pad(ml_b_gate, ((0, 0), (0, LANES - 2 * ML_HEADS)))[:, None],
        ml_wgr=jnp.swapaxes(gate_w, 1, 2).astype(BF16),
        ml_bgr=jnp.broadcast_to(ml_b_gate[:, :, None], ml_b_gate.shape + (LANES,)),
        ml_wout=ml_w_out.astype(BF16),
        sw_win=sw_win.astype(BF16), sw_wout=sw_wout.reshape(sw_w_out.shape).astype(BF16),
        sw_sinks=sinks, sw_sinks_log2=sinks * LOG2E)


def kernel(x_prompt, x_sample, p_prompt, p_sample, state_mlstm_C, state_mlstm_n, state_mlstm_m, cache_swa_k, cache_swa_v, state_conv, g_pre_mix, g_post_mix, g_pre_ffn, g_post_ffn, g_ple, ml_w_in, ml_b_gate, ml_w_out, sw_w_in, sw_sinks, sw_w_out, ffn_w_up, ffn_conv_w, ffn_conv_b, ffn_w_down, ple_w_proj, ple_w_gate):
    depth = g_pre_mix.shape[0]
    w = _prepare(g_pre_mix, g_post_mix, g_pre_ffn, g_post_ffn, g_ple, ml_w_in, ml_b_gate, ml_w_out, sw_w_in,
                 sw_sinks, sw_w_out, ffn_w_up, ffn_conv_w, ffn_conv_b, ffn_w_down, ple_w_proj, ple_w_gate)
    kc_all = jnp.transpose(cache_swa_k, (0, 1, 3, 4, 2))
    vc_all = jnp.transpose(cache_swa_v, (0, 1, 3, 4, 2))
    hp = x_prompt
    hs = x_sample[:, 0, :]
    cp, np_, mp, ns, ms = [], [], [], [], []
    kp, vp = [], []
    convp, convs = [], []
    c_stack = k_stack = v_stack = None
    for i in range(depth):
        j = i // 2
        if i % 2 == 0:
            hp, c1, n1, m1 = _mlstm_prompt(hp, w, j)
            hs, c_stack, n2, m2 = _mlstm_sample(hs, state_mlstm_C, state_mlstm_n[j], state_mlstm_m[j], w, j, c_stack)
            cp.append(c1); np_.append(n1); mp.append(m1)
            ns.append(n2); ms.append(m2)
        else:
            hp, k1, v1 = _swa_prompt(hp, w, j)
            hs, k_stack, v_stack = _swa_sample(hs, kc_all, vc_all, w, j, k_stack, v_stack)
            kp.append(k1); vp.append(v1)
        hp, cvp = _ffn_prompt(hp, p_prompt, w, i)
        hs, cvs = _ffn_sample(hs, p_sample[i][:, 0, :], state_conv[i], w, i)
        convp.append(cvp); convs.append(cvs)

    def cache_layout(a):
        return jnp.transpose(a, (0, 1, 4, 2, 3))

    kv_shape = (len(kp), x_prompt.shape[0], SW_KV_HEADS, SW_HD, WINDOW)
    return (hp, hs[:, None, :],
            jnp.stack(cp), jnp.stack(np_), jnp.stack(mp),
            cache_layout(jnp.stack(kp).reshape(kv_shape)), cache_layout(jnp.stack(vp).reshape(kv_shape)),
            jnp.stack(convp),
            c_stack, jnp.stack(ns), jnp.stack(ms), cache_layout(k_stack), cache_layout(v_stack), jnp.stack(convs))
```

```python
import math

import jax
import jax.numpy as jnp
from jax import lax
from jax.experimental import pallas as pl
from jax.experimental.pallas import tpu as pltpu

F32 = jnp.float32
BF16 = jnp.bfloat16

EPS = 1e-6
PLE_DIM = 256
ML_HEADS = 4
ML_DK = 128
ML_DV = 256
ML_HK = ML_HEADS * ML_DK
ML_QKVO = 2 * ML_HK + 2 * ML_HEADS * ML_DV
SW_HEADS = 16
SW_KV_HEADS = 4
SW_HD = 64
SW_GROUP = SW_HEADS // SW_KV_HEADS
SW_NQ = SW_HEADS * SW_HD
SW_NKV = SW_KV_HEADS * SW_HD
WINDOW = 128
ROPE_THETA = 10000.0
PAST_LEN = 8192
LOG2E = math.log2(math.e)

LANES = 128
SUBLANES = 8
FFN_CHUNK = 256
FFN_DOWN_GROUP = 4
FFN_ROWS = 512
ML_CHUNK_ROWS = 256
SW_ROWS = 512
SAMPLE_STATE_BATCH = 8
VMEM_LIMIT = 56 * 1024 * 1024

assert SW_KV_HEADS == 4 and SW_GROUP == 4 and 2 * SW_HD == LANES and WINDOW == LANES


def _params(*sem):
    return pltpu.CompilerParams(dimension_semantics=sem, vmem_limit_bytes=VMEM_LIMIT)


def _resident(shape):
    zeros = (0,) * len(shape)
    return pl.BlockSpec(shape, lambda *_: zeros, pipeline_mode=pl.Buffered(1))


def _layer(a, layer):
    zeros = (0,) * (a.ndim - 1)
    return pl.BlockSpec((None,) + a.shape[1:], lambda *_: (layer,) + zeros, pipeline_mode=pl.Buffered(1))


def _whole(shape):
    zeros = (0,) * len(shape)
    return pl.BlockSpec(shape, lambda *_: zeros)


def _rms(x, g):
    return x * lax.rsqrt(jnp.mean(x * x, axis=-1, keepdims=True) + EPS) * g


def _sigmoid(x):
    return 1.0 / (1.0 + jnp.exp(-x))


def _log_sigmoid(x):
    return jnp.minimum(x, 0.0) - jnp.log1p(jnp.exp(-jnp.abs(x)))


def _dot(a, b):
    return jnp.dot(a, b, preferred_element_type=F32)


def _dot_nt(a, b):
    return lax.dot_general(a, b, (((1,), (1,)), ((), ())), preferred_element_type=F32)


def _dot_tn(a, b):
    return lax.dot_general(a, b, (((0,), (0,)), ((), ())), preferred_element_type=F32)


def _cumsum(x, axis):
    n = x.shape[axis]
    idx = lax.broadcasted_iota(jnp.int32, x.shape, axis)
    s = 1
    while s < n:
        x = x + jnp.where(idx >= s, pltpu.roll(x, s, axis=axis), 0.0)
        s *= 2
    return x


def _ffn_tail(x, f, p, gpost_ref, gple_ref, wgate_ref, wproj_ref):
    x1 = x + _rms(f, gpost_ref[...])
    gate = _sigmoid(_dot(_rms(x1, gple_ref[...]).astype(BF16), wgate_ref[...]))
    return x1 + gate * _dot(p.astype(BF16), wproj_ref[...])


def _ffn_chunk_cols(j, hidden):
    return (slice(j * FFN_CHUNK, (j + 1) * FFN_CHUNK),
            slice(hidden + j * FFN_CHUNK, hidden + (j + 1) * FFN_CHUNK))


def _ffn_prompt_kernel(x_ref, p_ref, gpre_ref, gpost_ref, gple_ref, wup_ref, cw_ref, cb_ref, wdown_ref,
                       wgate_ref, wproj_ref, out_ref, conv_ref, carry_ref, xp_ref, pp_ref, acc_ref):
    t = pl.program_id(1)
    rows = x_ref.shape[0]
    hidden = wdown_ref.shape[0]
    nch = hidden // FFN_CHUNK
    S = SUBLANES
    G = rows // S

    @pl.when(t == 0)
    def _():
        carry_ref[...] = jnp.zeros_like(carry_ref)

    def interleaved(i):
        return pl.ds(((S * i) % G) * S + (S * i) // G, S, stride=S)

    for i in range(G):
        for c in range(xp_ref.shape[0]):
            xp_ref[c, interleaved(i), :] = x_ref[S * i:S * (i + 1), c * LANES:(c + 1) * LANES]
        for c in range(pp_ref.shape[0]):
            pp_ref[c, interleaved(i), :] = p_ref[S * i:S * (i + 1), c * LANES:(c + 1) * LANES]
    x = jnp.concatenate([xp_ref[c] for c in range(xp_ref.shape[0])], axis=1)
    p = jnp.concatenate([pp_ref[c] for c in range(pp_ref.shape[0])], axis=1)
    ple = _dot(p.astype(BF16), wproj_ref[...])
    xn = _rms(x, gpre_ref[...]).astype(BF16)
    first = lax.broadcasted_iota(jnp.int32, (S, FFN_CHUNK), 0) == 0

    def conv(u, cols):
        prev = carry_ref[:, cols]
        back1 = jnp.where(first, prev[2 * S - 1:2 * S], pltpu.roll(u[rows - S:], 1, axis=0))
        back2 = jnp.where(first, prev[S - 1:S], pltpu.roll(u[rows - 2 * S:rows - S], 1, axis=0))
        u1 = jnp.concatenate([back1, u[:rows - S]], axis=0)
        u2 = jnp.concatenate([back2, back1, u[:rows - 2 * S]], axis=0)
        carry_ref[:, cols] = u[rows - 2 * S:]
        cw = cw_ref[:, cols]
        return cb_ref[:, cols] + u2 * cw[0:1] + u1 * cw[1:2] + u * cw[2:3]

    def up(j):
        return [_dot(xn, wup_ref[:, cols]) for cols in _ffn_chunk_cols(j, hidden)]

    u = up(0)
    pending = []
    for j in range(nch):
        u_next = up(j + 1) if j + 1 < nch else None
        gcols, vcols = _ffn_chunk_cols(j, hidden)
        cg = conv(u[0], gcols)
        pending.append((cg * _sigmoid(cg) * conv(u[1], vcols)).astype(BF16))
        if len(pending) == FFN_DOWN_GROUP or j == nch - 1:
            lo = (j + 1 - len(pending)) * FFN_CHUNK
            d = _dot(jnp.concatenate(pending, axis=1), wdown_ref[lo:(j + 1) * FFN_CHUNK, :])
            if lo == 0:
                acc_ref[...] = d
            else:
                acc_ref[...] += d
            pending = []
        u = u_next

    half_rows = rows // 2
    for part in range(2):
        r = slice(part * half_rows, (part + 1) * half_rows)
        x1 = x[r] + _rms(acc_ref[r, :], gpost_ref[...])
        gate = _sigmoid(_dot(_rms(x1, gple_ref[...]).astype(BF16), wgate_ref[...]))
        res = x1 + gate * ple[r]
        for c in range(xp_ref.shape[0]):
            xp_ref[c, r, :] = res[:, c * LANES:(c + 1) * LANES]
    for i in range(G):
        for c in range(xp_ref.shape[0]):
            out_ref[S * i:S * (i + 1), c * LANES:(c + 1) * LANES] = xp_ref[c, interleaved(i), :]

    @pl.when(t == pl.num_programs(1) - 1)
    def _():
        conv_ref[0] = carry_ref[...]


def _ffn_sample_kernel(x_ref, p_ref, prev0_ref, prev1_ref, gpre_ref, gpost_ref, gple_ref, wup_ref, cw_ref,
                       cb_ref, wdown_ref, wgate_ref, wproj_ref, out_ref, u_ref, acc_ref):
    hidden = wdown_ref.shape[0]
    x = x_ref[...]
    xn = _rms(x, gpre_ref[...]).astype(BF16)

    def conv(cols):
        u = _dot(xn, wup_ref[:, cols])
        u_ref[:, cols] = u
        cw = cw_ref[:, cols]
        return cb_ref[:, cols] + prev0_ref[:, cols] * cw[0:1] + prev1_ref[:, cols] * cw[1:2] + u * cw[2:3]

    for j in range(hidden // FFN_CHUNK):
        gcols, vcols = _ffn_chunk_cols(j, hidden)
        cg = conv(gcols)
        h = cg * _sigmoid(cg) * conv(vcols)
        d = _dot(h.astype(BF16), wdown_ref[j * FFN_CHUNK:(j + 1) * FFN_CHUNK, :])
        if j == 0:
            acc_ref[...] = d
        else:
            acc_ref[...] += d
    out_ref[...] = _ffn_tail(x, acc_ref[...], p_ref[...], gpost_ref, gple_ref, wgate_ref, wproj_ref)


_FFN_KEYS = ("g_pre_ffn", "g_post_ffn", "g_ple", "ffn_wup", "ffn_cw", "ffn_cb", "ffn_wdown", "ple_wgate",
             "ple_wproj")


def _ffn_prompt(h, p_all, w, layer):
    b, t, d = h.shape
    f2 = w["ffn_wup"].shape[-1]
    nt = t // FFN_ROWS
    assert t % FFN_ROWS == 0 and (FFN_ROWS // SUBLANES) % SUBLANES == 0
    out, conv = pl.pallas_call(
        _ffn_prompt_kernel,
        grid=(b, nt),
        in_specs=[pl.BlockSpec((FFN_ROWS, d), lambda i, j: (i * nt + j, 0)),
                  pl.BlockSpec((FFN_ROWS, PLE_DIM), lambda i, j: ((layer * b + i) * nt + j, 0))]
        + [_layer(w[k], layer) for k in _FFN_KEYS],
        out_specs=[pl.BlockSpec((FFN_ROWS, d), lambda i, j: (i * nt + j, 0)),
                   pl.BlockSpec((1, 2 * SUBLANES, f2), lambda i, j: (i, 0, 0))],
        out_shape=[jax.ShapeDtypeStruct((b * t, d), F32),
                   jax.ShapeDtypeStruct((b, 2 * SUBLANES, f2), F32)],
        scratch_shapes=[pltpu.VMEM((2 * SUBLANES, f2), F32),
                        pltpu.VMEM((d // LANES, FFN_ROWS, LANES), F32),
                        pltpu.VMEM((PLE_DIM // LANES, FFN_ROWS, LANES), F32), pltpu.VMEM((FFN_ROWS, d), F32)],
        compiler_params=_params("arbitrary", "arbitrary"),
        name="ffn_prompt",
    )(h.reshape(b * t, d), p_all.reshape(-1, PLE_DIM), *[w[k] for k in _FFN_KEYS])
    return out.reshape(b, t, d), conv[:, SUBLANES - 1::SUBLANES, :]


def _ffn_sample(h, p, prev, w, layer):
    b, d = h.shape
    f2 = w["ffn_wup"].shape[-1]
    args = [h, p, prev[:, 0], prev[:, 1]]
    out, u = pl.pallas_call(
        _ffn_sample_kernel,
        grid=(1,),
        in_specs=[_resident(a.shape) for a in args] + [_layer(w[k], layer) for k in _FFN_KEYS],
        out_specs=[_whole((b, d)), _whole((b, f2))],
        out_shape=[jax.ShapeDtypeStruct((b, d), F32), jax.ShapeDtypeStruct((b, f2), F32)],
        scratch_shapes=[pltpu.VMEM((b, d), F32)],
        compiler_params=_params("arbitrary"),
        name="ffn_sample",
    )(*args, *[w[k] for k in _FFN_KEYS])
    return out, jnp.stack([prev[:, 1], u], axis=1)


_ML_KEYS = ("g_pre_mix", "g_post_mix", "ml_win", "ml_wgc", "ml_bgc", "ml_wgr", "ml_bgr", "ml_wout")


def _mlstm_prompt_kernel(x_ref, gpre_ref, gpost_ref, win_ref, wgc_ref, bgc_ref, wgr_ref, bgr_ref, wout_ref,
                         out_ref, c_out_ref, n_out_ref, m_out_ref, c_ref, n_ref, m_ref, h_ref):
    t = pl.program_id(1)
    L = x_ref.shape[0]

    @pl.when(t == 0)
    def _():
        c_ref[...] = jnp.zeros_like(c_ref)
        n_ref[...] = jnp.zeros_like(n_ref)
        m_ref[...] = jnp.zeros_like(m_ref)

    x = x_ref[...]
    xn = _rms(x, gpre_ref[...]).astype(BF16)
    gc = _dot(xn, wgc_ref[...]) + bgc_ref[...]
    gr = _dot_nt(wgr_ref[...], xn) + bgr_ref[:, 0:1]
    proj_qk = _dot(xn, win_ref[:, :2 * ML_HK])
    is_forget_r = lax.broadcasted_iota(jnp.int32, gr.shape, 0) >= ML_HEADS
    bc = _cumsum(_log_sigmoid(gc), 0)
    br = _cumsum(jnp.where(is_forget_r, _log_sigmoid(gr), 0.0), 1)
    causal = lax.broadcasted_iota(jnp.int32, (L, L), 0) >= lax.broadcasted_iota(jnp.int32, (L, L), 1)

    heads = range(ML_HEADS)
    q = [proj_qk[:, h * ML_DK:(h + 1) * ML_DK] * (ML_DK ** -0.5) for h in heads]
    k = [proj_qk[:, ML_HK + h * ML_DK:ML_HK + (h + 1) * ML_DK] for h in heads]
    qb = [a.astype(BF16) for a in q]
    qk = [_dot_nt(qb[h], k[h].astype(BF16)) for h in heads]
    qc = [_dot(qb[h], c_ref[h].astype(BF16)) for h in heads]
    proj_vo = _dot(xn, win_ref[:, 2 * ML_HK:])
    vb = [proj_vo[:, h * ML_DV:(h + 1) * ML_DV].astype(BF16) for h in heads]

    s_b, wk_b, w_inter, inv_den, w_old, n_new, m_new = [], [], [], [], [], [], []
    for h in heads:
        ig_r = gr[h:h + 1, :]
        b_r = br[ML_HEADS + h:ML_HEADS + h + 1, :]
        ig_c = gc[:, h:h + 1]
        b_c = bc[:, ML_HEADS + h:ML_HEADS + h + 1]
        m_prev = m_ref[h:h + 1, 0:1]
        n_prev = n_ref[h:h + 1, :]

        d = jnp.where(causal, b_c + (ig_r - b_r), -jnp.inf)
        a = b_c + m_prev
        m_t = jnp.maximum(a, jnp.max(d, axis=1, keepdims=True))
        wi = jnp.exp(a - m_t)
        s = qk[h] * jnp.exp(d - m_t)
        qn = wi * jnp.sum(q[h] * n_prev, axis=1, keepdims=True) + jnp.sum(s, axis=1, keepdims=True)
        w_inter.append(wi)
        inv_den.append(1.0 / jnp.maximum(jnp.abs(qn), jnp.exp(-m_t)))
        s_b.append(s.astype(BF16))

        b_last = b_r[:, L - 1:L]
        g_r = b_last - b_r + ig_r
        g_c = b_last - b_c + ig_c
        mn = jnp.maximum(b_last + m_prev, jnp.max(g_r, axis=1, keepdims=True))
        wo = jnp.exp(b_last + m_prev - mn)
        wk = jnp.exp(g_c - mn) * k[h]
        wk_b.append(wk.astype(BF16))
        w_old.append(wo)
        n_new.append(wo * n_prev + jnp.sum(wk, axis=0, keepdims=True))
        m_new.append(mn)

    sv = [_dot(s_b[h], vb[h]) for h in heads]
    kv = [_dot_tn(wk_b[h], vb[h]) for h in heads]
    for h in heads:
        h_ref[:, h * ML_DV:(h + 1) * ML_DV] = (w_inter[h] * qc[h] + sv[h]) * inv_den[h]
        c_ref[h] = w_old[h] * c_ref[h] + kv[h]
        n_ref[h:h + 1, :] = n_new[h]
        m_ref[h:h + 1, :] = jnp.broadcast_to(m_new[h], (1, LANES))

    o = proj_vo[:, ML_HEADS * ML_DV:]
    y = _dot((_sigmoid(o) * h_ref[...]).astype(BF16), wout_ref[...])
    out_ref[...] = x + _rms(y, gpost_ref[...])

    @pl.when(t == pl.num_programs(1) - 1)
    def _():
        c_out_ref[0] = c_ref[...]
        n_out_ref[0] = n_ref[0:ML_HEADS, :]
        m_out_ref[0] = m_ref[...]


def _ml_layers(layer):
    return [2 * layer, 2 * layer] + [layer] * (len(_ML_KEYS) - 2)


def _mlstm_prompt(h, w, layer):
    b, t, d = h.shape
    L = ML_CHUNK_ROWS
    nt = t // L
    assert t % L == 0
    out, c, n, m = pl.pallas_call(
        _mlstm_prompt_kernel,
        grid=(b, nt),
        in_specs=[pl.BlockSpec((L, d), lambda i, j: (i * nt + j, 0))]
        + [_layer(w[k], l) for k, l in zip(_ML_KEYS, _ml_layers(layer))],
        out_specs=[pl.BlockSpec((L, d), lambda i, j: (i * nt + j, 0)),
                   pl.BlockSpec((1, ML_HEADS, ML_DK, ML_DV), lambda i, j: (i, 0, 0, 0)),
                   pl.BlockSpec((1, ML_HEADS, ML_DK), lambda i, j: (i, 0, 0)),
                   pl.BlockSpec((1, SUBLANES, LANES), lambda i, j: (i, 0, 0))],
        out_shape=[jax.ShapeDtypeStruct((b * t, d), F32),
                   jax.ShapeDtypeStruct((b, ML_HEADS, ML_DK, ML_DV), F32),
                   jax.ShapeDtypeStruct((b, ML_HEADS, ML_DK), F32),
                   jax.ShapeDtypeStruct((b, SUBLANES, LANES), F32)],
        scratch_shapes=[pltpu.VMEM((ML_HEADS, ML_DK, ML_DV), F32), pltpu.VMEM((SUBLANES, ML_DK), F32),
                        pltpu.VMEM((SUBLANES, LANES), F32), pltpu.VMEM((L, ML_HEADS * ML_DV), F32)],
        compiler_params=_params("arbitrary", "arbitrary"),
        name="mlstm_prompt",
    )(h.reshape(b * t, d), *[w[k] for k in _ML_KEYS])
    return out.reshape(b, t, d), c, n, m[:, :ML_HEADS, 0]


def _mlstm_sample_pre_kernel(x_ref, n_ref, m_ref, gpre_ref, win_ref, wgc_ref, bgc_ref,
                             q_ref, kw_ref, v_ref, o_ref, wold_ref, scal_ref, nnew_ref):
    H = ML_HEADS
    xn = _rms(x_ref[...], gpre_ref[...]).astype(BF16)
    proj = _dot(xn, win_ref[...])
    gates = _dot(xn, wgc_ref[...]) + bgc_ref[...]
    v_ref[...] = proj[:, 2 * ML_HK:2 * ML_HK + H * ML_DV]
    o_ref[...] = proj[:, 2 * ML_HK + H * ML_DV:]
    scal_ref[...] = jnp.zeros_like(scal_ref)
    for h in range(H):
        q = proj[:, h * ML_DK:(h + 1) * ML_DK] * (ML_DK ** -0.5)
        k = proj[:, ML_HK + h * ML_DK:ML_HK + (h + 1) * ML_DK]
        n_prev = n_ref[:, h * ML_DK:(h + 1) * ML_DK]
        ig = gates[:, h:h + 1]
        lf = _log_sigmoid(gates[:, H + h:H + h + 1])
        a = lf + m_ref[:, h:h + 1]
        m_t = jnp.maximum(a, ig)
        w_old = jnp.exp(a - m_t)
        w_new = jnp.exp(ig - m_t)
        s = jnp.sum(q * k, axis=1, keepdims=True) * w_new
        qn = w_old * jnp.sum(q * n_prev, axis=1, keepdims=True) + s
        q_ref[:, h * ML_DK:(h + 1) * ML_DK] = q
        kw_ref[:, h * ML_DK:(h + 1) * ML_DK] = w_new * k
        nnew_ref[:, h * ML_DK:(h + 1) * ML_DK] = w_old * n_prev + w_new * k
        wold_ref[:, h * ML_DV:(h + 1) * ML_DV] = jnp.broadcast_to(w_old, (w_old.shape[0], ML_DV))
        scal_ref[:, h:h + 1] = w_old
        scal_ref[:, H + h:H + h + 1] = s
        scal_ref[:, 2 * H + h:2 * H + h + 1] = jnp.maximum(jnp.abs(qn), jnp.exp(-m_t))
        scal_ref[:, 3 * H + h:3 * H + h + 1] = m_t


def _mlstm_sample_state_kernel(c_ref, qt_ref, kt_ref, v_ref, wold_ref, *rest):
    cnew_ref, qc_ref = rest[-2:]
    for b in range(c_ref.shape[0]):
        for h in range(ML_HEADS):
            c = c_ref[b, h]
            qc = qt_ref[0, h * ML_DK:(h + 1) * ML_DK, b:b + 1]
            kc = kt_ref[0, h * ML_DK:(h + 1) * ML_DK, b:b + 1]
            vr = v_ref[b:b + 1, h * ML_DV:(h + 1) * ML_DV]
            wo = wold_ref[b:b + 1, h * ML_DV:(h + 1) * ML_DV]
            qc_ref[b:b + 1, h * ML_DV:(h + 1) * ML_DV] = jnp.sum(qc * c, axis=0, keepdims=True)
            cnew_ref[b, h] = c * wo + kc * vr


def _mlstm_sample_post_kernel(x_ref, qc_ref, v_ref, o_ref, scal_ref, gpost_ref, wout_ref, out_ref, h_ref):
    H = ML_HEADS
    for h in range(H):
        sl = slice(h * ML_DV, (h + 1) * ML_DV)
        num = scal_ref[:, h:h + 1] * qc_ref[:, sl] + scal_ref[:, H + h:H + h + 1] * v_ref[:, sl]
        h_ref[:, sl] = num / scal_ref[:, 2 * H + h:2 * H + h + 1]
    y = _dot((_sigmoid(o_ref[...]) * h_ref[...]).astype(BF16), wout_ref[...])
    out_ref[...] = x_ref[...] + _rms(y, gpost_ref[...])


def _aliased(stack_prev, args, in_specs):
    if stack_prev is None:
        return {}
    args.append(stack_prev)
    in_specs.append(pl.BlockSpec(memory_space=pl.ANY))
    return {len(args) - 1: 0}


def _mlstm_sample(h, c_all, n, m, w, layer, c_stack):
    b, d = h.shape
    H = ML_HEADS
    bb = SAMPLE_STATE_BATCH
    nb = b // bb
    assert b % bb == 0
    pre_in = [h, n.reshape(b, H * ML_DK), jnp.pad(m, ((0, 0), (0, LANES - H)))]
    pre_keys = ("g_pre_mix", "ml_win", "ml_wgc", "ml_bgc")
    pre_layers = (2 * layer, layer, layer, layer)
    pre_out = [(b, ML_HK), (b, ML_HK), (b, H * ML_DV), (b, H * ML_DV), (b, H * ML_DV), (b, LANES), (b, ML_HK)]
    q, kw, v, o, wold, scal, nnew = pl.pallas_call(
        _mlstm_sample_pre_kernel,
        grid=(1,),
        in_specs=[_resident(a.shape) for a in pre_in] + [_layer(w[k], l) for k, l in zip(pre_keys, pre_layers)],
        out_specs=[_whole(s) for s in pre_out],
        out_shape=[jax.ShapeDtypeStruct(s, F32) for s in pre_out],
        compiler_params=_params("arbitrary"),
        name="mlstm_sample_pre",
    )(*pre_in, *[w[k] for k in pre_keys])

    def cols(a):
        return a.reshape(nb, bb, ML_HK).transpose(0, 2, 1)

    state_args = [c_all, cols(q), cols(kw), v, wold]
    state_specs = [pl.BlockSpec((None, bb, H, ML_DK, ML_DV), lambda i: (layer, i, 0, 0, 0)),
                   pl.BlockSpec((1, ML_HK, bb), lambda i: (i, 0, 0)),
                   pl.BlockSpec((1, ML_HK, bb), lambda i: (i, 0, 0)),
                   pl.BlockSpec((bb, H * ML_DV), lambda i: (i, 0)),
                   pl.BlockSpec((bb, H * ML_DV), lambda i: (i, 0))]
    aliases = _aliased(c_stack, state_args, state_specs)
    c_stack, qc = pl.pallas_call(
        _mlstm_sample_state_kernel,
        grid=(nb,),
        in_specs=state_specs,
        out_specs=[pl.BlockSpec((None, bb, H, ML_DK, ML_DV), lambda i: (layer, i, 0, 0, 0)),
                   pl.BlockSpec((bb, H * ML_DV), lambda i: (i, 0))],
        out_shape=[jax.ShapeDtypeStruct(c_all.shape, F32), jax.ShapeDtypeStruct((b, H * ML_DV), F32)],
        input_output_aliases=aliases,
        compiler_params=_params("arbitrary"),
        name="mlstm_sample_state",
    )(*state_args)

    post_in = [h, qc, v, o, scal]
    out = pl.pallas_call(
        _mlstm_sample_post_kernel,
        grid=(1,),
        in_specs=[_resident(a.shape) for a in post_in]
        + [_layer(w["g_post_mix"], 2 * layer), _layer(w["ml_wout"], layer)],
        out_specs=_whole((b, d)),
        out_shape=jax.ShapeDtypeStruct((b, d), F32),
        scratch_shapes=[pltpu.VMEM((b, H * ML_DV), F32)],
        compiler_params=_params("arbitrary"),
        name="mlstm_sample_post",
    )(*post_in, w["g_post_mix"], w["ml_wout"])
    return out, c_stack, nnew.reshape(b, H, ML_DK), scal[:, 3 * H:4 * H]


def _rope_tables(pos):
    half = SW_HD // 2
    inv = ROPE_THETA ** (-jnp.arange(half, dtype=F32) / half)
    ang = pos[:, None] * inv[None, :]
    cos = jnp.tile(jnp.cos(ang), (1, LANES // half))
    sin = jnp.tile(jnp.concatenate([-jnp.sin(ang), jnp.sin(ang)], axis=1), (1, LANES // SW_HD))
    return cos, sin


def _first_half_lanes(rows):
    lane = lax.broadcasted_iota(jnp.int32, (rows, LANES), 1)
    return jnp.bitwise_and(lane, SW_HD - 1) < SW_HD // 2


def _rope_block(x, cos, sin, first_half):
    partner = jnp.where(first_half, pltpu.roll(x, LANES - SW_HD // 2, axis=1), pltpu.roll(x, SW_HD // 2, axis=1))
    return x * cos + partner * sin


def _pair_heads(a, axis):
    shp = a.shape
    a = a.reshape(shp[:axis] + (2, 2, SW_GROUP) + shp[axis + 1:])
    a = jnp.swapaxes(a, axis + 1, axis + 2)
    return a.reshape(shp)


_SW_KEYS = ("g_pre_mix", "g_post_mix", "sw_win", "sw_wout")


def _swa_prompt_kernel(sink_ref, x_ref, cos_ref, sin_ref, gpre_ref, gpost_ref, win_ref, wout_ref,
                       out_ref, kout_ref, vout_ref, k_ref, v_ref, kf_ref, vf_ref):
    t = pl.program_id(1)
    rows = x_ref.shape[0]
    nsub = rows // WINDOW
    W = WINDOW
    G = SW_GROUP
    K2 = 2 * W
    pairs = range(SW_KV_HEADS // 2)

    @pl.when(t == 0)
    def _():
        k_ref[0:W, :] = jnp.zeros((W, SW_NKV), BF16)
        v_ref[0:W, :] = jnp.zeros((W, SW_NKV), BF16)

    first_half = _first_half_lanes(W)
    qi = jnp.bitwise_and(lax.broadcasted_iota(jnp.int32, (G * W, K2), 0), W - 1)
    ki = lax.broadcasted_iota(jnp.int32, (G * W, K2), 1)
    band = (ki >= qi) & (ki <= qi + W)
    lane = lax.broadcasted_iota(jnp.int32, (1, LANES), 1)
    low = lane < SW_HD
    high = jnp.logical_not(low)
    zeros = jnp.zeros((1, LANES), BF16)
    ones_low = jnp.where(low, 1.0, 0.0).astype(BF16)
    ones_high = jnp.where(high, 1.0, 0.0).astype(BF16)
    member = lax.shift_right_logical(lax.broadcasted_iota(jnp.int32, (G * W, 1), 0),
                                     jnp.int32(W.bit_length() - 1))

    def sink_column(pair, half):
        col = jnp.full((G * W, 1), sink_ref[2 * (pair * G) + half], F32)
        for g in range(1, G):
            col = jnp.where(member == g, sink_ref[2 * (pair * G + g) + half], col)
        return col

    sinks = [[sink_column(pair, half) for half in range(2)] for pair in pairs]

    def project(i):
        r = slice(i * W, (i + 1) * W)
        xn = _rms(x_ref[r, :], gpre_ref[...]).astype(BF16)
        proj = _dot(xn, win_ref[:, :SW_NQ + SW_NKV])
        v = _dot(xn, win_ref[:, SW_NQ + SW_NKV:])
        cos = cos_ref[r, :]
        sin = sin_ref[r, :]
        q = [(_rope_block(proj[:, c * LANES:(c + 1) * LANES], cos, sin, first_half)
              * (SW_HD ** -0.5 * LOG2E)).astype(BF16) for c in range(SW_NQ // LANES)]
        for c in range(SW_NKV // LANES):
            blk = _rope_block(proj[:, SW_NQ + c * LANES:SW_NQ + (c + 1) * LANES], cos, sin, first_half)
            kf_ref[r, c * LANES:(c + 1) * LANES] = blk
            k_ref[W + i * W:W + (i + 1) * W, c * LANES:(c + 1) * LANES] = blk.astype(BF16)
        vf_ref[r, :] = v
        v_ref[W + i * W:W + (i + 1) * W, :] = v.astype(BF16)
        return [jnp.concatenate(q[pair * G:(pair + 1) * G], axis=0) for pair in pairs]

    def scores(i, q4):
        out = []
        for pair in pairs:
            k2 = k_ref[i * W:(i + 2) * W, pair * LANES:(pair + 1) * LANES]
            kz = jnp.concatenate([jnp.where(low, k2, zeros), jnp.where(high, k2, zeros)], axis=0)
            out.append(_dot_nt(q4[pair], kz))
        return out

    def softmax_numerators(i, s):
        first_key = jnp.where(t * nsub + i > 0, 0, W)
        mask = band & (ki >= first_key)
        out = []
        for pair in pairs:
            probs, sink_terms = [], []
            for half in range(2):
                sh = jnp.where(mask, s[pair][:, half * K2:(half + 1) * K2], -jnp.inf)
                sk = sinks[pair][half]
                mx = jnp.maximum(jnp.max(sh, axis=1, keepdims=True), sk)
                probs.append(jnp.exp2(sh - mx).astype(BF16))
                sink_terms.append(jnp.exp2(sk - mx))
            out.append((jnp.concatenate(probs, axis=1), jnp.where(low, sink_terms[0], sink_terms[1])))
        return out

    def weighted_values(i, soft):
        out = []
        for pair in pairs:
            v2 = v_ref[i * W:(i + 2) * W, pair * LANES:(pair + 1) * LANES]
            vz = jnp.concatenate(
                [jnp.concatenate([jnp.where(low, v2, zeros), jnp.broadcast_to(ones_low, v2.shape)], axis=1),
                 jnp.concatenate([jnp.where(high, v2, zeros), jnp.broadcast_to(ones_high, v2.shape)], axis=1)],
                axis=0)
            out.append((_dot(soft[pair][0], vz), soft[pair][1]))
        return out

    def finish(i, weighted):
        blocks = []
        for o, sink_term in weighted:
            both = (o[:, :LANES] / (o[:, LANES:] + sink_term)).astype(BF16)
            blocks += [both[g * W:(g + 1) * W] for g in range(G)]
        y = _dot(jnp.concatenate(blocks, axis=1), wout_ref[...])
        r = slice(i * W, (i + 1) * W)
        out_ref[r, :] = x_ref[r, :] + _rms(y, gpost_ref[...])

    s = scores(0, project(0))
    weighted = None
    for i in range(nsub):
        q_next = project(i + 1) if i + 1 < nsub else None
        soft = softmax_numerators(i, s)
        if i > 0:
            finish(i - 1, weighted)
        weighted = weighted_values(i, soft)
        if i + 1 < nsub:
            s = scores(i + 1, q_next)
    finish(nsub - 1, weighted)

    k_ref[0:W, :] = k_ref[rows:rows + W, :]
    v_ref[0:W, :] = v_ref[rows:rows + W, :]

    @pl.when(t == pl.num_programs(1) - 1)
    def _():
        kout_ref[0] = kf_ref[rows - W:, :].T
        vout_ref[0] = vf_ref[rows - W:, :].T


def _swa_prompt(h, w, layer):
    b, t, d = h.shape
    rows = SW_ROWS
    nt = t // rows
    assert t % rows == 0 and rows % WINDOW == 0 and t >= WINDOW
    cos, sin = _rope_tables(jnp.arange(t, dtype=F32))
    layers = (2 * layer + 1, 2 * layer + 1, layer, layer)
    out, k, v = pl.pallas_call(
        _swa_prompt_kernel,
        grid_spec=pltpu.PrefetchScalarGridSpec(
            num_scalar_prefetch=1,
            grid=(b, nt),
            in_specs=[pl.BlockSpec((rows, d), lambda i, j, s: (i * nt + j, 0)),
                      pl.BlockSpec((rows, LANES), lambda i, j, s: (j, 0)),
                      pl.BlockSpec((rows, LANES), lambda i, j, s: (j, 0))]
            + [_layer(w[k_], l) for k_, l in zip(_SW_KEYS, layers)],
            out_specs=[pl.BlockSpec((rows, d), lambda i, j, s: (i * nt + j, 0)),
                       pl.BlockSpec((1, SW_NKV, WINDOW), lambda i, j, s: (i, 0, 0)),
                       pl.BlockSpec((1, SW_NKV, WINDOW), lambda i, j, s: (i, 0, 0))],
            scratch_shapes=[pltpu.VMEM((WINDOW + rows, SW_NKV), BF16), pltpu.VMEM((WINDOW + rows, SW_NKV), BF16),
                            pltpu.VMEM((rows, SW_NKV), F32), pltpu.VMEM((rows, SW_NKV), F32)]),
        out_shape=[jax.ShapeDtypeStruct((b * t, d), F32),
                   jax.ShapeDtypeStruct((b, SW_NKV, WINDOW), F32),
                   jax.ShapeDtypeStruct((b, SW_NKV, WINDOW), F32)],
        compiler_params=_params("arbitrary", "arbitrary"),
        name="swa_prompt",
    )(w["sw_sinks_log2"][layer], h.reshape(b * t, d), cos, sin, *[w[k_] for k_ in _SW_KEYS])
    return out.reshape(b, t, d), k, v


def _swa_sample_pre_kernel(x_ref, cos_ref, sin_ref, gpre_ref, win_ref, q_ref, k_ref, v_ref, kt_ref, vt_ref):
    rows = x_ref.shape[0]
    xn = _rms(x_ref[...], gpre_ref[...]).astype(BF16)
    proj = _dot(xn, win_ref[...])
    cos = cos_ref[...]
    sin = sin_ref[...]
    first_half = _first_half_lanes(rows)
    for c in range(SW_NQ // LANES):
        blk = _rope_block(proj[:, c * LANES:(c + 1) * LANES], cos, sin, first_half)
        q_ref[:, c * LANES:(c + 1) * LANES] = blk * (SW_HD ** -0.5)
    for c in range(SW_NKV // LANES):
        k_ref[:, c * LANES:(c + 1) * LANES] = _rope_block(
            proj[:, SW_NQ + c * LANES:SW_NQ + (c + 1) * LANES], cos, sin, first_half)
    v = proj[:, SW_NQ + SW_NKV:]
    v_ref[...] = v
    kt_ref[...] = k_ref[...].T
    vt_ref[...] = v.T


def _swa_sample_attn_kernel(q_ref, kc_ref, vc_ref, knew_ref, vnew_ref, kcol_ref, vcol_ref, sink_ref, *rest):
    o_ref, kout_ref, vout_ref = rest[-3:]
    nbuf = kc_ref.shape[-1]
    pos = lax.broadcasted_iota(jnp.int32, (SW_HEADS, 1), 0)
    grp = 2 * lax.shift_right_logical(pos, jnp.int32(3)) + jnp.bitwise_and(pos, 1)
    sk = sink_ref[:, 0:1]
    lane = lax.broadcasted_iota(jnp.int32, (1, nbuf), 1)
    dist = nbuf - lane
    valid = (dist >= 0) & (dist <= WINDOW)
    newest = lane == nbuf - 1
    for b in range(q_ref.shape[0]):
        q = q_ref[b]
        qb = q.astype(BF16)
        s = jnp.zeros((SW_HEADS, nbuf), F32)
        s_new = jnp.zeros((SW_HEADS, 1), F32)
        for kv in range(SW_KV_HEADS):
            sl = slice(kv * SW_HD, (kv + 1) * SW_HD)
            kt = kc_ref[b, kv]
            s = jnp.where(grp == kv, _dot(qb, kt.astype(BF16)), s)
            s_new = jnp.where(grp == kv, jnp.sum(q * knew_ref[b, :, sl], axis=1, keepdims=True), s_new)
            kout_ref[b, kv] = jnp.where(newest, kcol_ref[0, sl, b:b + 1], pltpu.roll(kt, nbuf - 1, axis=1))
        s = jnp.where(valid, s, -jnp.inf)
        mx = jnp.maximum(jnp.maximum(jnp.max(s, axis=1, keepdims=True), s_new), sk)
        p = jnp.exp(s - mx)
        p_new = jnp.exp(s_new - mx)
        den = jnp.sum(p, axis=1, keepdims=True) + p_new + jnp.exp(sk - mx)
        pb = p.astype(BF16)
        o = jnp.zeros((SW_HEADS, SW_HD), F32)
        for kv in range(SW_KV_HEADS):
            sl = slice(kv * SW_HD, (kv + 1) * SW_HD)
            vt = vc_ref[b, kv]
            o = jnp.where(grp == kv, _dot_nt(pb, vt.astype(BF16)) + p_new * vnew_ref[b, :, sl], o)
            vout_ref[b, kv] = jnp.where(newest, vcol_ref[0, sl, b:b + 1], pltpu.roll(vt, nbuf - 1, axis=1))
        o_ref[b] = o / den


def _out_proj_kernel(x_ref, a_ref, gpost_ref, wout_ref, out_ref):
    y = _dot(a_ref[...].astype(BF16), wout_ref[...])
    out_ref[...] = x_ref[...] + _rms(y, gpost_ref[...])


def _swa_sample(h, kc_all, vc_all, w, layer, k_stack, v_stack):
    b, d = h.shape
    nbuf = kc_all.shape[-1]
    bb = SAMPLE_STATE_BATCH
    nb = b // bb
    assert b % bb == 0 and nbuf == LANES
    cos, sin = _rope_tables(jnp.full((1,), PAST_LEN, F32))
    pre_in = [h, cos, sin]
    pre_out = [(b, SW_NQ), (b, SW_NKV), (b, SW_NKV), (SW_NKV, b), (SW_NKV, b)]
    q, knew, vnew, kt, vt = pl.pallas_call(
        _swa_sample_pre_kernel,
        grid=(1,),
        in_specs=[_resident(a.shape) for a in pre_in]
        + [_layer(w["g_pre_mix"], 2 * layer + 1), _layer(w["sw_win"], layer)],
        out_specs=[_whole(s) for s in pre_out],
        out_shape=[jax.ShapeDtypeStruct(s, F32) for s in pre_out],
        compiler_params=_params("arbitrary"),
        name="swa_sample_pre",
    )(*pre_in, w["g_pre_mix"], w["sw_win"])

    def cols(a):
        return a.reshape(SW_NKV, nb, bb).transpose(1, 0, 2)

    cache_spec = pl.BlockSpec((None, bb, SW_KV_HEADS, SW_HD, nbuf), lambda i: (layer, i, 0, 0, 0))
    args = [q.reshape(b, SW_HEADS, SW_HD), kc_all, vc_all, knew.reshape(b, 1, SW_NKV), vnew.reshape(b, 1, SW_NKV),
            cols(kt), cols(vt), jnp.broadcast_to(w["sw_sinks"][layer][:, None], (SW_HEADS, LANES))]
    specs = [pl.BlockSpec((bb, SW_HEADS, SW_HD), lambda i: (i, 0, 0)), cache_spec, cache_spec,
             pl.BlockSpec((bb, 1, SW_NKV), lambda i: (i, 0, 0)),
             pl.BlockSpec((bb, 1, SW_NKV), lambda i: (i, 0, 0)),
             pl.BlockSpec((1, SW_NKV, bb), lambda i: (i, 0, 0)),
             pl.BlockSpec((1, SW_NKV, bb), lambda i: (i, 0, 0)),
             _resident((SW_HEADS, LANES))]
    aliases = {}
    if k_stack is not None:
        args += [k_stack, v_stack]
        specs += [pl.BlockSpec(memory_space=pl.ANY), pl.BlockSpec(memory_space=pl.ANY)]
        aliases = {len(args) - 2: 1, len(args) - 1: 2}
    o, k_stack, v_stack = pl.pallas_call(
        _swa_sample_attn_kernel,
        grid=(nb,),
        in_specs=specs,
        out_specs=[pl.BlockSpec((bb, SW_HEADS, SW_HD), lambda i: (i, 0, 0)), cache_spec, cache_spec],
        out_shape=[jax.ShapeDtypeStruct((b, SW_HEADS, SW_HD), F32),
                   jax.ShapeDtypeStruct(kc_all.shape, F32), jax.ShapeDtypeStruct(vc_all.shape, F32)],
        input_output_aliases=aliases,
        compiler_params=_params("arbitrary"),
        name="swa_sample_attn",
    )(*args)

    post_in = [h, o.reshape(b, SW_NQ)]
    out = pl.pallas_call(
        _out_proj_kernel,
        grid=(1,),
        in_specs=[_resident(a.shape) for a in post_in]
        + [_layer(w["g_post_mix"], 2 * layer + 1), _layer(w["sw_wout"], layer)],
        out_specs=_whole((b, d)),
        out_shape=jax.ShapeDtypeStruct((b, d), F32),
        compiler_params=_params("arbitrary"),
        name="swa_sample_post",
    )(*post_in, w["g_post_mix"], w["sw_wout"])
    return out, k_stack, v_stack


def _prepare(g_pre_mix, g_post_mix, g_pre_ffn, g_post_ffn, g_ple, ml_w_in, ml_b_gate, ml_w_out, sw_w_in,
             sw_sinks, sw_w_out, ffn_w_up, ffn_conv_w, ffn_conv_b, ffn_w_down, ple_w_proj, ple_w_gate):
    gate_w = ml_w_in[:, :, ML_QKVO:]
    q_cols = _pair_heads(sw_w_in[:, :, :SW_NQ].reshape(sw_w_in.shape[:2] + (SW_HEADS, SW_HD)), 2)
    sw_win = jnp.concatenate([q_cols.reshape(sw_w_in.shape[:2] + (SW_NQ,)), sw_w_in[:, :, SW_NQ:]], axis=-1)
    sw_wout = _pair_heads(sw_w_out.reshape((sw_w_out.shape[0], SW_HEADS, SW_HD, sw_w_out.shape[-1])), 1)
    sinks = _pair_heads(sw_sinks, 1)
    return dict(
        g_pre_mix=g_pre_mix[:, None], g_post_mix=g_post_mix[:, None], g_pre_ffn=g_pre_ffn[:, None],
        g_post_ffn=g_post_ffn[:, None], g_ple=g_ple[:, None],
        ffn_wup=ffn_w_up.astype(BF16), ffn_cw=ffn_conv_w, ffn_cb=ffn_conv_b[:, None],
        ffn_wdown=ffn_w_down.astype(BF16), ple_wgate=ple_w_gate.astype(BF16), ple_wproj=ple_w_proj.astype(BF16),
        ml_win=ml_w_in[:, :, :ML_QKVO].astype(BF16),
        ml_wgc=jnp.pad(gate_w, ((0, 0), (0, 0), (0, LANES - 2 * ML_HEADS))).astype(BF16),
        ml_bgc=jnp.pad(ml_b_gate, ((0, 0), (0, LANES - 2 * ML_HEADS)))[:, None],
        ml_wgr=jnp.swapaxes(gate_w, 1, 2).astype(BF16),
        ml_bgr=jnp.broadcast_to(ml_b_gate[:, :, None], ml_b_gate.shape + (LANES,)),
        ml_wout=ml_w_out.astype(BF16),
        sw_win=sw_win.astype(BF16), sw_wout=sw_wout.reshape(sw_w_out.shape).astype(BF16),
        sw_sinks=sinks, sw_sinks_log2=sinks * LOG2E)


def kernel(x_prompt, x_sample, p_prompt, p_sample, state_mlstm_C, state_mlstm_n, state_mlstm_m, cache_swa_k, cache_swa_v, state_conv, g_pre_mix, g_post_mix, g_pre_ffn, g_post_ffn, g_ple, ml_w_in, ml_b_gate, ml_w_out, sw_w_in, sw_sinks, sw_w_out, ffn_w_up, ffn_conv_w, ffn_conv_b, ffn_w_down, ple_w_proj, ple_w_gate):
    depth = g_pre_mix.shape[0]
    w = _prepare(g_pre_mix, g_post_mix, g_pre_ffn, g_post_ffn, g_ple, ml_w_in, ml_b_gate, ml_w_out, sw_w_in,
                 sw_sinks, sw_w_out, ffn_w_up, ffn_conv_w, ffn_conv_b, ffn_w_down, ple_w_proj, ple_w_gate)
    kc_all = jnp.transpose(cache_swa_k, (0, 1, 3, 4, 2))
    vc_all = jnp.transpose(cache_swa_v, (0, 1, 3, 4, 2))
    hp = x_prompt
    hs = x_sample[:, 0, :]
    cp, np_, mp, ns, ms = [], [], [], [], []
    kp, vp = [], []
    convp, convs = [], []
    c_stack = k_stack = v_stack = None
    for i in range(depth):
        j = i // 2
        if i % 2 == 0:
            hp, c1, n1, m1 = _mlstm_prompt(hp, w, j)
            hs, c_stack, n2, m2 = _mlstm_sample(hs, state_mlstm_C, state_mlstm_n[j], state_mlstm_m[j], w, j, c_stack)
            cp.append(c1); np_.append(n1); mp.append(m1)
            ns.append(n2); ms.append(m2)
        else:
            hp, k1, v1 = _swa_prompt(hp, w, j)
            hs, k_stack, v_stack = _swa_sample(hs, kc_all, vc_all, w, j, k_stack, v_stack)
            kp.append(k1); vp.append(v1)
        hp, cvp = _ffn_prompt(hp, p_prompt, w, i)
        hs, cvs = _ffn_sample(hs, p_sample[i][:, 0, :], state_conv[i], w, i)
        convp.append(cvp); convs.append(cvs)

    def cache_layout(a):
        return jnp.transpose(a, (0, 1, 4, 2, 3))

    kv_shape = (len(kp), x_prompt.shape[0], SW_KV_HEADS, SW_HD, WINDOW)
    return (hp, hs[:, None, :],
            jnp.stack(cp), jnp.stack(np_), jnp.stack(mp),
            cache_layout(jnp.stack(kp).reshape(kv_shape)), cache_layout(jnp.stack(vp).reshape(kv_shape)),
            jnp.stack(convp),
            c_stack, jnp.stack(ns), jnp.stack(ms), cache_layout(k_stack), cache_layout(v_stack), jnp.stack(convs))
```

```python
import math

import jax
import jax.numpy as jnp
from jax import lax
from jax.experimental import pallas as pl
from jax.experimental.pallas import tpu as pltpu

F32 = jnp.float32
BF16 = jnp.bfloat16

EPS = 1e-6
PLE_DIM = 256
ML_HEADS = 4
ML_DK = 128
ML_DV = 256
ML_HK = ML_HEADS * ML_DK
ML_QKVO = 2 * ML_HK + 2 * ML_HEADS * ML_DV
SW_HEADS = 16
SW_KV_HEADS = 4
SW_HD = 64
SW_GROUP = SW_HEADS // SW_KV_HEADS
SW_NQ = SW_HEADS * SW_HD
SW_NKV = SW_KV_HEADS * SW_HD
WINDOW = 128
ROPE_THETA = 10000.0
PAST_LEN = 8192
LOG2E = math.log2(math.e)

LANES = 128
SUBLANES = 8
FFN_CHUNK = 256
FFN_DOWN_GROUP = 4
FFN_ROWS = 512
ML_CHUNK_ROWS = 256
ML_STEP_ROWS = 1024
SW_ROWS = 1024
SW_UNIT = 2
SAMPLE_STATE_BATCH = 8
VMEM_LIMIT = 56 * 1024 * 1024

assert SW_KV_HEADS == 4 and SW_GROUP == 4 and 2 * SW_HD == LANES and WINDOW == LANES


def _params(*sem):
    return pltpu.CompilerParams(dimension_semantics=sem, vmem_limit_bytes=VMEM_LIMIT)


def _resident(shape):
    zeros = (0,) * len(shape)
    return pl.BlockSpec(shape, lambda *_: zeros, pipeline_mode=pl.Buffered(1))


def _layer(a, layer):
    zeros = (0,) * (a.ndim - 1)
    return pl.BlockSpec((None,) + a.shape[1:], lambda *_: (layer,) + zeros, pipeline_mode=pl.Buffered(1))


def _whole(shape):
    zeros = (0,) * len(shape)
    return pl.BlockSpec(shape, lambda *_: zeros)


def _rms(x, g):
    return x * lax.rsqrt(jnp.mean(x * x, axis=-1, keepdims=True) + EPS) * g


def _sigmoid(x):
    return 1.0 / (1.0 + jnp.exp(-x))


def _log_sigmoid(x):
    return jnp.minimum(x, 0.0) - jnp.log1p(jnp.exp(-jnp.abs(x)))


def _dot(a, b):
    return jnp.dot(a, b, preferred_element_type=F32)


def _dot_nt(a, b):
    return lax.dot_general(a, b, (((1,), (1,)), ((), ())), preferred_element_type=F32)


def _dot_tn(a, b):
    return lax.dot_general(a, b, (((0,), (0,)), ((), ())), preferred_element_type=F32)


def _cumsum(x, axis):
    n = x.shape[axis]
    idx = lax.broadcasted_iota(jnp.int32, x.shape, axis)
    s = 1
    while s < n:
        x = x + jnp.where(idx >= s, pltpu.roll(x, s, axis=axis), 0.0)
        s *= 2
    return x


def _ffn_tail(x, f, p, gpost_ref, gple_ref, wgate_ref, wproj_ref):
    x1 = x + _rms(f, gpost_ref[...])
    gate = _sigmoid(_dot(_rms(x1, gple_ref[...]).astype(BF16), wgate_ref[...]))
    return x1 + gate * _dot(p.astype(BF16), wproj_ref[...])


def _ffn_chunk_cols(j, hidden):
    return (slice(j * FFN_CHUNK, (j + 1) * FFN_CHUNK),
            slice(hidden + j * FFN_CHUNK, hidden + (j + 1) * FFN_CHUNK))


def _ffn_prompt_kernel(x_ref, p_ref, gpre_ref, gpost_ref, gple_ref, wup_ref, cw_ref, cb_ref, wdown_ref,
                       wgate_ref, wproj_ref, out_ref, conv_ref, carry_ref, xp_ref, pp_ref, acc_ref):
    t = pl.program_id(1)
    rows = x_ref.shape[0]
    hidden = wdown_ref.shape[0]
    nch = hidden // FFN_CHUNK
    S = SUBLANES
    G = rows // S

    @pl.when(t == 0)
    def _():
        carry_ref[...] = jnp.zeros_like(carry_ref)

    def interleaved(i):
        return pl.ds(((S * i) % G) * S + (S * i) // G, S, stride=S)

    for i in range(G):
        for c in range(xp_ref.shape[0]):
            xp_ref[c, interleaved(i), :] = x_ref[S * i:S * (i + 1), c * LANES:(c + 1) * LANES]
        for c in range(pp_ref.shape[0]):
            pp_ref[c, interleaved(i), :] = p_ref[S * i:S * (i + 1), c * LANES:(c + 1) * LANES]
    x = jnp.concatenate([xp_ref[c] for c in range(xp_ref.shape[0])], axis=1)
    p = jnp.concatenate([pp_ref[c] for c in range(pp_ref.shape[0])], axis=1)
    ple = _dot(p.astype(BF16), wproj_ref[...])
    xn = _rms(x, gpre_ref[...]).astype(BF16)
    first = lax.broadcasted_iota(jnp.int32, (S, FFN_CHUNK), 0) == 0

    def conv(u, cols):
        prev = carry_ref[:, cols]
        back1 = jnp.where(first, prev[2 * S - 1:2 * S], pltpu.roll(u[rows - S:], 1, axis=0))
        back2 = jnp.where(first, prev[S - 1:S], pltpu.roll(u[rows - 2 * S:rows - S], 1, axis=0))
        u1 = jnp.concatenate([back1, u[:rows - S]], axis=0)
        u2 = jnp.concatenate([back2, back1, u[:rows - 2 * S]], axis=0)
        carry_ref[:, cols] = u[rows - 2 * S:]
        cw = cw_ref[:, cols]
        return cb_ref[:, cols] + u2 * cw[0:1] + u1 * cw[1:2] + u * cw[2:3]

    def up(j):
        return [_dot(xn, wup_ref[:, cols]) for cols in _ffn_chunk_cols(j, hidden)]

    u = up(0)
    pending = []
    for j in range(nch):
        u_next = up(j + 1) if j + 1 < nch else None
        gcols, vcols = _ffn_chunk_cols(j, hidden)
        cg = conv(u[0], gcols)
        pending.append((cg * _sigmoid(cg) * conv(u[1], vcols)).astype(BF16))
        if len(pending) == FFN_DOWN_GROUP or j == nch - 1:
            lo = (j + 1 - len(pending)) * FFN_CHUNK
            d = _dot(jnp.concatenate(pending, axis=1), wdown_ref[lo:(j + 1) * FFN_CHUNK, :])
            if lo == 0:
                acc_ref[...] = d
            else:
                acc_ref[...] += d
            pending = []
        u = u_next

    half_rows = rows // 2
    for part in range(2):
        r = slice(part * half_rows, (part + 1) * half_rows)
        x1 = x[r] + _rms(acc_ref[r, :], gpost_ref[...])
        gate = _sigmoid(_dot(_rms(x1, gple_ref[...]).astype(BF16), wgate_ref[...]))
        res = x1 + gate * ple[r]
        for c in range(xp_ref.shape[0]):
            xp_ref[c, r, :] = res[:, c * LANES:(c + 1) * LANES]
    for i in range(G):
        for c in range(xp_ref.shape[0]):
            out_ref[S * i:S * (i + 1), c * LANES:(c + 1) * LANES] = xp_ref[c, interleaved(i), :]

    @pl.when(t == pl.num_programs(1) - 1)
    def _():
        conv_ref[0] = carry_ref[...]


def _ffn_sample_kernel(x_ref, p_ref, prev0_ref, prev1_ref, gpre_ref, gpost_ref, gple_ref, wup_ref, cw_ref,
                       cb_ref, wdown_ref, wgate_ref, wproj_ref, out_ref, u_ref, acc_ref):
    hidden = wdown_ref.shape[0]
    x = x_ref[...]
    xn = _rms(x, gpre_ref[...]).astype(BF16)

    def conv(cols):
        u = _dot(xn, wup_ref[:, cols])
        u_ref[:, cols] = u
        cw = cw_ref[:, cols]
        return cb_ref[:, cols] + prev0_ref[:, cols] * cw[0:1] + prev1_ref[:, cols] * cw[1:2] + u * cw[2:3]

    for j in range(hidden // FFN_CHUNK):
        gcols, vcols = _ffn_chunk_cols(j, hidden)
        cg = conv(gcols)
        h = cg * _sigmoid(cg) * conv(vcols)
        d = _dot(h.astype(BF16), wdown_ref[j * FFN_CHUNK:(j + 1) * FFN_CHUNK, :])
        if j == 0:
            acc_ref[...] = d
        else:
            acc_ref[...] += d
    out_ref[...] = _ffn_tail(x, acc_ref[...], p_ref[...], gpost_ref, gple_ref, wgate_ref, wproj_ref)


_FFN_KEYS = ("g_pre_ffn", "g_post_ffn", "g_ple", "ffn_wup", "ffn_cw", "ffn_cb", "ffn_wdown", "ple_wgate",
             "ple_wproj")


def _ffn_prompt(h, p_all, w, layer):
    b, t, d = h.shape
    f2 = w["ffn_wup"].shape[-1]
    nt = t // FFN_ROWS
    assert t % FFN_ROWS == 0 and (FFN_ROWS // SUBLANES) % SUBLANES == 0
    out, conv = pl.pallas_call(
        _ffn_prompt_kernel,
        grid=(b, nt),
        in_specs=[pl.BlockSpec((FFN_ROWS, d), lambda i, j: (i * nt + j, 0)),
                  pl.BlockSpec((FFN_ROWS, PLE_DIM), lambda i, j: ((layer * b + i) * nt + j, 0))]
        + [_layer(w[k], layer) for k in _FFN_KEYS],
        out_specs=[pl.BlockSpec((FFN_ROWS, d), lambda i, j: (i * nt + j, 0)),
                   pl.BlockSpec((1, 2 * SUBLANES, f2), lambda i, j: (i, 0, 0))],
        out_shape=[jax.ShapeDtypeStruct((b * t, d), F32),
                   jax.ShapeDtypeStruct((b, 2 * SUBLANES, f2), F32)],
        scratch_shapes=[pltpu.VMEM((2 * SUBLANES, f2), F32),
                        pltpu.VMEM((d // LANES, FFN_ROWS, LANES), F32),
                        pltpu.VMEM((PLE_DIM // LANES, FFN_ROWS, LANES), F32), pltpu.VMEM((FFN_ROWS, d), F32)],
        compiler_params=_params("arbitrary", "arbitrary"),
        name="ffn_prompt",
    )(h.reshape(b * t, d), p_all.reshape(-1, PLE_DIM), *[w[k] for k in _FFN_KEYS])
    return out.reshape(b, t, d), conv[:, SUBLANES - 1::SUBLANES, :]


def _ffn_sample(h, p, prev, w, layer):
    b, d = h.shape
    f2 = w["ffn_wup"].shape[-1]
    args = [h, p, prev[:, 0], prev[:, 1]]
    out, u = pl.pallas_call(
        _ffn_sample_kernel,
        grid=(1,),
        in_specs=[_resident(a.shape) for a in args] + [_layer(w[k], layer) for k in _FFN_KEYS],
        out_specs=[_whole((b, d)), _whole((b, f2))],
        out_shape=[jax.ShapeDtypeStruct((b, d), F32), jax.ShapeDtypeStruct((b, f2), F32)],
        scratch_shapes=[pltpu.VMEM((b, d), F32)],
        compiler_params=_params("arbitrary"),
        name="ffn_sample",
    )(*args, *[w[k] for k in _FFN_KEYS])
    return out, jnp.stack([prev[:, 1], u], axis=1)


_ML_KEYS = ("g_pre_mix", "g_post_mix", "ml_win", "ml_wgc", "ml_bgc", "ml_wgr", "ml_bgr", "ml_wout")


def _mlstm_prompt_kernel(x_ref, gpre_ref, gpost_ref, win_ref, wgc_ref, bgc_ref, wgr_ref, bgr_ref, wout_ref,
                         out_ref, c_out_ref, n_out_ref, m_out_ref, c_ref, n_ref, m_ref):
    t = pl.program_id(1)
    L = ML_CHUNK_ROWS
    nchunk = x_ref.shape[0] // L
    heads = range(ML_HEADS)

    @pl.when(t == 0)
    def _():
        c_ref[...] = jnp.zeros_like(c_ref)
        n_ref[...] = jnp.zeros_like(n_ref)
        m_ref[...] = jnp.zeros_like(m_ref)

    causal = lax.broadcasted_iota(jnp.int32, (L, L), 0) >= lax.broadcasted_iota(jnp.int32, (L, L), 1)

    def project_qk(i):
        r = slice(i * L, (i + 1) * L)
        xn = _rms(x_ref[r, :], gpre_ref[...]).astype(BF16)
        gc = _dot(xn, wgc_ref[...]) + bgc_ref[...]
        gr = _dot_nt(wgr_ref[...], xn) + bgr_ref[:, 0:1]
        proj = _dot(xn, win_ref[:, :2 * ML_HK])
        is_forget_r = lax.broadcasted_iota(jnp.int32, gr.shape, 0) >= ML_HEADS
        bc = _cumsum(_log_sigmoid(gc), 0)
        br = _cumsum(jnp.where(is_forget_r, _log_sigmoid(gr), 0.0), 1)
        q = [proj[:, h * ML_DK:(h + 1) * ML_DK] * (ML_DK ** -0.5) for h in heads]
        k = [proj[:, ML_HK + h * ML_DK:ML_HK + (h + 1) * ML_DK] for h in heads]
        return dict(xn=xn, gc=gc, gr=gr, bc=bc, br=br, q=q, k=k)

    def head_scores(p):
        qb = [a.astype(BF16) for a in p["q"]]
        qk = [_dot_nt(qb[h], p["k"][h].astype(BF16)) for h in heads]
        qc = [_dot(qb[h], c_ref[h].astype(BF16)) for h in heads]
        return qk, qc

    def project_vo(p):
        proj = _dot(p["xn"], win_ref[:, 2 * ML_HK:])
        return [proj[:, h * ML_DV:(h + 1) * ML_DV].astype(BF16) for h in heads], proj[:, ML_HEADS * ML_DV:]

    def gating(p, qk):
        out = []
        for h in heads:
            ig_r = p["gr"][h:h + 1, :]
            b_r = p["br"][ML_HEADS + h:ML_HEADS + h + 1, :]
            ig_c = p["gc"][:, h:h + 1]
            b_c = p["bc"][:, ML_HEADS + h:ML_HEADS + h + 1]
            m_prev = m_ref[h:h + 1, 0:1]
            n_prev = n_ref[h:h + 1, :]

            d = jnp.where(causal, b_c + (ig_r - b_r), -jnp.inf)
            a = b_c + m_prev
            m_t = jnp.maximum(a, jnp.max(d, axis=1, keepdims=True))
            w_inter = jnp.exp(a - m_t)
            s = qk[h] * jnp.exp(d - m_t)
            qn = w_inter * jnp.sum(p["q"][h] * n_prev, axis=1, keepdims=True) + jnp.sum(s, axis=1, keepdims=True)
            inv_den = 1.0 / jnp.maximum(jnp.abs(qn), jnp.exp(-m_t))

            b_last = b_r[:, L - 1:L]
            g_r = b_last - b_r + ig_r
            g_c = b_last - b_c + ig_c
            m_new = jnp.maximum(b_last + m_prev, jnp.max(g_r, axis=1, keepdims=True))
            w_old = jnp.exp(b_last + m_prev - m_new)
            wk = jnp.exp(g_c - m_new) * p["k"][h]
            out.append(dict(s=s.astype(BF16), wk=wk.astype(BF16), w_inter=w_inter, inv_den=inv_den, w_old=w_old,
                            n_new=w_old * n_prev + jnp.sum(wk, axis=0, keepdims=True), m_new=m_new))
        return out

    def finish(i, hidden, o):
        y = _dot((_sigmoid(o) * hidden).astype(BF16), wout_ref[...])
        r = slice(i * L, (i + 1) * L)
        out_ref[r, :] = x_ref[r, :] + _rms(y, gpost_ref[...])

    p = project_qk(0)
    qk, qc = head_scores(p)
    vb, o = project_vo(p)
    for i in range(nchunk):
        p_next = project_qk(i + 1) if i + 1 < nchunk else None
        g = gating(p, qk)
        sv = [_dot(g[h]["s"], vb[h]) for h in heads]
        kv = [_dot_tn(g[h]["wk"], vb[h]) for h in heads]
        hidden = jnp.concatenate([(g[h]["w_inter"] * qc[h] + sv[h]) * g[h]["inv_den"] for h in heads], axis=1)
        for h in heads:
            c_ref[h] = g[h]["w_old"] * c_ref[h] + kv[h]
            n_ref[h:h + 1, :] = g[h]["n_new"]
            m_ref[h:h + 1, :] = jnp.broadcast_to(g[h]["m_new"], (1, LANES))
        o_done = o
        if p_next is not None:
            p = p_next
            qk, qc = head_scores(p)
            vb, o = project_vo(p)
        finish(i, hidden, o_done)

    @pl.when(t == pl.num_programs(1) - 1)
    def _():
        c_out_ref[0] = c_ref[...]
        n_out_ref[0] = n_ref[0:ML_HEADS, :]
        m_out_ref[0] = m_ref[...]


def _ml_layers(layer):
    return [2 * layer, 2 * layer] + [layer] * (len(_ML_KEYS) - 2)


def _mlstm_prompt(h, w, layer):
    b, t, d = h.shape
    L = ML_STEP_ROWS
    nt = t // L
    assert t % L == 0 and L % ML_CHUNK_ROWS == 0
    out, c, n, m = pl.pallas_call(
        _mlstm_prompt_kernel,
        grid=(b, nt),
        in_specs=[pl.BlockSpec((L, d), lambda i, j: (i * nt + j, 0))]
        + [_layer(w[k], l) for k, l in zip(_ML_KEYS, _ml_layers(layer))],
        out_specs=[pl.BlockSpec((L, d), lambda i, j: (i * nt + j, 0)),
                   pl.BlockSpec((1, ML_HEADS, ML_DK, ML_DV), lambda i, j: (i, 0, 0, 0)),
                   pl.BlockSpec((1, ML_HEADS, ML_DK), lambda i, j: (i, 0, 0)),
                   pl.BlockSpec((1, SUBLANES, LANES), lambda i, j: (i, 0, 0))],
        out_shape=[jax.ShapeDtypeStruct((b * t, d), F32),
                   jax.ShapeDtypeStruct((b, ML_HEADS, ML_DK, ML_DV), F32),
                   jax.ShapeDtypeStruct((b, ML_HEADS, ML_DK), F32),
                   jax.ShapeDtypeStruct((b, SUBLANES, LANES), F32)],
        scratch_shapes=[pltpu.VMEM((ML_HEADS, ML_DK, ML_DV), F32), pltpu.VMEM((SUBLANES, ML_DK), F32),
                        pltpu.VMEM((SUBLANES, LANES), F32)],
        compiler_params=_params("arbitrary", "arbitrary"),
        name="mlstm_prompt",
    )(h.reshape(b * t, d), *[w[k] for k in _ML_KEYS])
    return out.reshape(b, t, d), c, n, m[:, :ML_HEADS, 0]


def _mlstm_sample_pre_kernel(x_ref, n_ref, m_ref, gpre_ref, win_ref, wgc_ref, bgc_ref,
                             q_ref, kw_ref, v_ref, o_ref, wold_ref, scal_ref, nnew_ref):
    H = ML_HEADS
    xn = _rms(x_ref[...], gpre_ref[...]).astype(BF16)
    proj = _dot(xn, win_ref[...])
    gates = _dot(xn, wgc_ref[...]) + bgc_ref[...]
    v_ref[...] = proj[:, 2 * ML_HK:2 * ML_HK + H * ML_DV]
    o_ref[...] = proj[:, 2 * ML_HK + H * ML_DV:]
    scal_ref[...] = jnp.zeros_like(scal_ref)
    for h in range(H):
        q = proj[:, h * ML_DK:(h + 1) * ML_DK] * (ML_DK ** -0.5)
        k = proj[:, ML_HK + h * ML_DK:ML_HK + (h + 1) * ML_DK]
        n_prev = n_ref[:, h * ML_DK:(h + 1) * ML_DK]
        ig = gates[:, h:h + 1]
        lf = _log_sigmoid(gates[:, H + h:H + h + 1])
        a = lf + m_ref[:, h:h + 1]
        m_t = jnp.maximum(a, ig)
        w_old = jnp.exp(a - m_t)
        w_new = jnp.exp(ig - m_t)
        s = jnp.sum(q * k, axis=1, keepdims=True) * w_new
        qn = w_old * jnp.sum(q * n_prev, axis=1, keepdims=True) + s
        q_ref[:, h * ML_DK:(h + 1) * ML_DK] = q
        kw_ref[:, h * ML_DK:(h + 1) * ML_DK] = w_new * k
        nnew_ref[:, h * ML_DK:(h + 1) * ML_DK] = w_old * n_prev + w_new * k
        wold_ref[:, h * ML_DV:(h + 1) * ML_DV] = jnp.broadcast_to(w_old, (w_old.shape[0], ML_DV))
        scal_ref[:, h:h + 1] = w_old
        scal_ref[:, H + h:H + h + 1] = s
        scal_ref[:, 2 * H + h:2 * H + h + 1] = jnp.maximum(jnp.abs(qn), jnp.exp(-m_t))
        scal_ref[:, 3 * H + h:3 * H + h + 1] = m_t


def _mlstm_sample_state_kernel(c_ref, qt_ref, kt_ref, v_ref, wold_ref, *rest):
    cnew_ref, qc_ref = rest[-2:]
    for b in range(c_ref.shape[0]):
        for h in range(ML_HEADS):
            c = c_ref[b, h]
            qc = qt_ref[0, h * ML_DK:(h + 1) * ML_DK, b:b + 1]
            kc = kt_ref[0, h * ML_DK:(h + 1) * ML_DK, b:b + 1]
            vr = v_ref[b:b + 1, h * ML_DV:(h + 1) * ML_DV]
            wo = wold_ref[b:b + 1, h * ML_DV:(h + 1) * ML_DV]
            qc_ref[b:b + 1, h * ML_DV:(h + 1) * ML_DV] = jnp.sum(qc * c, axis=0, keepdims=True)
            cnew_ref[b, h] = c * wo + kc * vr


def _mlstm_sample_post_kernel(x_ref, qc_ref, v_ref, o_ref, scal_ref, gpost_ref, wout_ref, out_ref, h_ref):
    H = ML_HEADS
    for h in range(H):
        sl = slice(h * ML_DV, (h + 1) * ML_DV)
        num = scal_ref[:, h:h + 1] * qc_ref[:, sl] + scal_ref[:, H + h:H + h + 1] * v_ref[:, sl]
        h_ref[:, sl] = num / scal_ref[:, 2 * H + h:2 * H + h + 1]
    y = _dot((_sigmoid(o_ref[...]) * h_ref[...]).astype(BF16), wout_ref[...])
    out_ref[...] = x_ref[...] + _rms(y, gpost_ref[...])


def _aliased(stack_prev, args, in_specs):
    if stack_prev is None:
        return {}
    args.append(stack_prev)
    in_specs.append(pl.BlockSpec(memory_space=pl.ANY))
    return {len(args) - 1: 0}


def _mlstm_sample(h, c_all, n, m, w, layer, c_stack):
    b, d = h.shape
    H = ML_HEADS
    bb = SAMPLE_STATE_BATCH
    nb = b // bb
    assert b % bb == 0
    pre_in = [h, n.reshape(b, H * ML_DK), jnp.pad(m, ((0, 0), (0, LANES - H)))]
    pre_keys = ("g_pre_mix", "ml_win", "ml_wgc", "ml_bgc")
    pre_layers = (2 * layer, layer, layer, layer)
    pre_out = [(b, ML_HK), (b, ML_HK), (b, H * ML_DV), (b, H * ML_DV), (b, H * ML_DV), (b, LANES), (b, ML_HK)]
    q, kw, v, o, wold, scal, nnew = pl.pallas_call(
        _mlstm_sample_pre_kernel,
        grid=(1,),
        in_specs=[_resident(a.shape) for a in pre_in] + [_layer(w[k], l) for k, l in zip(pre_keys, pre_layers)],
        out_specs=[_whole(s) for s in pre_out],
        out_shape=[jax.ShapeDtypeStruct(s, F32) for s in pre_out],
        compiler_params=_params("arbitrary"),
        name="mlstm_sample_pre",
    )(*pre_in, *[w[k] for k in pre_keys])

    def cols(a):
        return a.reshape(nb, bb, ML_HK).transpose(0, 2, 1)

    state_args = [c_all, cols(q), cols(kw), v, wold]
    state_specs = [pl.BlockSpec((None, bb, H, ML_DK, ML_DV), lambda i: (layer, i, 0, 0, 0)),
                   pl.BlockSpec((1, ML_HK, bb), lambda i: (i, 0, 0)),
                   pl.BlockSpec((1, ML_HK, bb), lambda i: (i, 0, 0)),
                   pl.BlockSpec((bb, H * ML_DV), lambda i: (i, 0)),
                   pl.BlockSpec((bb, H * ML_DV), lambda i: (i, 0))]
    aliases = _aliased(c_stack, state_args, state_specs)
    c_stack, qc = pl.pallas_call(
        _mlstm_sample_state_kernel,
        grid=(nb,),
        in_specs=state_specs,
        out_specs=[pl.BlockSpec((None, bb, H, ML_DK, ML_DV), lambda i: (layer, i, 0, 0, 0)),
                   pl.BlockSpec((bb, H * ML_DV), lambda i: (i, 0))],
        out_shape=[jax.ShapeDtypeStruct(c_all.shape, F32), jax.ShapeDtypeStruct((b, H * ML_DV), F32)],
        input_output_aliases=aliases,
        compiler_params=_params("arbitrary"),
        name="mlstm_sample_state",
    )(*state_args)

    post_in = [h, qc, v, o, scal]
    out = pl.pallas_call(
        _mlstm_sample_post_kernel,
        grid=(1,),
        in_specs=[_resident(a.shape) for a in post_in]
        + [_layer(w["g_post_mix"], 2 * layer), _layer(w["ml_wout"], layer)],
        out_specs=_whole((b, d)),
        out_shape=jax.ShapeDtypeStruct((b, d), F32),
        scratch_shapes=[pltpu.VMEM((b, H * ML_DV), F32)],
        compiler_params=_params("arbitrary"),
        name="mlstm_sample_post",
    )(*post_in, w["g_post_mix"], w["ml_wout"])
    return out, c_stack, nnew.reshape(b, H, ML_DK), scal[:, 3 * H:4 * H]


def _rope_tables(pos):
    half = SW_HD // 2
    inv = ROPE_THETA ** (-jnp.arange(half, dtype=F32) / half)
    ang = pos[:, None] * inv[None, :]
    cos = jnp.tile(jnp.cos(ang), (1, LANES // half))
    sin = jnp.tile(jnp.concatenate([-jnp.sin(ang), jnp.sin(ang)], axis=1), (1, LANES // SW_HD))
    return cos, sin


def _first_half_lanes(rows):
    lane = lax.broadcasted_iota(jnp.int32, (rows, LANES), 1)
    return jnp.bitwise_and(lane, SW_HD - 1) < SW_HD // 2


def _rope_block(x, cos, sin, first_half):
    partner = jnp.where(first_half, pltpu.roll(x, LANES - SW_HD // 2, axis=1), pltpu.roll(x, SW_HD // 2, axis=1))
    return x * cos + partner * sin


def _pair_heads(a, axis):
    shp = a.shape
    a = a.reshape(shp[:axis] + (2, 2, SW_GROUP) + shp[axis + 1:])
    a = jnp.swapaxes(a, axis + 1, axis + 2)
    return a.reshape(shp)


_SW_KEYS = ("g_pre_mix", "g_post_mix", "sw_win", "sw_wout")


def _swa_prompt_kernel(sink_ref, x_ref, cos_ref, sin_ref, gpre_ref, gpost_ref, win_ref, wout_ref,
                       out_ref, kout_ref, vout_ref, k_ref, v_ref, kf_ref, vf_ref):
    t = pl.program_id(1)
    rows = x_ref.shape[0]
    nsub = rows // WINDOW
    W = WINDOW
    G = SW_GROUP
    K2 = 2 * W
    pairs = range(SW_KV_HEADS // 2)

    @pl.when(t == 0)
    def _():
        k_ref[0:W, :] = jnp.zeros((W, SW_NKV), BF16)
        v_ref[0:W, :] = jnp.zeros((W, SW_NKV), BF16)

    first_half = _first_half_lanes(SW_UNIT * W)
    qi = jnp.bitwise_and(lax.broadcasted_iota(jnp.int32, (G * W, K2), 0), W - 1)
    ki = lax.broadcasted_iota(jnp.int32, (G * W, K2), 1)
    band = (ki >= qi) & (ki <= qi + W)
    lane = lax.broadcasted_iota(jnp.int32, (1, LANES), 1)
    low = lane < SW_HD
    high = jnp.logical_not(low)
    zeros = jnp.zeros((1, LANES), BF16)
    ones_low = jnp.where(low, 1.0, 0.0).astype(BF16)
    ones_high = jnp.where(high, 1.0, 0.0).astype(BF16)
    member = lax.shift_right_logical(lax.broadcasted_iota(jnp.int32, (G * W, 1), 0),
                                     jnp.int32(W.bit_length() - 1))

    def sink_column(pair, half):
        col = jnp.full((G * W, 1), sink_ref[2 * (pair * G) + half], F32)
        for g in range(1, G):
            col = jnp.where(member == g, sink_ref[2 * (pair * G + g) + half], col)
        return col

    sinks = [[sink_column(pair, half) for half in range(2)] for pair in pairs]

    U = SW_UNIT

    def project(u):
        r = slice(u * U * W, (u + 1) * U * W)
        xn = _rms(x_ref[r, :], gpre_ref[...]).astype(BF16)
        proj = _dot(xn, win_ref[:, :SW_NQ + SW_NKV])
        v = _dot(xn, win_ref[:, SW_NQ + SW_NKV:])
        cos = cos_ref[r, :]
        sin = sin_ref[r, :]
        q = [(_rope_block(proj[:, c * LANES:(c + 1) * LANES], cos, sin, first_half)
              * (SW_HD ** -0.5 * LOG2E)).astype(BF16) for c in range(SW_NQ // LANES)]
        for c in range(SW_NKV // LANES):
            blk = _rope_block(proj[:, SW_NQ + c * LANES:SW_NQ + (c + 1) * LANES], cos, sin, first_half)
            kf_ref[r, c * LANES:(c + 1) * LANES] = blk
            k_ref[W + u * U * W:W + (u + 1) * U * W, c * LANES:(c + 1) * LANES] = blk.astype(BF16)
        vf_ref[r, :] = v
        v_ref[W + u * U * W:W + (u + 1) * U * W, :] = v.astype(BF16)
        return [[jnp.concatenate([a[j * W:(j + 1) * W] for a in q[pair * G:(pair + 1) * G]], axis=0)
                 for pair in pairs] for j in range(U)]

    def scores(i, q4):
        out = []
        for pair in pairs:
            k2 = k_ref[i * W:(i + 2) * W, pair * LANES:(pair + 1) * LANES]
            kz = jnp.concatenate([jnp.where(low, k2, zeros), jnp.where(high, k2, zeros)], axis=0)
            out.append(_dot_nt(q4[pair], kz))
        return out

    def softmax_numerators(i, s):
        first_key = jnp.where(t * nsub + i > 0, 0, W)
        mask = band & (ki >= first_key)
        out = []
        for pair in pairs:
            probs, sink_terms = [], []
            for half in range(2):
                sh = jnp.where(mask, s[pair][:, half * K2:(half + 1) * K2], -jnp.inf)
                sk = sinks[pair][half]
                mx = jnp.maximum(jnp.max(sh, axis=1, keepdims=True), sk)
                probs.append(jnp.exp2(sh - mx).astype(BF16))
                sink_terms.append(jnp.exp2(sk - mx))
            out.append((jnp.concatenate(probs, axis=1), jnp.where(low, sink_terms[0], sink_terms[1])))
        return out

    def weighted_values(i, soft):
        out = []
        for pair in pairs:
            v2 = v_ref[i * W:(i + 2) * W, pair * LANES:(pair + 1) * LANES]
            vz = jnp.concatenate(
                [jnp.concatenate([jnp.where(low, v2, zeros), jnp.broadcast_to(ones_low, v2.shape)], axis=1),
                 jnp.concatenate([jnp.where(high, v2, zeros), jnp.broadcast_to(ones_high, v2.shape)], axis=1)],
                axis=0)
            out.append((_dot(soft[pair][0], vz), soft[pair][1]))
        return out

    def finish(u, weighted):
        rows_u = []
        for j in range(U):
            blocks = []
            for o, sink_term in weighted[j]:
                both = (o[:, :LANES] / (o[:, LANES:] + sink_term)).astype(BF16)
                blocks += [both[g * W:(g + 1) * W] for g in range(G)]
            rows_u.append(jnp.concatenate(blocks, axis=1))
        y = _dot(jnp.concatenate(rows_u, axis=0), wout_ref[...])
        r = slice(u * U * W, (u + 1) * U * W)
        out_ref[r, :] = x_ref[r, :] + _rms(y, gpost_ref[...])

    nunit = nsub // U
    q_unit = project(0)
    s = scores(0, q_unit[0])
    weighted, done = [], None
    for i in range(nsub):
        u, j = divmod(i, U)
        q_next = project(u + 1) if j == U - 1 and u + 1 < nunit else None
        soft = softmax_numerators(i, s)
        if j == 0 and done is not None:
            finish(u - 1, done)
            done = None
        weighted.append(weighted_values(i, soft))
        if j == U - 1:
            done, weighted = weighted, []
            if q_next is not None:
                q_unit = q_next
        if i + 1 < nsub:
            s = scores(i + 1, q_unit[(i + 1) % U])
    finish(nunit - 1, done)

    k_ref[0:W, :] = k_ref[rows:rows + W, :]
    v_ref[0:W, :] = v_ref[rows:rows + W, :]

    @pl.when(t == pl.num_programs(1) - 1)
    def _():
        kout_ref[0] = kf_ref[rows - W:, :].T
        vout_ref[0] = vf_ref[rows - W:, :].T


def _swa_prompt(h, w, layer):
    b, t, d = h.shape
    rows = SW_ROWS
    nt = t // rows
    assert t % rows == 0 and rows % WINDOW == 0 and t >= WINDOW
    cos, sin = _rope_tables(jnp.arange(t, dtype=F32))
    layers = (2 * layer + 1, 2 * layer + 1, layer, layer)
    out, k, v = pl.pallas_call(
        _swa_prompt_kernel,
        grid_spec=pltpu.PrefetchScalarGridSpec(
            num_scalar_prefetch=1,
            grid=(b, nt),
            in_specs=[pl.BlockSpec((rows, d), lambda i, j, s: (i * nt + j, 0)),
                      pl.BlockSpec((rows, LANES), lambda i, j, s: (j, 0)),
                      pl.BlockSpec((rows, LANES), lambda i, j, s: (j, 0))]
            + [_layer(w[k_], l) for k_, l in zip(_SW_KEYS, layers)],
            out_specs=[pl.BlockSpec((rows, d), lambda i, j, s: (i * nt + j, 0)),
                       pl.BlockSpec((1, SW_NKV, WINDOW), lambda i, j, s: (i, 0, 0)),
                       pl.BlockSpec((1, SW_NKV, WINDOW), lambda i, j, s: (i, 0, 0))],
            scratch_shapes=[pltpu.VMEM((WINDOW + rows, SW_NKV), BF16), pltpu.VMEM((WINDOW + rows, SW_NKV), BF16),
                            pltpu.VMEM((rows, SW_NKV), F32), pltpu.VMEM((rows, SW_NKV), F32)]),
        out_shape=[jax.ShapeDtypeStruct((b * t, d), F32),
                   jax.ShapeDtypeStruct((b, SW_NKV, WINDOW), F32),
                   jax.ShapeDtypeStruct((b, SW_NKV, WINDOW), F32)],
        compiler_params=_params("arbitrary", "arbitrary"),
        name="swa_prompt",
    )(w["sw_sinks_log2"][layer], h.reshape(b * t, d), cos, sin, *[w[k_] for k_ in _SW_KEYS])
    return out.reshape(b, t, d), k, v


def _swa_sample_pre_kernel(x_ref, cos_ref, sin_ref, gpre_ref, win_ref, q_ref, k_ref, v_ref, kt_ref, vt_ref):
    rows = x_ref.shape[0]
    xn = _rms(x_ref[...], gpre_ref[...]).astype(BF16)
    proj = _dot(xn, win_ref[...])
    cos = cos_ref[...]
    sin = sin_ref[...]
    first_half = _first_half_lanes(rows)
    for c in range(SW_NQ // LANES):
        blk = _rope_block(proj[:, c * LANES:(c + 1) * LANES], cos, sin, first_half)
        q_ref[:, c * LANES:(c + 1) * LANES] = blk * (SW_HD ** -0.5)
    for c in range(SW_NKV // LANES):
        k_ref[:, c * LANES:(c + 1) * LANES] = _rope_block(
            proj[:, SW_NQ + c * LANES:SW_NQ + (c + 1) * LANES], cos, sin, first_half)
    v = proj[:, SW_NQ + SW_NKV:]
    v_ref[...] = v
    kt_ref[...] = k_ref[...].T
    vt_ref[...] = v.T


def _swa_sample_attn_kernel(q_ref, kc_ref, vc_ref, knew_ref, vnew_ref, kcol_ref, vcol_ref, sink_ref, *rest):
    o_ref, kout_ref, vout_ref = rest[-3:]
    nbuf = kc_ref.shape[-1]
    pos = lax.broadcasted_iota(jnp.int32, (SW_HEADS, 1), 0)
    grp = 2 * lax.shift_right_logical(pos, jnp.int32(3)) + jnp.bitwise_and(pos, 1)
    sk = sink_ref[:, 0:1]
    lane = lax.broadcasted_iota(jnp.int32, (1, nbuf), 1)
    dist = nbuf - lane
    valid = (dist >= 0) & (dist <= WINDOW)
    newest = lane == nbuf - 1
    for b in range(q_ref.shape[0]):
        q = q_ref[b]
        qb = q.astype(BF16)
        s = jnp.zeros((SW_HEADS, nbuf), F32)
        s_new = jnp.zeros((SW_HEADS, 1), F32)
        for kv in range(SW_KV_HEADS):
            sl = slice(kv * SW_HD, (kv + 1) * SW_HD)
            kt = kc_ref[b, kv]
            s = jnp.where(grp == kv, _dot(qb, kt.astype(BF16)), s)
            s_new = jnp.where(grp == kv, jnp.sum(q * knew_ref[b, :, sl], axis=1, keepdims=True), s_new)
            kout_ref[b, kv] = jnp.where(newest, kcol_ref[0, sl, b:b + 1], pltpu.roll(kt, nbuf - 1, axis=1))
        s = jnp.where(valid, s, -jnp.inf)
        mx = jnp.maximum(jnp.maximum(jnp.max(s, axis=1, keepdims=True), s_new), sk)
        p = jnp.exp(s - mx)
        p_new = jnp.exp(s_new - mx)
        den = jnp.sum(p, axis=1, keepdims=True) + p_new + jnp.exp(sk - mx)
        pb = p.astype(BF16)
        o = jnp.zeros((SW_HEADS, SW_HD), F32)
        for kv in range(SW_KV_HEADS):
            sl = slice(kv * SW_HD, (kv + 1) * SW_HD)
            vt = vc_ref[b, kv]
            o = jnp.where(grp == kv, _dot_nt(pb, vt.astype(BF16)) + p_new * vnew_ref[b, :, sl], o)
            vout_ref[b, kv] = jnp.where(newest, vcol_ref[0, sl, b:b + 1], pltpu.roll(vt, nbuf - 1, axis=1))
        o_ref[b] = o / den


def _out_proj_kernel(x_ref, a_ref, gpost_ref, wout_ref, out_ref):
    y = _dot(a_ref[...].astype(BF16), wout_ref[...])
    out_ref[...] = x_ref[...] + _rms(y, gpost_ref[...])


def _swa_sample(h, kc_all, vc_all, w, layer, k_stack, v_stack):
    b, d = h.shape
    nbuf = kc_all.shape[-1]
    bb = SAMPLE_STATE_BATCH
    nb = b // bb
    assert b % bb == 0 and nbuf == LANES
    cos, sin = _rope_tables(jnp.full((1,), PAST_LEN, F32))
    pre_in = [h, cos, sin]
    pre_out = [(b, SW_NQ), (b, SW_NKV), (b, SW_NKV), (SW_NKV, b), (SW_NKV, b)]
    q, knew, vnew, kt, vt = pl.pallas_call(
        _swa_sample_pre_kernel,
        grid=(1,),
        in_specs=[_resident(a.shape) for a in pre_in]
        + [_layer(w["g_pre_mix"], 2 * layer + 1), _layer(w["sw_win"], layer)],
        out_specs=[_whole(s) for s in pre_out],
        out_shape=[jax.ShapeDtypeStruct(s, F32) for s in pre_out],
        compiler_params=_params("arbitrary"),
        name="swa_sample_pre",
    )(*pre_in, w["g_pre_mix"], w["sw_win"])

    def cols(a):
        return a.reshape(SW_NKV, nb, bb).transpose(1, 0, 2)

    cache_spec = pl.BlockSpec((None, bb, SW_KV_HEADS, SW_HD, nbuf), lambda i: (layer, i, 0, 0, 0))
    args = [q.reshape(b, SW_HEADS, SW_HD), kc_all, vc_all, knew.reshape(b, 1, SW_NKV), vnew.reshape(b, 1, SW_NKV),
            cols(kt), cols(vt), jnp.broadcast_to(w["sw_sinks"][layer][:, None], (SW_HEADS, LANES))]
    specs = [pl.BlockSpec((bb, SW_HEADS, SW_HD), lambda i: (i, 0, 0)), cache_spec, cache_spec,
             pl.BlockSpec((bb, 1, SW_NKV), lambda i: (i, 0, 0)),
             pl.BlockSpec((bb, 1, SW_NKV), lambda i: (i, 0, 0)),
             pl.BlockSpec((1, SW_NKV, bb), lambda i: (i, 0, 0)),
             pl.BlockSpec((1, SW_NKV, bb), lambda i: (i, 0, 0)),
             _resident((SW_HEADS, LANES))]
    aliases = {}
    if k_stack is not None:
        args += [k_stack, v_stack]
        specs += [pl.BlockSpec(memory_space=pl.ANY), pl.BlockSpec(memory_space=pl.ANY)]
        aliases = {len(args) - 2: 1, len(args) - 1: 2}
    o, k_stack, v_stack = pl.pallas_call(
        _swa_sample_attn_kernel,
        grid=(nb,),
        in_specs=specs,
        out_specs=[pl.BlockSpec((bb, SW_HEADS, SW_HD), lambda i: (i, 0, 0)), cache_spec, cache_spec],
        out_shape=[jax.ShapeDtypeStruct((b, SW_HEADS, SW_HD), F32),
                   jax.ShapeDtypeStruct(kc_all.shape, F32), jax.ShapeDtypeStruct(vc_all.shape, F32)],
        input_output_aliases=aliases,
        compiler_params=_params("arbitrary"),
        name="swa_sample_attn",
    )(*args)

    post_in = [h, o.reshape(b, SW_NQ)]
    out = pl.pallas_call(
        _out_proj_kernel,
        grid=(1,),
        in_specs=[_resident(a.shape) for a in post_in]
        + [_layer(w["g_post_mix"], 2 * layer + 1), _layer(w["sw_wout"], layer)],
        out_specs=_whole((b, d)),
        out_shape=jax.ShapeDtypeStruct((b, d), F32),
        compiler_params=_params("arbitrary"),
        name="swa_sample_post",
    )(*post_in, w["g_post_mix"], w["sw_wout"])
    return out, k_stack, v_stack


def _prepare(g_pre_mix, g_post_mix, g_pre_ffn, g_post_ffn, g_ple, ml_w_in, ml_b_gate, ml_w_out, sw_w_in,
             sw_sinks, sw_w_out, ffn_w_up, ffn_conv_w, ffn_conv_b, ffn_w_down, ple_w_proj, ple_w_gate):
    gate_w = ml_w_in[:, :, ML_QKVO:]
    q_cols = _pair_heads(sw_w_in[:, :, :SW_NQ].reshape(sw_w_in.shape[:2] + (SW_HEADS, SW_HD)), 2)
    sw_win = jnp.concatenate([q_cols.reshape(sw_w_in.shape[:2] + (SW_NQ,)), sw_w_in[:, :, SW_NQ:]], axis=-1)
    sw_wout = _pair_heads(sw_w_out.reshape((sw_w_out.shape[0], SW_HEADS, SW_HD, sw_w_out.shape[-1])), 1)
    sinks = _pair_heads(sw_sinks, 1)
    return dict(
        g_pre_mix=g_pre_mix[:, None], g_post_mix=g_post_mix[:, None], g_pre_ffn=g_pre_ffn[:, None],
        g_post_ffn=g_post_ffn[:, None], g_ple=g_ple[:, None],
        ffn_wup=ffn_w_up.astype(BF16), ffn_cw=ffn_conv_w, ffn_cb=ffn_conv_b[:, None],
        ffn_wdown=ffn_w_down.astype(BF16), ple_wgate=ple_w_gate.astype(BF16), ple_wproj=ple_w_proj.astype(BF16),
        ml_win=ml_w_in[:, :, :ML_QKVO].astype(BF16),
        ml_wgc=jnp.pad(gate_w, ((0, 0), (0, 0), (0, LANES - 2 * ML_HEADS))).astype(BF16),
        ml_bgc=jnp.pad(ml_b_gate, ((0, 0), (0, LANES - 2 * ML_HEADS)))[:, None],
        ml_wgr=jnp.swapaxes(gate_w, 1, 2).astype(BF16),
        ml_bgr=jnp.broadcast_to(ml_b_gate[:, :, None], ml_b_gate.shape + (LANES,)),
        ml_wout=ml_w_out.astype(BF16),
        sw_win=sw_win.astype(BF16), sw_wout=sw_wout.reshape(sw_w_out.shape).astype(BF16),
        sw_sinks=sinks, sw_sinks_log2=sinks * LOG2E)


def kernel(x_prompt, x_sample, p_prompt, p_sample, state_mlstm_C, state_mlstm_n, state_mlstm_m, cache_swa_k, cache_swa_v, state_conv, g_pre_mix, g_post_mix, g_pre_ffn, g_post_ffn, g_ple, ml_w_in, ml_b_gate, ml_w_out, sw_w_in, sw_sinks, sw_w_out, ffn_w_up, ffn_conv_w, ffn_conv_b, ffn_w_down, ple_w_proj, ple_w_gate):
    depth = g_pre_mix.shape[0]
    w = _prepare(g_pre_mix, g_post_mix, g_pre_ffn, g_post_ffn, g_ple, ml_w_in, ml_b_gate, ml_w_out, sw_w_in,
                 sw_sinks, sw_w_out, ffn_w_up, ffn_conv_w, ffn_conv_b, ffn_w_down, ple_w_proj, ple_w_gate)
    kc_all = jnp.transpose(cache_swa_k, (0, 1, 3, 4, 2))
    vc_all = jnp.transpose(cache_swa_v, (0, 1, 3, 4, 2))
    hp = x_prompt
    hs = x_sample[:, 0, :]
    cp, np_, mp, ns, ms = [], [], [], [], []
    kp, vp = [], []
    convp, convs = [], []
    c_stack = k_stack = v_stack = None
    for i in range(depth):
        j = i // 2
        if i % 2 == 0:
            hp, c1, n1, m1 = _mlstm_prompt(hp, w, j)
            hs, c_stack, n2, m2 = _mlstm_sample(hs, state_mlstm_C, state_mlstm_n[j], state_mlstm_m[j], w, j, c_stack)
            cp.append(c1); np_.append(n1); mp.append(m1)
            ns.append(n2); ms.append(m2)
        else:
            hp, k1, v1 = _swa_prompt(hp, w, j)
            hs, k_stack, v_stack = _swa_sample(hs, kc_all, vc_all, w, j, k_stack, v_stack)
            kp.append(k1); vp.append(v1)
        hp, cvp = _ffn_prompt(hp, p_prompt, w, i)
        hs, cvs = _ffn_sample(hs, p_sample[i][:, 0, :], state_conv[i], w, i)
        convp.append(cvp); convs.append(cvs)

    def cache_layout(a):
        return jnp.transpose(a, (0, 1, 4, 2, 3))

    kv_shape = (len(kp), x_prompt.shape[0], SW_KV_HEADS, SW_HD, WINDOW)
    return (hp, hs[:, None, :],
            jnp.stack(cp), jnp.stack(np_), jnp.stack(mp),
            cache_layout(jnp.stack(kp).reshape(kv_shape)), cache_layout(jnp.stack(vp).reshape(kv_shape)),
            jnp.stack(convp),
            c_stack, jnp.stack(ns), jnp.stack(ms), cache_layout(k_stack), cache_layout(v_stack), jnp.stack(convs))
```

```python
import math

import jax
import jax.numpy as jnp
from jax import lax
from jax.experimental import pallas as pl
from jax.experimental.pallas import tpu as pltpu

F32 = jnp.float32
BF16 = jnp.bfloat16

EPS = 1e-6
PLE_DIM = 256
ML_HEADS = 4
ML_DK = 128
ML_DV = 256
ML_HK = ML_HEADS * ML_DK
ML_QKVO = 2 * ML_HK + 2 * ML_HEADS * ML_DV
SW_HEADS = 16
SW_KV_HEADS = 4
SW_HD = 64
SW_GROUP = SW_HEADS // SW_KV_HEADS
SW_NQ = SW_HEADS * SW_HD
SW_NKV = SW_KV_HEADS * SW_HD
WINDOW = 128
ROPE_THETA = 10000.0
PAST_LEN = 8192
LOG2E = math.log2(math.e)

LANES = 128
SUBLANES = 8
FFN_CHUNK = 256
FFN_DOWN_GROUP = 4
FFN_ROWS = 512
FFN_TILES = 2
ML_CHUNK_ROWS = 256
ML_STEP_ROWS = 1024
SW_ROWS = 1024
SW_UNIT = 2
SAMPLE_STATE_BATCH = 8
VMEM_LIMIT = 60 * 1024 * 1024

assert SW_KV_HEADS == 4 and SW_GROUP == 4 and 2 * SW_HD == LANES and WINDOW == LANES


def _params(*sem):
    return pltpu.CompilerParams(dimension_semantics=sem, vmem_limit_bytes=VMEM_LIMIT)


def _resident(shape):
    zeros = (0,) * len(shape)
    return pl.BlockSpec(shape, lambda *_: zeros, pipeline_mode=pl.Buffered(1))


def _layer(a, layer):
    zeros = (0,) * (a.ndim - 1)
    return pl.BlockSpec((None,) + a.shape[1:], lambda *_: (layer,) + zeros, pipeline_mode=pl.Buffered(1))


def _whole(shape):
    zeros = (0,) * len(shape)
    return pl.BlockSpec(shape, lambda *_: zeros)


def _rms(x, g):
    return x * lax.rsqrt(jnp.mean(x * x, axis=-1, keepdims=True) + EPS) * g


def _sigmoid(x):
    return 1.0 / (1.0 + jnp.exp(-x))


def _log_sigmoid(x):
    return jnp.minimum(x, 0.0) - jnp.log1p(jnp.exp(-jnp.abs(x)))


def _dot(a, b):
    return jnp.dot(a, b, preferred_element_type=F32)


def _dot_nt(a, b):
    return lax.dot_general(a, b, (((1,), (1,)), ((), ())), preferred_element_type=F32)


def _dot_tn(a, b):
    return lax.dot_general(a, b, (((0,), (0,)), ((), ())), preferred_element_type=F32)


def _cumsum(x, axis):
    n = x.shape[axis]
    idx = lax.broadcasted_iota(jnp.int32, x.shape, axis)
    s = 1
    while s < n:
        x = x + jnp.where(idx >= s, pltpu.roll(x, s, axis=axis), 0.0)
        s *= 2
    return x


def _ffn_tail(x, f, p, gpost_ref, gple_ref, wgate_ref, wproj_ref):
    x1 = x + _rms(f, gpost_ref[...])
    gate = _sigmoid(_dot(_rms(x1, gple_ref[...]).astype(BF16), wgate_ref[...]))
    return x1 + gate * _dot(p.astype(BF16), wproj_ref[...])


def _ffn_chunk_cols(j, hidden):
    return (slice(j * FFN_CHUNK, (j + 1) * FFN_CHUNK),
            slice(hidden + j * FFN_CHUNK, hidden + (j + 1) * FFN_CHUNK))


def _ffn_prompt_kernel(x_ref, p_ref, gpre_ref, gpost_ref, gple_ref, wup_ref, cw_ref, cb_ref, wdown_ref,
                       wgate_ref, wproj_ref, out_ref, conv_ref, carry_ref, xp_ref, pp_ref, acc_ref):
    t = pl.program_id(1)
    rows = FFN_ROWS
    ntile = x_ref.shape[0] // rows
    hidden = wdown_ref.shape[0]
    nch = hidden // FFN_CHUNK
    S = SUBLANES
    G = rows // S
    nx = xp_ref.shape[1]
    npl = pp_ref.shape[1]

    @pl.when(t == 0)
    def _():
        carry_ref[...] = jnp.zeros_like(carry_ref)

    def interleaved(i):
        return pl.ds(((S * i) % G) * S + (S * i) // G, S, stride=S)

    first = lax.broadcasted_iota(jnp.int32, (S, FFN_CHUNK), 0) == 0

    def load_tile(n):
        for i in range(G):
            src = slice(n * rows + S * i, n * rows + S * (i + 1))
            for c in range(nx):
                xp_ref[n, c, interleaved(i), :] = x_ref[src, c * LANES:(c + 1) * LANES]
            for c in range(npl):
                pp_ref[n, c, interleaved(i), :] = p_ref[src, c * LANES:(c + 1) * LANES]
        x = jnp.concatenate([xp_ref[n, c] for c in range(nx)], axis=1)
        p = jnp.concatenate([pp_ref[n, c] for c in range(npl)], axis=1)
        ple = _dot(p.astype(BF16), wproj_ref[...])
        return x, ple, _rms(x, gpre_ref[...]).astype(BF16)

    def conv(u, cols):
        prev = carry_ref[:, cols]
        back1 = jnp.where(first, prev[2 * S - 1:2 * S], pltpu.roll(u[rows - S:], 1, axis=0))
        back2 = jnp.where(first, prev[S - 1:S], pltpu.roll(u[rows - 2 * S:rows - S], 1, axis=0))
        u1 = jnp.concatenate([back1, u[:rows - S]], axis=0)
        u2 = jnp.concatenate([back2, back1, u[:rows - 2 * S]], axis=0)
        carry_ref[:, cols] = u[rows - 2 * S:]
        cw = cw_ref[:, cols]
        return cb_ref[:, cols] + u2 * cw[0:1] + u1 * cw[1:2] + u * cw[2:3]

    def up(xn, j):
        return [_dot(xn, wup_ref[:, cols]) for cols in _ffn_chunk_cols(j, hidden)]

    def gate_of(n, x):
        out = []
        for part in range(2):
            r = slice(part * (rows // 2), (part + 1) * (rows // 2))
            x1 = x[r] + _rms(acc_ref[n, r, :], gpost_ref[...])
            out.append((x1, _dot(_rms(x1, gple_ref[...]).astype(BF16), wgate_ref[...])))
        return out

    def store_tile(n, gated, ple):
        for part, (x1, gate) in enumerate(gated):
            r = slice(part * (rows // 2), (part + 1) * (rows // 2))
            res = x1 + _sigmoid(gate) * ple[r]
            for c in range(nx):
                xp_ref[n, c, r, :] = res[:, c * LANES:(c + 1) * LANES]
        for i in range(G):
            dst = slice(n * rows + S * i, n * rows + S * (i + 1))
            for c in range(nx):
                out_ref[dst, c * LANES:(c + 1) * LANES] = xp_ref[n, c, interleaved(i), :]

    tiles = [load_tile(n) for n in range(ntile)]
    tail = None
    gated = None
    for n in range(ntile):
        x, ple, xn = tiles[n]
        u = up(xn, 0)
        pending = []
        for j in range(nch):
            u_next = up(xn, j + 1) if j + 1 < nch else None
            if tail is not None and j == 2:
                gated = gate_of(tail[0], tail[1])
            if tail is not None and j == 5:
                store_tile(tail[0], gated, tail[2])
                tail = None
            gcols, vcols = _ffn_chunk_cols(j, hidden)
            cg = conv(u[0], gcols)
            pending.append((cg * _sigmoid(cg) * conv(u[1], vcols)).astype(BF16))
            if len(pending) == FFN_DOWN_GROUP or j == nch - 1:
                lo = (j + 1 - len(pending)) * FFN_CHUNK
                d = _dot(jnp.concatenate(pending, axis=1), wdown_ref[lo:(j + 1) * FFN_CHUNK, :])
                if lo == 0:
                    acc_ref[n] = d
                else:
                    acc_ref[n] += d
                pending = []
            u = u_next
        tail = (n, x, ple)
    store_tile(tail[0], gate_of(tail[0], tail[1]), tail[2])

    @pl.when(t == pl.num_programs(1) - 1)
    def _():
        conv_ref[0] = carry_ref[...]


def _ffn_sample_kernel(x_ref, p_ref, prev0_ref, prev1_ref, gpre_ref, gpost_ref, gple_ref, wup_ref, cw_ref,
                       cb_ref, wdown_ref, wgate_ref, wproj_ref, out_ref, u_ref, acc_ref):
    hidden = wdown_ref.shape[0]
    x = x_ref[...]
    xn = _rms(x, gpre_ref[...]).astype(BF16)

    def conv(cols):
        u = _dot(xn, wup_ref[:, cols])
        u_ref[:, cols] = u
        cw = cw_ref[:, cols]
        return cb_ref[:, cols] + prev0_ref[:, cols] * cw[0:1] + prev1_ref[:, cols] * cw[1:2] + u * cw[2:3]

    for j in range(hidden // FFN_CHUNK):
        gcols, vcols = _ffn_chunk_cols(j, hidden)
        cg = conv(gcols)
        h = cg * _sigmoid(cg) * conv(vcols)
        d = _dot(h.astype(BF16), wdown_ref[j * FFN_CHUNK:(j + 1) * FFN_CHUNK, :])
        if j == 0:
            acc_ref[...] = d
        else:
            acc_ref[...] += d
    out_ref[...] = _ffn_tail(x, acc_ref[...], p_ref[...], gpost_ref, gple_ref, wgate_ref, wproj_ref)


_FFN_KEYS = ("g_pre_ffn", "g_post_ffn", "g_ple", "ffn_wup", "ffn_cw", "ffn_cb", "ffn_wdown", "ple_wgate",
             "ple_wproj")


def _ffn_prompt(h, p_all, w, layer):
    b, t, d = h.shape
    f2 = w["ffn_wup"].shape[-1]
    step = FFN_ROWS * FFN_TILES
    nt = t // step
    assert t % step == 0 and (FFN_ROWS // SUBLANES) % SUBLANES == 0
    out, conv = pl.pallas_call(
        _ffn_prompt_kernel,
        grid=(b, nt),
        in_specs=[pl.BlockSpec((step, d), lambda i, j: (i * nt + j, 0)),
                  pl.BlockSpec((step, PLE_DIM), lambda i, j: ((layer * b + i) * nt + j, 0))]
        + [_layer(w[k], layer) for k in _FFN_KEYS],
        out_specs=[pl.BlockSpec((step, d), lambda i, j: (i * nt + j, 0)),
                   pl.BlockSpec((1, 2 * SUBLANES, f2), lambda i, j: (i, 0, 0))],
        out_shape=[jax.ShapeDtypeStruct((b * t, d), F32),
                   jax.ShapeDtypeStruct((b, 2 * SUBLANES, f2), F32)],
        scratch_shapes=[pltpu.VMEM((2 * SUBLANES, f2), F32),
                        pltpu.VMEM((FFN_TILES, d // LANES, FFN_ROWS, LANES), F32),
                        pltpu.VMEM((FFN_TILES, PLE_DIM // LANES, FFN_ROWS, LANES), F32),
                        pltpu.VMEM((FFN_TILES, FFN_ROWS, d), F32)],
        compiler_params=_params("arbitrary", "arbitrary"),
        name="ffn_prompt",
    )(h.reshape(b * t, d), p_all.reshape(-1, PLE_DIM), *[w[k] for k in _FFN_KEYS])
    return out.reshape(b, t, d), conv[:, SUBLANES - 1::SUBLANES, :]


def _ffn_sample(h, p, prev, w, layer):
    b, d = h.shape
    f2 = w["ffn_wup"].shape[-1]
    args = [h, p, prev[:, 0], prev[:, 1]]
    out, u = pl.pallas_call(
        _ffn_sample_kernel,
        grid=(1,),
        in_specs=[_resident(a.shape) for a in args] + [_layer(w[k], layer) for k in _FFN_KEYS],
        out_specs=[_whole((b, d)), _whole((b, f2))],
        out_shape=[jax.ShapeDtypeStruct((b, d), F32), jax.ShapeDtypeStruct((b, f2), F32)],
        scratch_shapes=[pltpu.VMEM((b, d), F32)],
        compiler_params=_params("arbitrary"),
        name="ffn_sample",
    )(*args, *[w[k] for k in _FFN_KEYS])
    return out, jnp.stack([prev[:, 1], u], axis=1)


_ML_KEYS = ("g_pre_mix", "g_post_mix", "ml_win", "ml_wgc", "ml_bgc", "ml_wgr", "ml_bgr", "ml_wout")


def _mlstm_prompt_kernel(x_ref, gpre_ref, gpost_ref, win_ref, wgc_ref, bgc_ref, wgr_ref, bgr_ref, wout_ref,
                         out_ref, c_out_ref, n_out_ref, m_out_ref, c_ref, n_ref, m_ref):
    t = pl.program_id(1)
    L = ML_CHUNK_ROWS
    nchunk = x_ref.shape[0] // L
    heads = range(ML_HEADS)

    @pl.when(t == 0)
    def _():
        c_ref[...] = jnp.zeros_like(c_ref)
        n_ref[...] = jnp.zeros_like(n_ref)
        m_ref[...] = jnp.zeros_like(m_ref)

    causal = lax.broadcasted_iota(jnp.int32, (L, L), 0) >= lax.broadcasted_iota(jnp.int32, (L, L), 1)

    def project_qk(i):
        r = slice(i * L, (i + 1) * L)
        xn = _rms(x_ref[r, :], gpre_ref[...]).astype(BF16)
        gc = _dot(xn, wgc_ref[...]) + bgc_ref[...]
        gr = _dot_nt(wgr_ref[...], xn) + bgr_ref[:, 0:1]
        proj = _dot(xn, win_ref[:, :2 * ML_HK])
        is_forget_r = lax.broadcasted_iota(jnp.int32, gr.shape, 0) >= ML_HEADS
        bc = _cumsum(_log_sigmoid(gc), 0)
        br = _cumsum(jnp.where(is_forget_r, _log_sigmoid(gr), 0.0), 1)
        q = [proj[:, h * ML_DK:(h + 1) * ML_DK] * (ML_DK ** -0.5) for h in heads]
        k = [proj[:, ML_HK + h * ML_DK:ML_HK + (h + 1) * ML_DK] for h in heads]
        return dict(xn=xn, gc=gc, gr=gr, bc=bc, br=br, q=q, k=k)

    def head_scores(p):
        qb = [a.astype(BF16) for a in p["q"]]
        qk = [_dot_nt(qb[h], p["k"][h].astype(BF16)) for h in heads]
        qc = [_dot(qb[h], c_ref[h].astype(BF16)) for h in heads]
        return qk, qc

    def project_vo(p):
        proj = _dot(p["xn"], win_ref[:, 2 * ML_HK:])
        return [proj[:, h * ML_DV:(h + 1) * ML_DV].astype(BF16) for h in heads], proj[:, ML_HEADS * ML_DV:]

    def gating(p, qk):
        out = []
        for h in heads:
            ig_r = p["gr"][h:h + 1, :]
            b_r = p["br"][ML_HEADS + h:ML_HEADS + h + 1, :]
            ig_c = p["gc"][:, h:h + 1]
            b_c = p["bc"][:, ML_HEADS + h:ML_HEADS + h + 1]
            m_prev = m_ref[h:h + 1, 0:1]
            n_prev = n_ref[h:h + 1, :]

            d = jnp.where(causal, b_c + (ig_r - b_r), -jnp.inf)
            a = b_c + m_prev
            m_t = jnp.maximum(a, jnp.max(d, axis=1, keepdims=True))
            w_inter = jnp.exp(a - m_t)
            s = qk[h] * jnp.exp(d - m_t)
            qn = w_inter * jnp.sum(p["q"][h] * n_prev, axis=1, keepdims=True) + jnp.sum(s, axis=1, keepdims=True)
            inv_den = 1.0 / jnp.maximum(jnp.abs(qn), jnp.exp(-m_t))

            b_last = b_r[:, L - 1:L]
            g_r = b_last - b_r + ig_r
            g_c = b_last - b_c + ig_c
            m_new = jnp.maximum(b_last + m_prev, jnp.max(g_r, axis=1, keepdims=True))
            w_old = jnp.exp(b_last + m_prev - m_new)
            wk = jnp.exp(g_c - m_new) * p["k"][h]
            out.append(dict(s=s.astype(BF16), wk=wk.astype(BF16), w_inter=w_inter, inv_den=inv_den, w_old=w_old,
                            n_new=w_old * n_prev + jnp.sum(wk, axis=0, keepdims=True), m_new=m_new))
        return out

    def finish(i, hidden, o):
        y = _dot((_sigmoid(o) * hidden).astype(BF16), wout_ref[...])
        r = slice(i * L, (i + 1) * L)
        out_ref[r, :] = x_ref[r, :] + _rms(y, gpost_ref[...])

    p = project_qk(0)
    qk, qc = head_scores(p)
    vb, o = project_vo(p)
    for i in range(nchunk):
        p_next = project_qk(i + 1) if i + 1 < nchunk else None
        g = gating(p, qk)
        sv = [_dot(g[h]["s"], vb[h]) for h in heads]
        kv = [_dot_tn(g[h]["wk"], vb[h]) for h in heads]
        hidden = jnp.concatenate([(g[h]["w_inter"] * qc[h] + sv[h]) * g[h]["inv_den"] for h in heads], axis=1)
        for h in heads:
            c_ref[h] = g[h]["w_old"] * c_ref[h] + kv[h]
            n_ref[h:h + 1, :] = g[h]["n_new"]
            m_ref[h:h + 1, :] = jnp.broadcast_to(g[h]["m_new"], (1, LANES))
        o_done = o
        if p_next is not None:
            p = p_next
            qk, qc = head_scores(p)
            vb, o = project_vo(p)
        finish(i, hidden, o_done)

    @pl.when(t == pl.num_programs(1) - 1)
    def _():
        c_out_ref[0] = c_ref[...]
        n_out_ref[0] = n_ref[0:ML_HEADS, :]
        m_out_ref[0] = m_ref[...]


def _ml_layers(layer):
    return [2 * layer, 2 * layer] + [layer] * (len(_ML_KEYS) - 2)


def _mlstm_prompt(h, w, layer):
    b, t, d = h.shape
    L = ML_STEP_ROWS
    nt = t // L
    assert t % L == 0 and L % ML_CHUNK_ROWS == 0
    out, c, n, m = pl.pallas_call(
        _mlstm_prompt_kernel,
        grid=(b, nt),
        in_specs=[pl.BlockSpec((L, d), lambda i, j: (i * nt + j, 0))]
        + [_layer(w[k], l) for k, l in zip(_ML_KEYS, _ml_layers(layer))],
        out_specs=[pl.BlockSpec((L, d), lambda i, j: (i * nt + j, 0)),
                   pl.BlockSpec((1, ML_HEADS, ML_DK, ML_DV), lambda i, j: (i, 0, 0, 0)),
                   pl.BlockSpec((1, ML_HEADS, ML_DK), lambda i, j: (i, 0, 0)),
                   pl.BlockSpec((1, SUBLANES, LANES), lambda i, j: (i, 0, 0))],
        out_shape=[jax.ShapeDtypeStruct((b * t, d), F32),
                   jax.ShapeDtypeStruct((b, ML_HEADS, ML_DK, ML_DV), F32),
                   jax.ShapeDtypeStruct((b, ML_HEADS, ML_DK), F32),
                   jax.ShapeDtypeStruct((b, SUBLANES, LANES), F32)],
        scratch_shapes=[pltpu.VMEM((ML_HEADS, ML_DK, ML_DV), F32), pltpu.VMEM((SUBLANES, ML_DK), F32),
                        pltpu.VMEM((SUBLANES, LANES), F32)],
        compiler_params=_params("arbitrary", "arbitrary"),
        name="mlstm_prompt",
    )(h.reshape(b * t, d), *[w[k] for k in _ML_KEYS])
    return out.reshape(b, t, d), c, n, m[:, :ML_HEADS, 0]


def _mlstm_sample_pre_kernel(x_ref, n_ref, m_ref, gpre_ref, win_ref, wgc_ref, bgc_ref,
                             q_ref, kw_ref, v_ref, o_ref, wold_ref, scal_ref, nnew_ref):
    H = ML_HEADS
    xn = _rms(x_ref[...], gpre_ref[...]).astype(BF16)
    proj = _dot(xn, win_ref[...])
    gates = _dot(xn, wgc_ref[...]) + bgc_ref[...]
    v_ref[...] = proj[:, 2 * ML_HK:2 * ML_HK + H * ML_DV]
    o_ref[...] = proj[:, 2 * ML_HK + H * ML_DV:]
    scal_ref[...] = jnp.zeros_like(scal_ref)
    for h in range(H):
        q = proj[:, h * ML_DK:(h + 1) * ML_DK] * (ML_DK ** -0.5)
        k = proj[:, ML_HK + h * ML_DK:ML_HK + (h + 1) * ML_DK]
        n_prev = n_ref[:, h * ML_DK:(h + 1) * ML_DK]
        ig = gates[:, h:h + 1]
        lf = _log_sigmoid(gates[:, H + h:H + h + 1])
        a = lf + m_ref[:, h:h + 1]
        m_t = jnp.maximum(a, ig)
        w_old = jnp.exp(a - m_t)
        w_new = jnp.exp(ig - m_t)
        s = jnp.sum(q * k, axis=1, keepdims=True) * w_new
        qn = w_old * jnp.sum(q * n_prev, axis=1, keepdims=True) + s
        q_ref[:, h * ML_DK:(h + 1) * ML_DK] = q
        kw_ref[:, h * ML_DK:(h + 1) * ML_DK] = w_new * k
        nnew_ref[:, h * ML_DK:(h + 1) * ML_DK] = w_old * n_prev + w_new * k
        wold_ref[:, h * ML_DV:(h + 1) * ML_DV] = jnp.broadcast_to(w_old, (w_old.shape[0], ML_DV))
        scal_ref[:, h:h + 1] = w_old
        scal_ref[:, H + h:H + h + 1] = s
        scal_ref[:, 2 * H + h:2 * H + h + 1] = jnp.maximum(jnp.abs(qn), jnp.exp(-m_t))
        scal_ref[:, 3 * H + h:3 * H + h + 1] = m_t


def _mlstm_sample_state_kernel(c_ref, qt_ref, kt_ref, v_ref, wold_ref, *rest):
    cnew_ref, qc_ref = rest[-2:]
    for b in range(c_ref.shape[0]):
        for h in range(ML_HEADS):
            c = c_ref[b, h]
            qc = qt_ref[0, h * ML_DK:(h + 1) * ML_DK, b:b + 1]
            kc = kt_ref[0, h * ML_DK:(h + 1) * ML_DK, b:b + 1]
            vr = v_ref[b:b + 1, h * ML_DV:(h + 1) * ML_DV]
            wo = wold_ref[b:b + 1, h * ML_DV:(h + 1) * ML_DV]
            qc_ref[b:b + 1, h * ML_DV:(h + 1) * ML_DV] = jnp.sum(qc * c, axis=0, keepdims=True)
            cnew_ref[b, h] = c * wo + kc * vr


def _mlstm_sample_post_kernel(x_ref, qc_ref, v_ref, o_ref, scal_ref, gpost_ref, wout_ref, out_ref, h_ref):
    H = ML_HEADS
    for h in range(H):
        sl = slice(h * ML_DV, (h + 1) * ML_DV)
        num = scal_ref[:, h:h + 1] * qc_ref[:, sl] + scal_ref[:, H + h:H + h + 1] * v_ref[:, sl]
        h_ref[:, sl] = num / scal_ref[:, 2 * H + h:2 * H + h + 1]
    y = _dot((_sigmoid(o_ref[...]) * h_ref[...]).astype(BF16), wout_ref[...])
    out_ref[...] = x_ref[...] + _rms(y, gpost_ref[...])


def _aliased(stack_prev, args, in_specs):
    if stack_prev is None:
        return {}
    args.append(stack_prev)
    in_specs.append(pl.BlockSpec(memory_space=pl.ANY))
    return {len(args) - 1: 0}


def _mlstm_sample(h, c_all, n, m, w, layer, c_stack):
    b, d = h.shape
    H = ML_HEADS
    bb = SAMPLE_STATE_BATCH
    nb = b // bb
    assert b % bb == 0
    pre_in = [h, n.reshape(b, H * ML_DK), jnp.pad(m, ((0, 0), (0, LANES - H)))]
    pre_keys = ("g_pre_mix", "ml_win", "ml_wgc", "ml_bgc")
    pre_layers = (2 * layer, layer, layer, layer)
    pre_out = [(b, ML_HK), (b, ML_HK), (b, H * ML_DV), (b, H * ML_DV), (b, H * ML_DV), (b, LANES), (b, ML_HK)]
    q, kw, v, o, wold, scal, nnew = pl.pallas_call(
        _mlstm_sample_pre_kernel,
        grid=(1,),
        in_specs=[_resident(a.shape) for a in pre_in] + [_layer(w[k], l) for k, l in zip(pre_keys, pre_layers)],
        out_specs=[_whole(s) for s in pre_out],
        out_shape=[jax.ShapeDtypeStruct(s, F32) for s in pre_out],
        compiler_params=_params("arbitrary"),
        name="mlstm_sample_pre",
    )(*pre_in, *[w[k] for k in pre_keys])

    def cols(a):
        return a.reshape(nb, bb, ML_HK).transpose(0, 2, 1)

    state_args = [c_all, cols(q), cols(kw), v, wold]
    state_specs = [pl.BlockSpec((None, bb, H, ML_DK, ML_DV), lambda i: (layer, i, 0, 0, 0)),
                   pl.BlockSpec((1, ML_HK, bb), lambda i: (i, 0, 0)),
                   pl.BlockSpec((1, ML_HK, bb), lambda i: (i, 0, 0)),
                   pl.BlockSpec((bb, H * ML_DV), lambda i: (i, 0)),
                   pl.BlockSpec((bb, H * ML_DV), lambda i: (i, 0))]
    aliases = _aliased(c_stack, state_args, state_specs)
    c_stack, qc = pl.pallas_call(
        _mlstm_sample_state_kernel,
        grid=(nb,),
        in_specs=state_specs,
        out_specs=[pl.BlockSpec((None, bb, H, ML_DK, ML_DV), lambda i: (layer, i, 0, 0, 0)),
                   pl.BlockSpec((bb, H * ML_DV), lambda i: (i, 0))],
        out_shape=[jax.ShapeDtypeStruct(c_all.shape, F32), jax.ShapeDtypeStruct((b, H * ML_DV), F32)],
        input_output_aliases=aliases,
        compiler_params=_params("arbitrary"),
        name="mlstm_sample_state",
    )(*state_args)

    post_in = [h, qc, v, o, scal]
    out = pl.pallas_call(
        _mlstm_sample_post_kernel,
        grid=(1,),
        in_specs=[_resident(a.shape) for a in post_in]
        + [_layer(w["g_post_mix"], 2 * layer), _layer(w["ml_wout"], layer)],
        out_specs=_whole((b, d)),
        out_shape=jax.ShapeDtypeStruct((b, d), F32),
        scratch_shapes=[pltpu.VMEM((b, H * ML_DV), F32)],
        compiler_params=_params("arbitrary"),
        name="mlstm_sample_post",
    )(*post_in, w["g_post_mix"], w["ml_wout"])
    return out, c_stack, nnew.reshape(b, H, ML_DK), scal[:, 3 * H:4 * H]


def _rope_tables(pos):
    half = SW_HD // 2
    inv = ROPE_THETA ** (-jnp.arange(half, dtype=F32) / half)
    ang = pos[:, None] * inv[None, :]
    cos = jnp.tile(jnp.cos(ang), (1, LANES // half))
    sin = jnp.tile(jnp.concatenate([-jnp.sin(ang), jnp.sin(ang)], axis=1), (1, LANES // SW_HD))
    return cos, sin


def _first_half_lanes(rows):
    lane = lax.broadcasted_iota(jnp.int32, (rows, LANES), 1)
    return jnp.bitwise_and(lane, SW_HD - 1) < SW_HD // 2


def _rope_block(x, cos, sin, first_half):
    partner = jnp.where(first_half, pltpu.roll(x, LANES - SW_HD // 2, axis=1), pltpu.roll(x, SW_HD // 2, axis=1))
    return x * cos + partner * sin


def _pair_heads(a, axis):
    shp = a.shape
    a = a.reshape(shp[:axis] + (2, 2, SW_GROUP) + shp[axis + 1:])
    a = jnp.swapaxes(a, axis + 1, axis + 2)
    return a.reshape(shp)


_SW_KEYS = ("g_pre_mix", "g_post_mix", "sw_win", "sw_wout")


def _swa_prompt_kernel(sink_ref, x_ref, cos_ref, sin_ref, gpre_ref, gpost_ref, win_ref, wout_ref,
                       out_ref, kout_ref, vout_ref, k_ref, v_ref, kf_ref, vf_ref):
    t = pl.program_id(1)
    rows = x_ref.shape[0]
    nsub = rows // WINDOW
    W = WINDOW
    G = SW_GROUP
    K2 = 2 * W
    pairs = range(SW_KV_HEADS // 2)

    @pl.when(t == 0)
    def _():
        k_ref[0:W, :] = jnp.zeros((W, SW_NKV), BF16)
        v_ref[0:W, :] = jnp.zeros((W, SW_NKV), BF16)

    first_half = _first_half_lanes(SW_UNIT * W)
    qi = jnp.bitwise_and(lax.broadcasted_iota(jnp.int32, (G * W, K2), 0), W - 1)
    ki = lax.broadcasted_iota(jnp.int32, (G * W, K2), 1)
    band = (ki >= qi) & (ki <= qi + W)
    lane = lax.broadcasted_iota(jnp.int32, (1, LANES), 1)
    low = lane < SW_HD
    high = jnp.logical_not(low)
    zeros = jnp.zeros((1, LANES), BF16)
    ones_low = jnp.where(low, 1.0, 0.0).astype(BF16)
    ones_high = jnp.where(high, 1.0, 0.0).astype(BF16)
    member = lax.shift_right_logical(lax.broadcasted_iota(jnp.int32, (G * W, 1), 0),
                                     jnp.int32(W.bit_length() - 1))

    def sink_column(pair, half):
        col = jnp.full((G * W, 1), sink_ref[2 * (pair * G) + half], F32)
        for g in range(1, G):
            col = jnp.where(member == g, sink_ref[2 * (pair * G + g) + half], col)
        return col

    sinks = [[sink_column(pair, half) for half in range(2)] for pair in pairs]

    U = SW_UNIT

    def project(u):
        r = slice(u * U * W, (u + 1) * U * W)
        xn = _rms(x_ref[r, :], gpre_ref[...]).astype(BF16)
        proj = _dot(xn, win_ref[:, :SW_NQ + SW_NKV])
        v = _dot(xn, win_ref[:, SW_NQ + SW_NKV:])
        cos = cos_ref[r, :]
        sin = sin_ref[r, :]
        q = [(_rope_block(proj[:, c * LANES:(c + 1) * LANES], cos, sin, first_half)
              * (SW_HD ** -0.5 * LOG2E)).astype(BF16) for c in range(SW_NQ // LANES)]
        for c in range(SW_NKV // LANES):
            blk = _rope_block(proj[:, SW_NQ + c * LANES:SW_NQ + (c + 1) * LANES], cos, sin, first_half)
            kf_ref[r, c * LANES:(c + 1) * LANES] = blk
            k_ref[W + u * U * W:W + (u + 1) * U * W, c * LANES:(c + 1) * LANES] = blk.astype(BF16)
        vf_ref[r, :] = v
        v_ref[W + u * U * W:W + (u + 1) * U * W, :] = v.astype(BF16)
        return [[jnp.concatenate([a[j * W:(j + 1) * W] for a in q[pair * G:(pair + 1) * G]], axis=0)
                 for pair in pairs] for j in range(U)]

    def scores(i, q4):
        out = []
        for pair in pairs:
            k2 = k_ref[i * W:(i + 2) * W, pair * LANES:(pair + 1) * LANES]
            kz = jnp.concatenate([jnp.where(low, k2, zeros), jnp.where(high, k2, zeros)], axis=0)
            out.append(_dot_nt(q4[pair], kz))
        return out

    def softmax_numerators(i, s):
        first_key = jnp.where(t * nsub + i > 0, 0, W)
        mask = band & (ki >= first_key)
        out = []
        for pair in pairs:
            probs, sink_terms = [], []
            for half in range(2):
                sh = jnp.where(mask, s[pair][:, half * K2:(half + 1) * K2], -jnp.inf)
                sk = sinks[pair][half]
                mx = jnp.maximum(jnp.max(sh, axis=1, keepdims=True), sk)
                probs.append(jnp.exp2(sh - mx).astype(BF16))
                sink_terms.append(jnp.exp2(sk - mx))
            out.append((jnp.concatenate(probs, axis=1), jnp.where(low, sink_terms[0], sink_terms[1])))
        return out

    def weighted_values(i, soft):
        out = []
        for pair in pairs:
            v2 = v_ref[i * W:(i + 2) * W, pair * LANES:(pair + 1) * LANES]
            vz = jnp.concatenate(
                [jnp.concatenate([jnp.where(low, v2, zeros), jnp.broadcast_to(ones_low, v2.shape)], axis=1),
                 jnp.concatenate([jnp.where(high, v2, zeros), jnp.broadcast_to(ones_high, v2.shape)], axis=1)],
                axis=0)
            out.append((_dot(soft[pair][0], vz), soft[pair][1]))
        return out

    def finish(u, weighted):
        rows_u = []
        for j in range(U):
            blocks = []
            for o, sink_term in weighted[j]:
                both = (o[:, :LANES] / (o[:, LANES:] + sink_term)).astype(BF16)
                blocks += [both[g * W:(g + 1) * W] for g in range(G)]
            rows_u.append(jnp.concatenate(blocks, axis=1))
        y = _dot(jnp.concatenate(rows_u, axis=0), wout_ref[...])
        r = slice(u * U * W, (u + 1) * U * W)
        out_ref[r, :] = x_ref[r, :] + _rms(y, gpost_ref[...])

    nunit = nsub // U
    q_unit = project(0)
    s = scores(0, q_unit[0])
    weighted, done = [], None
    for i in range(nsub):
        u, j = divmod(i, U)
        q_next = project(u + 1) if j == U - 1 and u + 1 < nunit else None
        soft = softmax_numerators(i, s)
        if j == 0 and done is not None:
            finish(u - 1, done)
            done = None
        weighted.append(weighted_values(i, soft))
        if j == U - 1:
            done, weighted = weighted, []
            if q_next is not None:
                q_unit = q_next
        if i + 1 < nsub:
            s = scores(i + 1, q_unit[(i + 1) % U])
    finish(nunit - 1, done)

    k_ref[0:W, :] = k_ref[rows:rows + W, :]
    v_ref[0:W, :] = v_ref[rows:rows + W, :]

    @pl.when(t == pl.num_programs(1) - 1)
    def _():
        kout_ref[0] = kf_ref[rows - W:, :].T
        vout_ref[0] = vf_ref[rows - W:, :].T


def _swa_prompt(h, w, layer):
    b, t, d = h.shape
    rows = SW_ROWS
    nt = t // rows
    assert t % rows == 0 and rows % WINDOW == 0 and t >= WINDOW
    cos, sin = _rope_tables(jnp.arange(t, dtype=F32))
    layers = (2 * layer + 1, 2 * layer + 1, layer, layer)
    out, k, v = pl.pallas_call(
        _swa_prompt_kernel,
        grid_spec=pltpu.PrefetchScalarGridSpec(
            num_scalar_prefetch=1,
            grid=(b, nt),
            in_specs=[pl.BlockSpec((rows, d), lambda i, j, s: (i * nt + j, 0)),
                      pl.BlockSpec((rows, LANES), lambda i, j, s: (j, 0)),
                      pl.BlockSpec((rows, LANES), lambda i, j, s: (j, 0))]
            + [_layer(w[k_], l) for k_, l in zip(_SW_KEYS, layers)],
            out_specs=[pl.BlockSpec((rows, d), lambda i, j, s: (i * nt + j, 0)),
                       pl.BlockSpec((1, SW_NKV, WINDOW), lambda i, j, s: (i, 0, 0)),
                       pl.BlockSpec((1, SW_NKV, WINDOW), lambda i, j, s: (i, 0, 0))],
            scratch_shapes=[pltpu.VMEM((WINDOW + rows, SW_NKV), BF16), pltpu.VMEM((WINDOW + rows, SW_NKV), BF16),
                            pltpu.VMEM((rows, SW_NKV), F32), pltpu.VMEM((rows, SW_NKV), F32)]),
        out_shape=[jax.ShapeDtypeStruct((b * t, d), F32),
                   jax.ShapeDtypeStruct((b, SW_NKV, WINDOW), F32),
                   jax.ShapeDtypeStruct((b, SW_NKV, WINDOW), F32)],
        compiler_params=_params("arbitrary", "arbitrary"),
        name="swa_prompt",
    )(w["sw_sinks_log2"][layer], h.reshape(b * t, d), cos, sin, *[w[k_] for k_ in _SW_KEYS])
    return out.reshape(b, t, d), k, v


def _swa_sample_pre_kernel(x_ref, cos_ref, sin_ref, gpre_ref, win_ref, q_ref, k_ref, v_ref, kt_ref, vt_ref):
    rows = x_ref.shape[0]
    xn = _rms(x_ref[...], gpre_ref[...]).astype(BF16)
    proj = _dot(xn, win_ref[...])
    cos = cos_ref[...]
    sin = sin_ref[...]
    first_half = _first_half_lanes(rows)
    for c in range(SW_NQ // LANES):
        blk = _rope_block(proj[:, c * LANES:(c + 1) * LANES], cos, sin, first_half)
        q_ref[:, c * LANES:(c + 1) * LANES] = blk * (SW_HD ** -0.5)
    for c in range(SW_NKV // LANES):
        k_ref[:, c * LANES:(c + 1) * LANES] = _rope_block(
            proj[:, SW_NQ + c * LANES:SW_NQ + (c + 1) * LANES], cos, sin, first_half)
    v = proj[:, SW_NQ + SW_NKV:]
    v_ref[...] = v
    kt_ref[...] = k_ref[...].T
    vt_ref[...] = v.T


def _swa_sample_attn_kernel(q_ref, kc_ref, vc_ref, knew_ref, vnew_ref, kcol_ref, vcol_ref, sink_ref, *rest):
    o_ref, kout_ref, vout_ref = rest[-3:]
    nbuf = kc_ref.shape[-1]
    pos = lax.broadcasted_iota(jnp.int32, (SW_HEADS, SW_NKV), 0)
    grp = 2 * lax.shift_right_logical(pos, jnp.int32(3)) + jnp.bitwise_and(pos, 1)
    col_kv = lax.shift_right_logical(lax.broadcasted_iota(jnp.int32, (SW_HEADS, SW_NKV), 1),
                                     jnp.int32(SW_HD.bit_length() - 1))
    own = grp == col_kv
    sk = sink_ref[:, 0:1]
    lane = lax.broadcasted_iota(jnp.int32, (1, nbuf), 1)
    dist = nbuf - lane
    valid = (dist >= 0) & (dist <= WINDOW)
    newest = lane == nbuf - 1
    for b in range(q_ref.shape[0]):
        q = q_ref[b]
        kt = kc_ref[b].reshape(SW_NKV, nbuf)
        vt = vc_ref[b].reshape(SW_NKV, nbuf)
        s = jnp.where(valid, _dot(q.astype(BF16), kt.astype(BF16)), -jnp.inf)
        s_new = jnp.sum(q * knew_ref[b], axis=1, keepdims=True)
        mx = jnp.maximum(jnp.maximum(jnp.max(s, axis=1, keepdims=True), s_new), sk)
        p = jnp.exp(s - mx)
        p_new = jnp.exp(s_new - mx)
        den = jnp.sum(p, axis=1, keepdims=True) + p_new + jnp.exp(sk - mx)
        o = _dot_nt(p.astype(BF16), vt.astype(BF16)) + p_new * vnew_ref[b]
        o = jnp.where(own, o, 0.0)
        o = o[:, :LANES] + o[:, LANES:]
        o = o + pltpu.roll(o, SW_HD, axis=1)
        o_ref[b] = (o / den)[:, :SW_HD]
        kout_ref[b] = jnp.where(newest, kcol_ref[0, :, b:b + 1],
                                pltpu.roll(kt, nbuf - 1, axis=1)).reshape(SW_KV_HEADS, SW_HD, nbuf)
        vout_ref[b] = jnp.where(newest, vcol_ref[0, :, b:b + 1],
                                pltpu.roll(vt, nbuf - 1, axis=1)).reshape(SW_KV_HEADS, SW_HD, nbuf)


def _out_proj_kernel(x_ref, a_ref, gpost_ref, wout_ref, out_ref):
    y = _dot(a_ref[...].astype(BF16), wout_ref[...])
    out_ref[...] = x_ref[...] + _rms(y, gpost_ref[...])


def _swa_sample(h, kc_all, vc_all, w, layer, k_stack, v_stack):
    b, d = h.shape
    nbuf = kc_all.shape[-1]
    bb = SAMPLE_STATE_BATCH
    nb = b // bb
    assert b % bb == 0 and nbuf == LANES
    cos, sin = _rope_tables(jnp.full((1,), PAST_LEN, F32))
    pre_in = [h, cos, sin]
    pre_out = [(b, SW_NQ), (b, SW_NKV), (b, SW_NKV), (SW_NKV, b), (SW_NKV, b)]
    q, knew, vnew, kt, vt = pl.pallas_call(
        _swa_sample_pre_kernel,
        grid=(1,),
        in_specs=[_resident(a.shape) for a in pre_in]
        + [_layer(w["g_pre_mix"], 2 * layer + 1), _layer(w["sw_win"], layer)],
        out_specs=[_whole(s) for s in pre_out],
        out_shape=[jax.ShapeDtypeStruct(s, F32) for s in pre_out],
        compiler_params=_params("arbitrary"),
        name="swa_sample_pre",
    )(*pre_in, w["g_pre_mix"], w["sw_win"])

    def cols(a):
        return a.reshape(SW_NKV, nb, bb).transpose(1, 0, 2)

    cache_spec = pl.BlockSpec((None, bb, SW_KV_HEADS, SW_HD, nbuf), lambda i: (layer, i, 0, 0, 0))
    pos = jnp.arange(SW_HEADS)
    own_kv = (2 * (pos // 8) + pos % 2)[:, None] == jnp.arange(SW_KV_HEADS)[None, :]
    q_bd = jnp.where(own_kv[None, :, :, None], q.reshape(b, SW_HEADS, 1, SW_HD), 0.0).reshape(b, SW_HEADS, SW_NKV)
    args = [q_bd, kc_all, vc_all, knew.reshape(b, 1, SW_NKV), vnew.reshape(b, 1, SW_NKV),
            cols(kt), cols(vt), jnp.broadcast_to(w["sw_sinks"][layer][:, None], (SW_HEADS, LANES))]
    specs = [pl.BlockSpec((bb, SW_HEADS, SW_NKV), lambda i: (i, 0, 0)), cache_spec, cache_spec,
             pl.BlockSpec((bb, 1, SW_NKV), lambda i: (i, 0, 0)),
             pl.BlockSpec((bb, 1, SW_NKV), lambda i: (i, 0, 0)),
             pl.BlockSpec((1, SW_NKV, bb), lambda i: (i, 0, 0)),
             pl.BlockSpec((1, SW_NKV, bb), lambda i: (i, 0, 0)),
             _resident((SW_HEADS, LANES))]
    aliases = {}
    if k_stack is not None:
        args += [k_stack, v_stack]
        specs += [pl.BlockSpec(memory_space=pl.ANY), pl.BlockSpec(memory_space=pl.ANY)]
        aliases = {len(args) - 2: 1, len(args) - 1: 2}
    o, k_stack, v_stack = pl.pallas_call(
        _swa_sample_attn_kernel,
        grid=(nb,),
        in_specs=specs,
        out_specs=[pl.BlockSpec((bb, SW_HEADS, SW_HD), lambda i: (i, 0, 0)), cache_spec, cache_spec],
        out_shape=[jax.ShapeDtypeStruct((b, SW_HEADS, SW_HD), F32),
                   jax.ShapeDtypeStruct(kc_all.shape, F32), jax.ShapeDtypeStruct(vc_all.shape, F32)],
        input_output_aliases=aliases,
        compiler_params=_params("arbitrary"),
        name="swa_sample_attn",
    )(*args)

    post_in = [h, o.reshape(b, SW_NQ)]
    out = pl.pallas_call(
        _out_proj_kernel,
        grid=(1,),
        in_specs=[_resident(a.shape) for a in post_in]
        + [_layer(w["g_post_mix"], 2 * layer + 1), _layer(w["sw_wout"], layer)],
        out_specs=_whole((b, d)),
        out_shape=jax.ShapeDtypeStruct((b, d), F32),
        compiler_params=_params("arbitrary"),
        name="swa_sample_post",
    )(*post_in, w["g_post_mix"], w["sw_wout"])
    return out, k_stack, v_stack


def _prepare(g_pre_mix, g_post_mix, g_pre_ffn, g_post_ffn, g_ple, ml_w_in, ml_b_gate, ml_w_out, sw_w_in,
             sw_sinks, sw_w_out, ffn_w_up, ffn_conv_w, ffn_conv_b, ffn_w_down, ple_w_proj, ple_w_gate):
    gate_w = ml_w_in[:, :, ML_QKVO:]
    q_cols = _pair_heads(sw_w_in[:, :, :SW_NQ].reshape(sw_w_in.shape[:2] + (SW_HEADS, SW_HD)), 2)
    sw_win = jnp.concatenate([q_cols.reshape(sw_w_in.shape[:2] + (SW_NQ,)), sw_w_in[:, :, SW_NQ:]], axis=-1)
    sw_wout = _pair_heads(sw_w_out.reshape((sw_w_out.shape[0], SW_HEADS, SW_HD, sw_w_out.shape[-1])), 1)
    sinks = _pair_heads(sw_sinks, 1)
    return dict(
        g_pre_mix=g_pre_mix[:, None], g_post_mix=g_post_mix[:, None], g_pre_ffn=g_pre_ffn[:, None],
        g_post_ffn=g_post_ffn[:, None], g_ple=g_ple[:, None],
        ffn_wup=ffn_w_up.astype(BF16), ffn_cw=ffn_conv_w, ffn_cb=ffn_conv_b[:, None],
        ffn_wdown=ffn_w_down.astype(BF16), ple_wgate=ple_w_gate.astype(BF16), ple_wproj=ple_w_proj.astype(BF16),
        ml_win=ml_w_in[:, :, :ML_QKVO].astype(BF16),
        ml_wgc=jnp.pad(gate_w, ((0, 0), (0, 0), (0, LANES - 2 * ML_HEADS))).astype(BF16),
        ml_bgc=jnp.pad(ml_b_gate, ((0, 0), (0, LANES - 2 * ML_HEADS)))[:, None],
        ml_wgr=jnp.swapaxes(gate_w, 1, 2).astype(BF16),
        ml_bgr=jnp.broadcast_to(ml_b_gate[:, :, None], ml_b_gate.shape + (LANES,)),
        ml_wout=ml_w_out.astype(BF16),
        sw_win=sw_win.astype(BF16), sw_wout=sw_wout.reshape(sw_w_out.shape).astype(BF16),
        sw_sinks=sinks, sw_sinks_log2=sinks * LOG2E)


def kernel(x_prompt, x_sample, p_prompt, p_sample, state_mlstm_C, state_mlstm_n, state_mlstm_m, cache_swa_k, cache_swa_v, state_conv, g_pre_mix, g_post_mix, g_pre_ffn, g_post_ffn, g_ple, ml_w_in, ml_b_gate, ml_w_out, sw_w_in, sw_sinks, sw_w_out, ffn_w_up, ffn_conv_w, ffn_conv_b, ffn_w_down, ple_w_proj, ple_w_gate):
    depth = g_pre_mix.shape[0]
    w = _prepare(g_pre_mix, g_post_mix, g_pre_ffn, g_post_ffn, g_ple, ml_w_in, ml_b_gate, ml_w_out, sw_w_in,
                 sw_sinks, sw_w_out, ffn_w_up, ffn_conv_w, ffn_conv_b, ffn_w_down, ple_w_proj, ple_w_gate)
    kc_all = jnp.transpose(cache_swa_k, (0, 1, 3, 4, 2))
    vc_all = jnp.transpose(cache_swa_v, (0, 1, 3, 4, 2))
    hp = x_prompt
    hs = x_sample[:, 0, :]
    cp, np_, mp, ns, ms = [], [], [], [], []
    kp, vp = [], []
    convp, convs = [], []
    c_stack = k_stack = v_stack = None
    for i in range(depth):
        j = i // 2
        if i % 2 == 0:
            hp, c1, n1, m1 = _mlstm_prompt(hp, w, j)
            hs, c_stack, n2, m2 = _mlstm_sample(hs, state_mlstm_C, state_mlstm_n[j], state_mlstm_m[j], w, j, c_stack)
            cp.append(c1); np_.append(n1); mp.append(m1)
            ns.append(n2); ms.append(m2)
        else:
            hp, k1, v1 = _swa_prompt(hp, w, j)
            hs, k_stack, v_stack = _swa_sample(hs, kc_all, vc_all, w, j, k_stack, v_stack)
            kp.append(k1); vp.append(v1)
        hp, cvp = _ffn_prompt(hp, p_prompt, w, i)
        hs, cvs = _ffn_sample(hs, p_sample[i][:, 0, :], state_conv[i], w, i)
        convp.append(cvp); convs.append(cvs)

    def cache_layout(a):
        return jnp.transpose(a, (0, 1, 4, 2, 3))

    kv_shape = (len(kp), x_prompt.shape[0], SW_KV_HEADS, SW_HD, WINDOW)
    return (hp, hs[:, None, :],
            jnp.stack(cp), jnp.stack(np_), jnp.stack(mp),
            cache_layout(jnp.stack(kp).reshape(kv_shape)), cache_layout(jnp.stack(vp).reshape(kv_shape)),
            jnp.stack(convp),
            c_stack, jnp.stack(ns), jnp.stack(ms), cache_layout(k_stack), cache_layout(v_stack), jnp.stack(convs))
```

```python
import math

import jax
import jax.numpy as jnp
from jax import lax
from jax.experimental import pallas as pl
from jax.experimental.pallas import tpu as pltpu

F32 = jnp.float32
BF16 = jnp.bfloat16

EPS = 1e-6
PLE_DIM = 256
ML_HEADS = 4
ML_DK = 128
ML_DV = 256
ML_HK = ML_HEADS * ML_DK
ML_QKVO = 2 * ML_HK + 2 * ML_HEADS * ML_DV
SW_HEADS = 16
SW_KV_HEADS = 4
SW_HD = 64
SW_GROUP = SW_HEADS // SW_KV_HEADS
SW_NQ = SW_HEADS * SW_HD
SW_NKV = SW_KV_HEADS * SW_HD
WINDOW = 128
ROPE_THETA = 10000.0
PAST_LEN = 8192
LOG2E = math.log2(math.e)

LANES = 128
SUBLANES = 8
FFN_CHUNK = 256
FFN_DOWN_GROUP = 4
FFN_ROWS = 512
FFN_TILES = 2
ML_CHUNK_ROWS = 256
ML_STEP_ROWS = 1024
SW_ROWS = 1024
SW_UNIT = 2
SAMPLE_STATE_BATCH = 8
VMEM_LIMIT = 60 * 1024 * 1024

assert SW_KV_HEADS == 4 and SW_GROUP == 4 and 2 * SW_HD == LANES and WINDOW == LANES


def _params(*sem):
    return pltpu.CompilerParams(dimension_semantics=sem, vmem_limit_bytes=VMEM_LIMIT)


def _resident(shape):
    zeros = (0,) * len(shape)
    return pl.BlockSpec(shape, lambda *_: zeros, pipeline_mode=pl.Buffered(1))


def _layer(a, layer):
    zeros = (0,) * (a.ndim - 1)
    return pl.BlockSpec((None,) + a.shape[1:], lambda *_: (layer,) + zeros, pipeline_mode=pl.Buffered(1))


def _whole(shape):
    zeros = (0,) * len(shape)
    return pl.BlockSpec(shape, lambda *_: zeros)


def _rms(x, g):
    return x * lax.rsqrt(jnp.mean(x * x, axis=-1, keepdims=True) + EPS) * g


def _sigmoid(x):
    return 1.0 / (1.0 + jnp.exp(-x))


def _log_sigmoid(x):
    return jnp.minimum(x, 0.0) - jnp.log1p(jnp.exp(-jnp.abs(x)))


def _dot(a, b):
    return jnp.dot(a, b, preferred_element_type=F32)


def _dot_nt(a, b):
    return lax.dot_general(a, b, (((1,), (1,)), ((), ())), preferred_element_type=F32)


def _dot_tn(a, b):
    return lax.dot_general(a, b, (((0,), (0,)), ((), ())), preferred_element_type=F32)


def _cumsum(x, axis):
    n = x.shape[axis]
    idx = lax.broadcasted_iota(jnp.int32, x.shape, axis)
    s = 1
    while s < n:
        x = x + jnp.where(idx >= s, pltpu.roll(x, s, axis=axis), 0.0)
        s *= 2
    return x


def _ffn_tail(x, f, p, gpost_ref, gple_ref, wgate_ref, wproj_ref):
    x1 = x + _rms(f, gpost_ref[...])
    gate = _sigmoid(_dot(_rms(x1, gple_ref[...]).astype(BF16), wgate_ref[...]))
    return x1 + gate * _dot(p.astype(BF16), wproj_ref[...])


def _ffn_chunk_cols(j, hidden):
    return (slice(j * FFN_CHUNK, (j + 1) * FFN_CHUNK),
            slice(hidden + j * FFN_CHUNK, hidden + (j + 1) * FFN_CHUNK))


def _ffn_prompt_kernel(x_ref, p_ref, gpre_ref, gpost_ref, gple_ref, wup_ref, cw_ref, cb_ref, wdown_ref,
                       wgate_ref, wproj_ref, out_ref, conv_ref, carry_ref, xp_ref, pp_ref, acc_ref):
    t = pl.program_id(1)
    rows = FFN_ROWS
    ntile = x_ref.shape[0] // rows
    hidden = wdown_ref.shape[0]
    nch = hidden // FFN_CHUNK
    S = SUBLANES
    G = rows // S
    nx = xp_ref.shape[1]
    npl = pp_ref.shape[1]

    @pl.when(t == 0)
    def _():
        carry_ref[...] = jnp.zeros_like(carry_ref)

    def interleaved(i):
        return pl.ds(((S * i) % G) * S + (S * i) // G, S, stride=S)

    first = lax.broadcasted_iota(jnp.int32, (S, FFN_CHUNK), 0) == 0

    def load_tile(n):
        for i in range(G):
            src = slice(n * rows + S * i, n * rows + S * (i + 1))
            for c in range(nx):
                xp_ref[n, c, interleaved(i), :] = x_ref[src, c * LANES:(c + 1) * LANES]
            for c in range(npl):
                pp_ref[n, c, interleaved(i), :] = p_ref[src, c * LANES:(c + 1) * LANES]
        x = jnp.concatenate([xp_ref[n, c] for c in range(nx)], axis=1)
        p = jnp.concatenate([pp_ref[n, c] for c in range(npl)], axis=1)
        ple = _dot(p.astype(BF16), wproj_ref[...])
        return x, ple, _rms(x, gpre_ref[...]).astype(BF16)

    def conv(u, cols):
        prev = carry_ref[:, cols]
        back1 = jnp.where(first, prev[2 * S - 1:2 * S], pltpu.roll(u[rows - S:], 1, axis=0))
        back2 = jnp.where(first, prev[S - 1:S], pltpu.roll(u[rows - 2 * S:rows - S], 1, axis=0))
        u1 = jnp.concatenate([back1, u[:rows - S]], axis=0)
        u2 = jnp.concatenate([back2, back1, u[:rows - 2 * S]], axis=0)
        carry_ref[:, cols] = u[rows - 2 * S:]
        cw = cw_ref[:, cols]
        return cb_ref[:, cols] + u2 * cw[0:1] + u1 * cw[1:2] + u * cw[2:3]

    def up(xn, j):
        return [_dot(xn, wup_ref[:, cols]) for cols in _ffn_chunk_cols(j, hidden)]

    def gate_of(n, x):
        out = []
        for part in range(2):
            r = slice(part * (rows // 2), (part + 1) * (rows // 2))
            x1 = x[r] + _rms(acc_ref[n, r, :], gpost_ref[...])
            out.append((x1, _dot(_rms(x1, gple_ref[...]).astype(BF16), wgate_ref[...])))
        return out

    def store_tile(n, gated, ple):
        for part, (x1, gate) in enumerate(gated):
            r = slice(part * (rows // 2), (part + 1) * (rows // 2))
            res = x1 + _sigmoid(gate) * ple[r]
            for c in range(nx):
                xp_ref[n, c, r, :] = res[:, c * LANES:(c + 1) * LANES]
        for i in range(G):
            dst = slice(n * rows + S * i, n * rows + S * (i + 1))
            for c in range(nx):
                out_ref[dst, c * LANES:(c + 1) * LANES] = xp_ref[n, c, interleaved(i), :]

    tiles = [load_tile(n) for n in range(ntile)]
    tail = None
    gated = None
    for n in range(ntile):
        x, ple, xn = tiles[n]
        u = up(xn, 0)
        pending = []
        for j in range(nch):
            u_next = up(xn, j + 1) if j + 1 < nch else None
            if tail is not None and j == 2:
                gated = gate_of(tail[0], tail[1])
            if tail is not None and j == 5:
                store_tile(tail[0], gated, tail[2])
                tail = None
            gcols, vcols = _ffn_chunk_cols(j, hidden)
            cg = conv(u[0], gcols)
            pending.append((cg * _sigmoid(cg) * conv(u[1], vcols)).astype(BF16))
            if len(pending) == FFN_DOWN_GROUP or j == nch - 1:
                lo = (j + 1 - len(pending)) * FFN_CHUNK
                d = _dot(jnp.concatenate(pending, axis=1), wdown_ref[lo:(j + 1) * FFN_CHUNK, :])
                if lo == 0:
                    acc_ref[n] = d
                else:
                    acc_ref[n] += d
                pending = []
            u = u_next
        tail = (n, x, ple)
    store_tile(tail[0], gate_of(tail[0], tail[1]), tail[2])

    @pl.when(t == pl.num_programs(1) - 1)
    def _():
        conv_ref[0, 0:1, :] = carry_ref[S - 1:S, :]
        conv_ref[0, 1:2, :] = carry_ref[2 * S - 1:2 * S, :]


def _ffn_sample_kernel(x_ref, p_ref, state_ref, gpre_ref, gpost_ref, gple_ref, wup_ref, cw_ref, cb_ref,
                       wdown_ref, wgate_ref, wproj_ref, *rest):
    out_ref, state_out_ref, acc_ref = rest[-3:]
    hidden = wdown_ref.shape[0]
    x = x_ref[...]
    xn = _rms(x, gpre_ref[...]).astype(BF16)

    def conv(cols):
        u = _dot(xn, wup_ref[:, cols])
        blocks = range(cols.start // LANES, cols.stop // LANES)
        prev0 = jnp.concatenate([state_ref[:, 2 * c * LANES:(2 * c + 1) * LANES] for c in blocks], axis=1)
        prev1 = jnp.concatenate([state_ref[:, (2 * c + 1) * LANES:(2 * c + 2) * LANES] for c in blocks], axis=1)
        for n, c in enumerate(blocks):
            state_out_ref[:, 2 * c * LANES:(2 * c + 1) * LANES] = prev1[:, n * LANES:(n + 1) * LANES]
            state_out_ref[:, (2 * c + 1) * LANES:(2 * c + 2) * LANES] = u[:, n * LANES:(n + 1) * LANES]
        cw = cw_ref[:, cols]
        return cb_ref[:, cols] + prev0 * cw[0:1] + prev1 * cw[1:2] + u * cw[2:3]

    for j in range(hidden // FFN_CHUNK):
        gcols, vcols = _ffn_chunk_cols(j, hidden)
        cg = conv(gcols)
        h = cg * _sigmoid(cg) * conv(vcols)
        d = _dot(h.astype(BF16), wdown_ref[j * FFN_CHUNK:(j + 1) * FFN_CHUNK, :])
        if j == 0:
            acc_ref[...] = d
        else:
            acc_ref[...] += d
    out_ref[...] = _ffn_tail(x, acc_ref[...], p_ref[...], gpost_ref, gple_ref, wgate_ref, wproj_ref)


_FFN_KEYS = ("g_pre_ffn", "g_post_ffn", "g_ple", "ffn_wup", "ffn_cw", "ffn_cb", "ffn_wdown", "ple_wgate",
             "ple_wproj")


def _ffn_prompt(h, p_all, w, layer):
    b, t, d = h.shape
    f2 = w["ffn_wup"].shape[-1]
    step = FFN_ROWS * FFN_TILES
    nt = t // step
    assert t % step == 0 and (FFN_ROWS // SUBLANES) % SUBLANES == 0
    out, conv = pl.pallas_call(
        _ffn_prompt_kernel,
        grid=(b, nt),
        in_specs=[pl.BlockSpec((step, d), lambda i, j: (i * nt + j, 0)),
                  pl.BlockSpec((step, PLE_DIM), lambda i, j: ((layer * b + i) * nt + j, 0))]
        + [_layer(w[k], layer) for k in _FFN_KEYS],
        out_specs=[pl.BlockSpec((step, d), lambda i, j: (i * nt + j, 0)),
                   pl.BlockSpec((1, 2, f2), lambda i, j: (i, 0, 0))],
        out_shape=[jax.ShapeDtypeStruct((b * t, d), F32),
                   jax.ShapeDtypeStruct((b, 2, f2), F32)],
        scratch_shapes=[pltpu.VMEM((2 * SUBLANES, f2), F32),
                        pltpu.VMEM((FFN_TILES, d // LANES, FFN_ROWS, LANES), F32),
                        pltpu.VMEM((FFN_TILES, PLE_DIM // LANES, FFN_ROWS, LANES), F32),
                        pltpu.VMEM((FFN_TILES, FFN_ROWS, d), F32)],
        compiler_params=_params("arbitrary", "arbitrary"),
        name="ffn_prompt",
    )(h.reshape(b * t, d), p_all.reshape(-1, PLE_DIM), *[w[k] for k in _FFN_KEYS])
    return out.reshape(b, t, d), conv


def _interleave_rows(a):
    lead, n = a.shape[:-2], a.shape[-1]
    a = a.reshape(lead + (2, n // LANES, LANES))
    return jnp.swapaxes(a, -3, -2).reshape(lead + (2 * n,))


def _deinterleave_rows(a):
    lead, n = a.shape[:-1], a.shape[-1] // 2
    a = a.reshape(lead + (n // LANES, 2, LANES))
    return jnp.swapaxes(a, -3, -2).reshape(lead + (2, n))


def _ffn_sample(h, p, state_all, w, layer, state_stack):
    b, d = h.shape
    args = [h, p, state_all]
    specs = [_resident(h.shape), _resident(p.shape), _layer(state_all, layer)]
    args += [w[k] for k in _FFN_KEYS]
    specs += [_layer(w[k], layer) for k in _FFN_KEYS]
    aliases = {}
    if state_stack is not None:
        args.append(state_stack)
        specs.append(pl.BlockSpec(memory_space=pl.ANY))
        aliases = {len(args) - 1: 1}
    out, state_stack = pl.pallas_call(
        _ffn_sample_kernel,
        grid=(1,),
        in_specs=specs,
        out_specs=[_whole((b, d)),
                   pl.BlockSpec((None,) + state_all.shape[1:], lambda i: (layer, 0, 0))],
        out_shape=[jax.ShapeDtypeStruct((b, d), F32), jax.ShapeDtypeStruct(state_all.shape, F32)],
        scratch_shapes=[pltpu.VMEM((b, d), F32)],
        input_output_aliases=aliases,
        compiler_params=_params("arbitrary"),
        name="ffn_sample",
    )(*args)
    return out, state_stack


_ML_KEYS = ("g_pre_mix", "g_post_mix", "ml_win", "ml_wgc", "ml_bgc", "ml_wgr", "ml_bgr", "ml_wout")


def _mlstm_prompt_kernel(x_ref, gpre_ref, gpost_ref, win_ref, wgc_ref, bgc_ref, wgr_ref, bgr_ref, wout_ref,
                         out_ref, c_out_ref, n_out_ref, m_out_ref, c_ref, n_ref, m_ref):
    t = pl.program_id(1)
    L = ML_CHUNK_ROWS
    nchunk = x_ref.shape[0] // L
    heads = range(ML_HEADS)

    @pl.when(t == 0)
    def _():
        c_ref[...] = jnp.zeros_like(c_ref)
        n_ref[...] = jnp.zeros_like(n_ref)
        m_ref[...] = jnp.zeros_like(m_ref)

    causal = lax.broadcasted_iota(jnp.int32, (L, L), 0) >= lax.broadcasted_iota(jnp.int32, (L, L), 1)

    def project_qk(i):
        r = slice(i * L, (i + 1) * L)
        xn = _rms(x_ref[r, :], gpre_ref[...]).astype(BF16)
        gc = _dot(xn, wgc_ref[...]) + bgc_ref[...]
        gr = _dot_nt(wgr_ref[...], xn) + bgr_ref[:, 0:1]
        proj = _dot(xn, win_ref[:, :2 * ML_HK])
        is_forget_r = lax.broadcasted_iota(jnp.int32, gr.shape, 0) >= ML_HEADS
        bc = _cumsum(_log_sigmoid(gc), 0)
        br = _cumsum(jnp.where(is_forget_r, _log_sigmoid(gr), 0.0), 1)
        q = [proj[:, h * ML_DK:(h + 1) * ML_DK] * (ML_DK ** -0.5) for h in heads]
        k = [proj[:, ML_HK + h * ML_DK:ML_HK + (h + 1) * ML_DK] for h in heads]
        return dict(xn=xn, gc=gc, gr=gr, bc=bc, br=br, q=q, k=k)

    def head_scores(p):
        qb = [a.astype(BF16) for a in p["q"]]
        qk = [_dot_nt(qb[h], p["k"][h].astype(BF16)) for h in heads]
        qc = [_dot(qb[h], c_ref[h].astype(BF16)) for h in heads]
        return qk, qc

    def project_vo(p):
        proj = _dot(p["xn"], win_ref[:, 2 * ML_HK:])
        return [proj[:, h * ML_DV:(h + 1) * ML_DV].astype(BF16) for h in heads], proj[:, ML_HEADS * ML_DV:]

    def gating(p, qk):
        out = []
        for h in heads:
            ig_r = p["gr"][h:h + 1, :]
            b_r = p["br"][ML_HEADS + h:ML_HEADS + h + 1, :]
            ig_c = p["gc"][:, h:h + 1]
            b_c = p["bc"][:, ML_HEADS + h:ML_HEADS + h + 1]
            m_prev = m_ref[h:h + 1, 0:1]
            n_prev = n_ref[h:h + 1, :]

            d = jnp.where(causal, b_c + (ig_r - b_r), -jnp.inf)
            a = b_c + m_prev
            m_t = jnp.maximum(a, jnp.max(d, axis=1, keepdims=True))
            w_inter = jnp.exp(a - m_t)
            s = qk[h] * jnp.exp(d - m_t)
            qn = w_inter * jnp.sum(p["q"][h] * n_prev, axis=1, keepdims=True) + jnp.sum(s, axis=1, keepdims=True)
            inv_den = 1.0 / jnp.maximum(jnp.abs(qn), jnp.exp(-m_t))

            b_last = b_r[:, L - 1:L]
            g_r = b_last - b_r + ig_r
            g_c = b_last - b_c + ig_c
            m_new = jnp.maximum(b_last + m_prev, jnp.max(g_r, axis=1, keepdims=True))
            w_old = jnp.exp(b_last + m_prev - m_new)
            wk = jnp.exp(g_c - m_new) * p["k"][h]
            out.append(dict(s=s.astype(BF16), wk=wk.astype(BF16), w_inter=w_inter, inv_den=inv_den, w_old=w_old,
                            n_new=w_old * n_prev + jnp.sum(wk, axis=0, keepdims=True), m_new=m_new))
        return out

    def finish(i, hidden, o):
        y = _dot((_sigmoid(o) * hidden).astype(BF16), wout_ref[...])
        r = slice(i * L, (i + 1) * L)
        out_ref[r, :] = x_ref[r, :] + _rms(y, gpost_ref[...])

    p = project_qk(0)
    qk, qc = head_scores(p)
    vb, o = project_vo(p)
    for i in range(nchunk):
        p_next = project_qk(i + 1) if i + 1 < nchunk else None
        g = gating(p, qk)
        sv = [_dot(g[h]["s"], vb[h]) for h in heads]
        kv = [_dot_tn(g[h]["wk"], vb[h]) for h in heads]
        hidden = jnp.concatenate([(g[h]["w_inter"] * qc[h] + sv[h]) * g[h]["inv_den"] for h in heads], axis=1)
        for h in heads:
            c_ref[h] = g[h]["w_old"] * c_ref[h] + kv[h]
            n_ref[h:h + 1, :] = g[h]["n_new"]
            m_ref[h:h + 1, :] = jnp.broadcast_to(g[h]["m_new"], (1, LANES))
        o_done = o
        if p_next is not None:
            p = p_next
            qk, qc = head_scores(p)
            vb, o = project_vo(p)
        finish(i, hidden, o_done)

    @pl.when(t == pl.num_programs(1) - 1)
    def _():
        c_out_ref[0] = c_ref[...]
        n_out_ref[0] = n_ref[0:ML_HEADS, :]
        m_out_ref[0] = m_ref[...]


def _ml_layers(layer):
    return [2 * layer, 2 * layer] + [layer] * (len(_ML_KEYS) - 2)


def _mlstm_prompt(h, w, layer):
    b, t, d = h.shape
    L = ML_STEP_ROWS
    nt = t // L
    assert t % L == 0 and L % ML_CHUNK_ROWS == 0
    out, c, n, m = pl.pallas_call(
        _mlstm_prompt_kernel,
        grid=(b, nt),
        in_specs=[pl.BlockSpec((L, d), lambda i, j: (i * nt + j, 0))]
        + [_layer(w[k], l) for k, l in zip(_ML_KEYS, _ml_layers(layer))],
        out_specs=[pl.BlockSpec((L, d), lambda i, j: (i * nt + j, 0)),
                   pl.BlockSpec((1, ML_HEADS, ML_DK, ML_DV), lambda i, j: (i, 0, 0, 0)),
                   pl.BlockSpec((1, ML_HEADS, ML_DK), lambda i, j: (i, 0, 0)),
                   pl.BlockSpec((1, SUBLANES, LANES), lambda i, j: (i, 0, 0))],
        out_shape=[jax.ShapeDtypeStruct((b * t, d), F32),
                   jax.ShapeDtypeStruct((b, ML_HEADS, ML_DK, ML_DV), F32),
                   jax.ShapeDtypeStruct((b, ML_HEADS, ML_DK), F32),
                   jax.ShapeDtypeStruct((b, SUBLANES, LANES), F32)],
        scratch_shapes=[pltpu.VMEM((ML_HEADS, ML_DK, ML_DV), F32), pltpu.VMEM((SUBLANES, ML_DK), F32),
                        pltpu.VMEM((SUBLANES, LANES), F32)],
        compiler_params=_params("arbitrary", "arbitrary"),
        name="mlstm_prompt",
    )(h.reshape(b * t, d), *[w[k] for k in _ML_KEYS])
    return out.reshape(b, t, d), c, n, m[:, :ML_HEADS, 0]


def _mlstm_sample_pre_kernel(x_ref, n_ref, m_ref, gpre_ref, win_ref, wgc_ref, bgc_ref,
                             q_ref, kw_ref, v_ref, o_ref, wold_ref, scal_ref, nnew_ref):
    H = ML_HEADS
    xn = _rms(x_ref[...], gpre_ref[...]).astype(BF16)
    proj = _dot(xn, win_ref[...])
    gates = _dot(xn, wgc_ref[...]) + bgc_ref[...]
    v_ref[...] = proj[:, 2 * ML_HK:2 * ML_HK + H * ML_DV]
    o_ref[...] = proj[:, 2 * ML_HK + H * ML_DV:]
    scal_ref[...] = jnp.zeros_like(scal_ref)
    for h in range(H):
        q = proj[:, h * ML_DK:(h + 1) * ML_DK] * (ML_DK ** -0.5)
        k = proj[:, ML_HK + h * ML_DK:ML_HK + (h + 1) * ML_DK]
        n_prev = n_ref[:, h * ML_DK:(h + 1) * ML_DK]
        ig = gates[:, h:h + 1]
        lf = _log_sigmoid(gates[:, H + h:H + h + 1])
        a = lf + m_ref[:, h:h + 1]
        m_t = jnp.maximum(a, ig)
        w_old = jnp.exp(a - m_t)
        w_new = jnp.exp(ig - m_t)
        s = jnp.sum(q * k, axis=1, keepdims=True) * w_new
        qn = w_old * jnp.sum(q * n_prev, axis=1, keepdims=True) + s
        q_ref[:, h * ML_DK:(h + 1) * ML_DK] = q
        kw_ref[:, h * ML_DK:(h + 1) * ML_DK] = w_new * k
        nnew_ref[:, h * ML_DK:(h + 1) * ML_DK] = w_old * n_prev + w_new * k
        wold_ref[:, h * ML_DV:(h + 1) * ML_DV] = jnp.broadcast_to(w_old, (w_old.shape[0], ML_DV))
        scal_ref[:, h:h + 1] = w_old
        scal_ref[:, H + h:H + h + 1] = s
        scal_ref[:, 2 * H + h:2 * H + h + 1] = jnp.maximum(jnp.abs(qn), jnp.exp(-m_t))
        scal_ref[:, 3 * H + h:3 * H + h + 1] = m_t


def _mlstm_sample_state_kernel(c_ref, qt_ref, kt_ref, v_ref, wold_ref, *rest):
    cnew_ref, qc_ref = rest[-2:]
    for b in range(c_ref.shape[0]):
        for h in range(ML_HEADS):
            c = c_ref[b, h]
            qc = qt_ref[0, h * ML_DK:(h + 1) * ML_DK, b:b + 1]
            kc = kt_ref[0, h * ML_DK:(h + 1) * ML_DK, b:b + 1]
            vr = v_ref[b:b + 1, h * ML_DV:(h + 1) * ML_DV]
            wo = wold_ref[b:b + 1, h * ML_DV:(h + 1) * ML_DV]
            qc_ref[b:b + 1, h * ML_DV:(h + 1) * ML_DV] = jnp.sum(qc * c, axis=0, keepdims=True)
            cnew_ref[b, h] = c * wo + kc * vr


def _mlstm_sample_post_kernel(x_ref, qc_ref, v_ref, o_ref, scal_ref, gpost_ref, wout_ref, out_ref, h_ref):
    H = ML_HEADS
    for h in range(H):
        sl = slice(h * ML_DV, (h + 1) * ML_DV)
        num = scal_ref[:, h:h + 1] * qc_ref[:, sl] + scal_ref[:, H + h:H + h + 1] * v_ref[:, sl]
        h_ref[:, sl] = num / scal_ref[:, 2 * H + h:2 * H + h + 1]
    y = _dot((_sigmoid(o_ref[...]) * h_ref[...]).astype(BF16), wout_ref[...])
    out_ref[...] = x_ref[...] + _rms(y, gpost_ref[...])


def _aliased(stack_prev, args, in_specs):
    if stack_prev is None:
        return {}
    args.append(stack_prev)
    in_specs.append(pl.BlockSpec(memory_space=pl.ANY))
    return {len(args) - 1: 0}


def _mlstm_sample(h, c_all, n, m, w, layer, c_stack):
    b, d = h.shape
    H = ML_HEADS
    bb = SAMPLE_STATE_BATCH
    nb = b // bb
    assert b % bb == 0
    pre_in = [h, n.reshape(b, H * ML_DK), jnp.pad(m, ((0, 0), (0, LANES - H)))]
    pre_keys = ("g_pre_mix", "ml_win", "ml_wgc", "ml_bgc")
    pre_layers = (2 * layer, layer, layer, layer)
    pre_out = [(b, ML_HK), (b, ML_HK), (b, H * ML_DV), (b, H * ML_DV), (b, H * ML_DV), (b, LANES), (b, ML_HK)]
    q, kw, v, o, wold, scal, nnew = pl.pallas_call(
        _mlstm_sample_pre_kernel,
        grid=(1,),
        in_specs=[_resident(a.shape) for a in pre_in] + [_layer(w[k], l) for k, l in zip(pre_keys, pre_layers)],
        out_specs=[_whole(s) for s in pre_out],
        out_shape=[jax.ShapeDtypeStruct(s, F32) for s in pre_out],
        compiler_params=_params("arbitrary"),
        name="mlstm_sample_pre",
    )(*pre_in, *[w[k] for k in pre_keys])

    def cols(a):
        return a.reshape(nb, bb, ML_HK).transpose(0, 2, 1)

    state_args = [c_all, cols(q), cols(kw), v, wold]
    state_specs = [pl.BlockSpec((None, bb, H, ML_DK, ML_DV), lambda i: (layer, i, 0, 0, 0)),
                   pl.BlockSpec((1, ML_HK, bb), lambda i: (i, 0, 0)),
                   pl.BlockSpec((1, ML_HK, bb), lambda i: (i, 0, 0)),
                   pl.BlockSpec((bb, H * ML_DV), lambda i: (i, 0)),
                   pl.BlockSpec((bb, H * ML_DV), lambda i: (i, 0))]
    aliases = _aliased(c_stack, state_args, state_specs)
    c_stack, qc = pl.pallas_call(
        _mlstm_sample_state_kernel,
        grid=(nb,),
        in_specs=state_specs,
        out_specs=[pl.BlockSpec((None, bb, H, ML_DK, ML_DV), lambda i: (layer, i, 0, 0, 0)),
                   pl.BlockSpec((bb, H * ML_DV), lambda i: (i, 0))],
        out_shape=[jax.ShapeDtypeStruct(c_all.shape, F32), jax.ShapeDtypeStruct((b, H * ML_DV), F32)],
        input_output_aliases=aliases,
        compiler_params=_params("arbitrary"),
        name="mlstm_sample_state",
    )(*state_args)

    post_in = [h, qc, v, o, scal]
    out = pl.pallas_call(
        _mlstm_sample_post_kernel,
        grid=(1,),
        in_specs=[_resident(a.shape) for a in post_in]
        + [_layer(w["g_post_mix"], 2 * layer), _layer(w["ml_wout"], layer)],
        out_specs=_whole((b, d)),
        out_shape=jax.ShapeDtypeStruct((b, d), F32),
        scratch_shapes=[pltpu.VMEM((b, H * ML_DV), F32)],
        compiler_params=_params("arbitrary"),
        name="mlstm_sample_post",
    )(*post_in, w["g_post_mix"], w["ml_wout"])
    return out, c_stack, nnew.reshape(b, H, ML_DK), scal[:, 3 * H:4 * H]


def _rope_tables(pos):
    half = SW_HD // 2
    inv = ROPE_THETA ** (-jnp.arange(half, dtype=F32) / half)
    ang = pos[:, None] * inv[None, :]
    cos = jnp.tile(jnp.cos(ang), (1, LANES // half))
    sin = jnp.tile(jnp.concatenate([-jnp.sin(ang), jnp.sin(ang)], axis=1), (1, LANES // SW_HD))
    return cos, sin


def _first_half_lanes(rows):
    lane = lax.broadcasted_iota(jnp.int32, (rows, LANES), 1)
    return jnp.bitwise_and(lane, SW_HD - 1) < SW_HD // 2


def _rope_block(x, cos, sin, first_half):
    partner = jnp.where(first_half, pltpu.roll(x, LANES - SW_HD // 2, axis=1), pltpu.roll(x, SW_HD // 2, axis=1))
    return x * cos + partner * sin


def _pair_heads(a, axis):
    shp = a.shape
    a = a.reshape(shp[:axis] + (2, 2, SW_GROUP) + shp[axis + 1:])
    a = jnp.swapaxes(a, axis + 1, axis + 2)
    return a.reshape(shp)


_SW_KEYS = ("g_pre_mix", "g_post_mix", "sw_win", "sw_wout")


def _swa_prompt_kernel(sink_ref, x_ref, cos_ref, sin_ref, gpre_ref, gpost_ref, win_ref, wout_ref,
                       out_ref, kout_ref, vout_ref, k_ref, v_ref, kf_ref, vf_ref):
    t = pl.program_id(1)
    rows = x_ref.shape[0]
    nsub = rows // WINDOW
    W = WINDOW
    G = SW_GROUP
    K2 = 2 * W
    pairs = range(SW_KV_HEADS // 2)

    @pl.when(t == 0)
    def _():
        k_ref[0:W, :] = jnp.zeros((W, SW_NKV), BF16)
        v_ref[0:W, :] = jnp.zeros((W, SW_NKV), BF16)

    first_half = _first_half_lanes(SW_UNIT * W)
    qi = jnp.bitwise_and(lax.broadcasted_iota(jnp.int32, (G * W, K2), 0), W - 1)
    ki = lax.broadcasted_iota(jnp.int32, (G * W, K2), 1)
    band = (ki >= qi) & (ki <= qi + W)
    lane = lax.broadcasted_iota(jnp.int32, (1, LANES), 1)
    low = lane < SW_HD
    high = jnp.logical_not(low)
    zeros = jnp.zeros((1, LANES), BF16)
    ones_low = jnp.where(low, 1.0, 0.0).astype(BF16)
    ones_high = jnp.where(high, 1.0, 0.0).astype(BF16)
    member = lax.shift_right_logical(lax.broadcasted_iota(jnp.int32, (G * W, 1), 0),
                                     jnp.int32(W.bit_length() - 1))

    def sink_column(pair, half):
        col = jnp.full((G * W, 1), sink_ref[2 * (pair * G) + half], F32)
        for g in range(1, G):
            col = jnp.where(member == g, sink_ref[2 * (pair * G + g) + half], col)
        return col

    sinks = [[sink_column(pair, half) for half in range(2)] for pair in pairs]

    U = SW_UNIT

    def project(u):
        r = slice(u * U * W, (u + 1) * U * W)
        xn = _rms(x_ref[r, :], gpre_ref[...]).astype(BF16)
        proj = _dot(xn, win_ref[:, :SW_NQ + SW_NKV])
        v = _dot(xn, win_ref[:, SW_NQ + SW_NKV:])
        cos = cos_ref[r, :]
        sin = sin_ref[r, :]
        q = [(_rope_block(proj[:, c * LANES:(c + 1) * LANES], cos, sin, first_half)
              * (SW_HD ** -0.5 * LOG2E)).astype(BF16) for c in range(SW_NQ // LANES)]
        for c in range(SW_NKV // LANES):
            blk = _rope_block(proj[:, SW_NQ + c * LANES:SW_NQ + (c + 1) * LANES], cos, sin, first_half)
            kf_ref[r, c * LANES:(c + 1) * LANES] = blk
            k_ref[W + u * U * W:W + (u + 1) * U * W, c * LANES:(c + 1) * LANES] = blk.astype(BF16)
        vf_ref[r, :] = v
        v_ref[W + u * U * W:W + (u + 1) * U * W, :] = v.astype(BF16)
        return [[jnp.concatenate([a[j * W:(j + 1) * W] for a in q[pair * G:(pair + 1) * G]], axis=0)
                 for pair in pairs] for j in range(U)]

    def scores(i, q4):
        out = []
        for pair in pairs:
            k2 = k_ref[i * W:(i + 2) * W, pair * LANES:(pair + 1) * LANES]
            kz = jnp.concatenate([jnp.where(low, k2, zeros), jnp.where(high, k2, zeros)], axis=0)
            out.append(_dot_nt(q4[pair], kz))
        return out

    def softmax_numerators(i, s):
        first_key = jnp.where(t * nsub + i > 0, 0, W)
        mask = band & (ki >= first_key)
        out = []
        for pair in pairs:
            probs, sink_terms = [], []
            for half in range(2):
                sh = jnp.where(mask, s[pair][:, half * K2:(half + 1) * K2], -jnp.inf)
                sk = sinks[pair][half]
                mx = jnp.maximum(jnp.max(sh, axis=1, keepdims=True), sk)
                probs.append(jnp.exp2(sh - mx).astype(BF16))
                sink_terms.append(jnp.exp2(sk - mx))
            out.append((jnp.concatenate(probs, axis=1), jnp.where(low, sink_terms[0], sink_terms[1])))
        return out

    def weighted_values(i, soft):
        out = []
        for pair in pairs:
            v2 = v_ref[i * W:(i + 2) * W, pair * LANES:(pair + 1) * LANES]
            vz = jnp.concatenate(
                [jnp.concatenate([jnp.where(low, v2, zeros), jnp.broadcast_to(ones_low, v2.shape)], axis=1),
                 jnp.concatenate([jnp.where(high, v2, zeros), jnp.broadcast_to(ones_high, v2.shape)], axis=1)],
                axis=0)
            out.append((_dot(soft[pair][0], vz), soft[pair][1]))
        return out

    def finish(u, weighted):
        rows_u = []
        for j in range(U):
            blocks = []
            for o, sink_term in weighted[j]:
                both = (o[:, :LANES] / (o[:, LANES:] + sink_term)).astype(BF16)
                blocks += [both[g * W:(g + 1) * W] for g in range(G)]
            rows_u.append(jnp.concatenate(blocks, axis=1))
        y = _dot(jnp.concatenate(rows_u, axis=0), wout_ref[...])
        r = slice(u * U * W, (u + 1) * U * W)
        out_ref[r, :] = x_ref[r, :] + _rms(y, gpost_ref[...])

    nunit = nsub // U
    q_unit = project(0)
    s = scores(0, q_unit[0])
    weighted, done = [], None
    for i in range(nsub):
        u, j = divmod(i, U)
        q_next = project(u + 1) if j == U - 1 and u + 1 < nunit else None
        soft = softmax_numerators(i, s)
        if j == 0 and done is not None:
            finish(u - 1, done)
            done = None
        weighted.append(weighted_values(i, soft))
        if j == U - 1:
            done, weighted = weighted, []
            if q_next is not None:
                q_unit = q_next
        if i + 1 < nsub:
            s = scores(i + 1, q_unit[(i + 1) % U])
    finish(nunit - 1, done)

    k_ref[0:W, :] = k_ref[rows:rows + W, :]
    v_ref[0:W, :] = v_ref[rows:rows + W, :]

    @pl.when(t == pl.num_programs(1) - 1)
    def _():
        kout_ref[0] = kf_ref[rows - W:, :].T
        vout_ref[0] = vf_ref[rows - W:, :].T


def _swa_prompt(h, w, layer):
    b, t, d = h.shape
    rows = SW_ROWS
    nt = t // rows
    assert t % rows == 0 and rows % WINDOW == 0 and t >= WINDOW
    cos, sin = _rope_tables(jnp.arange(t, dtype=F32))
    layers = (2 * layer + 1, 2 * layer + 1, layer, layer)
    out, k, v = pl.pallas_call(
        _swa_prompt_kernel,
        grid_spec=pltpu.PrefetchScalarGridSpec(
            num_scalar_prefetch=1,
            grid=(b, nt),
            in_specs=[pl.BlockSpec((rows, d), lambda i, j, s: (i * nt + j, 0)),
                      pl.BlockSpec((rows, LANES), lambda i, j, s: (j, 0)),
                      pl.BlockSpec((rows, LANES), lambda i, j, s: (j, 0))]
            + [_layer(w[k_], l) for k_, l in zip(_SW_KEYS, layers)],
            out_specs=[pl.BlockSpec((rows, d), lambda i, j, s: (i * nt + j, 0)),
                       pl.BlockSpec((1, SW_NKV, WINDOW), lambda i, j, s: (i, 0, 0)),
                       pl.BlockSpec((1, SW_NKV, WINDOW), lambda i, j, s: (i, 0, 0))],
            scratch_shapes=[pltpu.VMEM((WINDOW + rows, SW_NKV), BF16), pltpu.VMEM((WINDOW + rows, SW_NKV), BF16),
                            pltpu.VMEM((rows, SW_NKV), F32), pltpu.VMEM((rows, SW_NKV), F32)]),
        out_shape=[jax.ShapeDtypeStruct((b * t, d), F32),
                   jax.ShapeDtypeStruct((b, SW_NKV, WINDOW), F32),
                   jax.ShapeDtypeStruct((b, SW_NKV, WINDOW), F32)],
        compiler_params=_params("arbitrary", "arbitrary"),
        name="swa_prompt",
    )(w["sw_sinks_log2"][layer], h.reshape(b * t, d), cos, sin, *[w[k_] for k_ in _SW_KEYS])
    return out.reshape(b, t, d), k, v


def _swa_sample_pre_kernel(x_ref, cos_ref, sin_ref, gpre_ref, win_ref, q_ref, k_ref, v_ref, kt_ref, vt_ref):
    rows = x_ref.shape[0]
    xn = _rms(x_ref[...], gpre_ref[...]).astype(BF16)
    proj = _dot(xn, win_ref[...])
    cos = cos_ref[...]
    sin = sin_ref[...]
    first_half = _first_half_lanes(rows)
    for c in range(SW_NQ // LANES):
        blk = _rope_block(proj[:, c * LANES:(c + 1) * LANES], cos, sin, first_half)
        q_ref[:, c * LANES:(c + 1) * LANES] = blk * (SW_HD ** -0.5)
    for c in range(SW_NKV // LANES):
        k_ref[:, c * LANES:(c + 1) * LANES] = _rope_block(
            proj[:, SW_NQ + c * LANES:SW_NQ + (c + 1) * LANES], cos, sin, first_half)
    v = proj[:, SW_NQ + SW_NKV:]
    v_ref[...] = v
    kt_ref[...] = k_ref[...].T
    vt_ref[...] = v.T


def _swa_sample_attn_kernel(q_ref, kc_ref, vc_ref, knew_ref, vnew_ref, kcol_ref, vcol_ref, sink_ref, *rest):
    o_ref, kout_ref, vout_ref = rest[-3:]
    nbuf = kc_ref.shape[-1]
    pos = lax.broadcasted_iota(jnp.int32, (SW_HEADS, SW_NKV), 0)
    grp = 2 * lax.shift_right_logical(pos, jnp.int32(3)) + jnp.bitwise_and(pos, 1)
    col_kv = lax.shift_right_logical(lax.broadcasted_iota(jnp.int32, (SW_HEADS, SW_NKV), 1),
                                     jnp.int32(SW_HD.bit_length() - 1))
    own = grp == col_kv
    sk = sink_ref[:, 0:1]
    lane = lax.broadcasted_iota(jnp.int32, (1, nbuf), 1)
    dist = nbuf - lane
    valid = (dist >= 0) & (dist <= WINDOW)
    newest = lane == nbuf - 1
    for b in range(q_ref.shape[0]):
        q = q_ref[b]
        kt = kc_ref[b].reshape(SW_NKV, nbuf)
        vt = vc_ref[b].reshape(SW_NKV, nbuf)
        s = jnp.where(valid, _dot(q.astype(BF16), kt.astype(BF16)), -jnp.inf)
        s_new = jnp.sum(q * knew_ref[b], axis=1, keepdims=True)
        mx = jnp.maximum(jnp.maximum(jnp.max(s, axis=1, keepdims=True), s_new), sk)
        p = jnp.exp(s - mx)
        p_new = jnp.exp(s_new - mx)
        den = jnp.sum(p, axis=1, keepdims=True) + p_new + jnp.exp(sk - mx)
        o = _dot_nt(p.astype(BF16), vt.astype(BF16)) + p_new * vnew_ref[b]
        o = jnp.where(own, o, 0.0)
        o = o[:, :LANES] + o[:, LANES:]
        o = o + pltpu.roll(o, SW_HD, axis=1)
        o_ref[b] = (o / den)[:, :SW_HD]
        kout_ref[b] = jnp.where(newest, kcol_ref[0, :, b:b + 1],
                                pltpu.roll(kt, nbuf - 1, axis=1)).reshape(SW_KV_HEADS, SW_HD, nbuf)
        vout_ref[b] = jnp.where(newest, vcol_ref[0, :, b:b + 1],
                                pltpu.roll(vt, nbuf - 1, axis=1)).reshape(SW_KV_HEADS, SW_HD, nbuf)


def _out_proj_kernel(x_ref, a_ref, gpost_ref, wout_ref, out_ref):
    y = _dot(a_ref[...].astype(BF16), wout_ref[...])
    out_ref[...] = x_ref[...] + _rms(y, gpost_ref[...])


def _swa_sample(h, kc_all, vc_all, w, layer, k_stack, v_stack):
    b, d = h.shape
    nbuf = kc_all.shape[-1]
    bb = SAMPLE_STATE_BATCH
    nb = b // bb
    assert b % bb == 0 and nbuf == LANES
    cos, sin = _rope_tables(jnp.full((1,), PAST_LEN, F32))
    pre_in = [h, cos, sin]
    pre_out = [(b, SW_NQ), (b, SW_NKV), (b, SW_NKV), (SW_NKV, b), (SW_NKV, b)]
    q, knew, vnew, kt, vt = pl.pallas_call(
        _swa_sample_pre_kernel,
        grid=(1,),
        in_specs=[_resident(a.shape) for a in pre_in]
        + [_layer(w["g_pre_mix"], 2 * layer + 1), _layer(w["sw_win"], layer)],
        out_specs=[_whole(s) for s in pre_out],
        out_shape=[jax.ShapeDtypeStruct(s, F32) for s in pre_out],
        compiler_params=_params("arbitrary"),
        name="swa_sample_pre",
    )(*pre_in, w["g_pre_mix"], w["sw_win"])

    def cols(a):
        return a.reshape(SW_NKV, nb, bb).transpose(1, 0, 2)

    cache_spec = pl.BlockSpec((None, bb, SW_KV_HEADS, SW_HD, nbuf), lambda i: (layer, i, 0, 0, 0))
    pos = jnp.arange(SW_HEADS)
    own_kv = (2 * (pos // 8) + pos % 2)[:, None] == jnp.arange(SW_KV_HEADS)[None, :]
    q_bd = jnp.where(own_kv[None, :, :, None], q.reshape(b, SW_HEADS, 1, SW_HD), 0.0).reshape(b, SW_HEADS, SW_NKV)
    args = [q_bd, kc_all, vc_all, knew.reshape(b, 1, SW_NKV), vnew.reshape(b, 1, SW_NKV),
            cols(kt), cols(vt), jnp.broadcast_to(w["sw_sinks"][layer][:, None], (SW_HEADS, LANES))]
    specs = [pl.BlockSpec((bb, SW_HEADS, SW_NKV), lambda i: (i, 0, 0)), cache_spec, cache_spec,
             pl.BlockSpec((bb, 1, SW_NKV), lambda i: (i, 0, 0)),
             pl.BlockSpec((bb, 1, SW_NKV), lambda i: (i, 0, 0)),
             pl.BlockSpec((1, SW_NKV, bb), lambda i: (i, 0, 0)),
             pl.BlockSpec((1, SW_NKV, bb), lambda i: (i, 0, 0)),
             _resident((SW_HEADS, LANES))]
    aliases = {}
    if k_stack is not None:
        args += [k_stack, v_stack]
        specs += [pl.BlockSpec(memory_space=pl.ANY), pl.BlockSpec(memory_space=pl.ANY)]
        aliases = {len(args) - 2: 1, len(args) - 1: 2}
    o, k_stack, v_stack = pl.pallas_call(
        _swa_sample_attn_kernel,
        grid=(nb,),
        in_specs=specs,
        out_specs=[pl.BlockSpec((bb, SW_HEADS, SW_HD), lambda i: (i, 0, 0)), cache_spec, cache_spec],
        out_shape=[jax.ShapeDtypeStruct((b, SW_HEADS, SW_HD), F32),
                   jax.ShapeDtypeStruct(kc_all.shape, F32), jax.ShapeDtypeStruct(vc_all.shape, F32)],
        input_output_aliases=aliases,
        compiler_params=_params("arbitrary"),
        name="swa_sample_attn",
    )(*args)

    post_in = [h, o.reshape(b, SW_NQ)]
    out = pl.pallas_call(
        _out_proj_kernel,
        grid=(1,),
        in_specs=[_resident(a.shape) for a in post_in]
        + [_layer(w["g_post_mix"], 2 * layer + 1), _layer(w["sw_wout"], layer)],
        out_specs=_whole((b, d)),
        out_shape=jax.ShapeDtypeStruct((b, d), F32),
        compiler_params=_params("arbitrary"),
        name="swa_sample_post",
    )(*post_in, w["g_post_mix"], w["sw_wout"])
    return out, k_stack, v_stack


def _prepare(g_pre_mix, g_post_mix, g_pre_ffn, g_post_ffn, g_ple, ml_w_in, ml_b_gate, ml_w_out, sw_w_in,
             sw_sinks, sw_w_out, ffn_w_up, ffn_conv_w, ffn_conv_b, ffn_w_down, ple_w_proj, ple_w_gate):
    gate_w = ml_w_in[:, :, ML_QKVO:]
    q_cols = _pair_heads(sw_w_in[:, :, :SW_NQ].reshape(sw_w_in.shape[:2] + (SW_HEADS, SW_HD)), 2)
    sw_win = jnp.concatenate([q_cols.reshape(sw_w_in.shape[:2] + (SW_NQ,)), sw_w_in[:, :, SW_NQ:]], axis=-1)
    sw_wout = _pair_heads(sw_w_out.reshape((sw_w_out.shape[0], SW_HEADS, SW_HD, sw_w_out.shape[-1])), 1)
    sinks = _pair_heads(sw_sinks, 1)
    return dict(
        g_pre_mix=g_pre_mix[:, None], g_post_mix=g_post_mix[:, None], g_pre_ffn=g_pre_ffn[:, None],
        g_post_ffn=g_post_ffn[:, None], g_ple=g_ple[:, None],
        ffn_wup=ffn_w_up.astype(BF16), ffn_cw=ffn_conv_w, ffn_cb=ffn_conv_b[:, None],
        ffn_wdown=ffn_w_down.astype(BF16), ple_wgate=ple_w_gate.astype(BF16), ple_wproj=ple_w_proj.astype(BF16),
        ml_win=ml_w_in[:, :, :ML_QKVO].astype(BF16),
        ml_wgc=jnp.pad(gate_w, ((0, 0), (0, 0), (0, LANES - 2 * ML_HEADS))).astype(BF16),
        ml_bgc=jnp.pad(ml_b_gate, ((0, 0), (0, LANES - 2 * ML_HEADS)))[:, None],
        ml_wgr=jnp.swapaxes(gate_w, 1, 2).astype(BF16),
        ml_bgr=jnp.broadcast_to(ml_b_gate[:, :, None], ml_b_gate.shape + (LANES,)),
        ml_wout=ml_w_out.astype(BF16),
        sw_win=sw_win.astype(BF16), sw_wout=sw_wout.reshape(sw_w_out.shape).astype(BF16),
        sw_sinks=sinks, sw_sinks_log2=sinks * LOG2E)


def kernel(x_prompt, x_sample, p_prompt, p_sample, state_mlstm_C, state_mlstm_n, state_mlstm_m, cache_swa_k, cache_swa_v, state_conv, g_pre_mix, g_post_mix, g_pre_ffn, g_post_ffn, g_ple, ml_w_in, ml_b_gate, ml_w_out, sw_w_in, sw_sinks, sw_w_out, ffn_w_up, ffn_conv_w, ffn_conv_b, ffn_w_down, ple_w_proj, ple_w_gate):
    depth = g_pre_mix.shape[0]
    w = _prepare(g_pre_mix, g_post_mix, g_pre_ffn, g_post_ffn, g_ple, ml_w_in, ml_b_gate, ml_w_out, sw_w_in,
                 sw_sinks, sw_w_out, ffn_w_up, ffn_conv_w, ffn_conv_b, ffn_w_down, ple_w_proj, ple_w_gate)
    kc_all = jnp.transpose(cache_swa_k, (0, 1, 3, 4, 2))
    vc_all = jnp.transpose(cache_swa_v, (0, 1, 3, 4, 2))
    hp = x_prompt
    hs = x_sample[:, 0, :]
    cp, np_, mp, ns, ms = [], [], [], [], []
    kp, vp = [], []
    convp = []
    conv_all = _interleave_rows(state_conv)
    c_stack = k_stack = v_stack = conv_stack = None
    for i in range(depth):
        j = i // 2
        if i % 2 == 0:
            hp, c1, n1, m1 = _mlstm_prompt(hp, w, j)
            hs, c_stack, n2, m2 = _mlstm_sample(hs, state_mlstm_C, state_mlstm_n[j], state_mlstm_m[j], w, j, c_stack)
            cp.append(c1); np_.append(n1); mp.append(m1)
            ns.append(n2); ms.append(m2)
        else:
            hp, k1, v1 = _swa_prompt(hp, w, j)
            hs, k_stack, v_stack = _swa_sample(hs, kc_all, vc_all, w, j, k_stack, v_stack)
            kp.append(k1); vp.append(v1)
        hp, cvp = _ffn_prompt(hp, p_prompt, w, i)
        hs, conv_stack = _ffn_sample(hs, p_sample[i][:, 0, :], conv_all, w, i, conv_stack)
        convp.append(cvp)

    def cache_layout(a):
        return jnp.transpose(a, (0, 1, 4, 2, 3))

    kv_shape = (len(kp), x_prompt.shape[0], SW_KV_HEADS, SW_HD, WINDOW)
    return (hp, hs[:, None, :],
            jnp.stack(cp), jnp.stack(np_), jnp.stack(mp),
            cache_layout(jnp.stack(kp).reshape(kv_shape)), cache_layout(jnp.stack(vp).reshape(kv_shape)),
            jnp.stack(convp),
            c_stack, jnp.stack(ns), jnp.stack(ms), cache_layout(k_stack), cache_layout(v_stack),
            _deinterleave_rows(conv_stack))
```

```python
import math

import jax
import jax.numpy as jnp
from jax import lax
from jax.experimental import pallas as pl
from jax.experimental.pallas import tpu as pltpu

F32 = jnp.float32
BF16 = jnp.bfloat16

EPS = 1e-6
PLE_DIM = 256
ML_HEADS = 4
ML_DK = 128
ML_DV = 256
ML_HK = ML_HEADS * ML_DK
ML_QKVO = 2 * ML_HK + 2 * ML_HEADS * ML_DV
SW_HEADS = 16
SW_KV_HEADS = 4
SW_HD = 64
SW_GROUP = SW_HEADS // SW_KV_HEADS
SW_NQ = SW_HEADS * SW_HD
SW_NKV = SW_KV_HEADS * SW_HD
WINDOW = 128
ROPE_THETA = 10000.0
PAST_LEN = 8192
LOG2E = math.log2(math.e)

LANES = 128
SUBLANES = 8
FFN_CHUNK = 256
FFN_DOWN_GROUP = 4
FFN_ROWS = 512
FFN_TILES = 2
ML_CHUNK_ROWS = 256
ML_STEP_ROWS = 1024
SW_ROWS = 1024
SW_UNIT = 2
SAMPLE_STATE_BATCH = 8
VMEM_LIMIT = 60 * 1024 * 1024

assert SW_KV_HEADS == 4 and SW_GROUP == 4 and 2 * SW_HD == LANES and WINDOW == LANES


def _params(*sem):
    return pltpu.CompilerParams(dimension_semantics=sem, vmem_limit_bytes=VMEM_LIMIT)


def _resident(shape):
    zeros = (0,) * len(shape)
    return pl.BlockSpec(shape, lambda *_: zeros, pipeline_mode=pl.Buffered(1))


def _layer(a, layer):
    zeros = (0,) * (a.ndim - 1)
    return pl.BlockSpec((None,) + a.shape[1:], lambda *_: (layer,) + zeros, pipeline_mode=pl.Buffered(1))


def _whole(shape):
    zeros = (0,) * len(shape)
    return pl.BlockSpec(shape, lambda *_: zeros)


def _rms(x, g):
    return x * lax.rsqrt(jnp.mean(x * x, axis=-1, keepdims=True) + EPS) * g


def _sigmoid(x):
    return 1.0 / (1.0 + jnp.exp(-x))


def _log_sigmoid(x):
    return jnp.minimum(x, 0.0) - jnp.log1p(jnp.exp(-jnp.abs(x)))


def _dot(a, b):
    return jnp.dot(a, b, preferred_element_type=F32)


def _dot_nt(a, b):
    return lax.dot_general(a, b, (((1,), (1,)), ((), ())), preferred_element_type=F32)


def _dot_tn(a, b):
    return lax.dot_general(a, b, (((0,), (0,)), ((), ())), preferred_element_type=F32)


def _cumsum(x, axis):
    n = x.shape[axis]
    idx = lax.broadcasted_iota(jnp.int32, x.shape, axis)
    s = 1
    while s < n:
        x = x + jnp.where(idx >= s, pltpu.roll(x, s, axis=axis), 0.0)
        s *= 2
    return x


def _ffn_tail(x, f, p, gpost_ref, gple_ref, wgate_ref, wproj_ref):
    x1 = x + _rms(f, gpost_ref[...])
    gate = _sigmoid(_dot(_rms(x1, gple_ref[...]).astype(BF16), wgate_ref[...]))
    return x1 + gate * _dot(p.astype(BF16), wproj_ref[...])


def _ffn_chunk_cols(j, hidden):
    return (slice(j * FFN_CHUNK, (j + 1) * FFN_CHUNK),
            slice(hidden + j * FFN_CHUNK, hidden + (j + 1) * FFN_CHUNK))


def _ffn_prompt_kernel(x_ref, p_ref, gpre_ref, gpost_ref, gple_ref, wup_ref, cw_ref, cb_ref, wdown_ref,
                       wgate_ref, wproj_ref, out_ref, conv_ref, carry_ref, xp_ref, pp_ref, acc_ref):
    t = pl.program_id(1)
    rows = FFN_ROWS
    ntile = x_ref.shape[0] // rows
    hidden = wdown_ref.shape[0]
    nch = hidden // FFN_CHUNK
    S = SUBLANES
    G = rows // S
    nx = xp_ref.shape[1]
    npl = pp_ref.shape[1]

    @pl.when(t == 0)
    def _():
        carry_ref[...] = jnp.zeros_like(carry_ref)

    def interleaved(i):
        return pl.ds(((S * i) % G) * S + (S * i) // G, S, stride=S)

    first = lax.broadcasted_iota(jnp.int32, (S, FFN_CHUNK), 0) == 0

    def load_tile(n):
        for i in range(G):
            src = slice(n * rows + S * i, n * rows + S * (i + 1))
            for c in range(nx):
                xp_ref[n, c, interleaved(i), :] = x_ref[src, c * LANES:(c + 1) * LANES]
            for c in range(npl):
                pp_ref[n, c, interleaved(i), :] = p_ref[src, c * LANES:(c + 1) * LANES]
        x = jnp.concatenate([xp_ref[n, c] for c in range(nx)], axis=1)
        p = jnp.concatenate([pp_ref[n, c] for c in range(npl)], axis=1)
        ple = _dot(p.astype(BF16), wproj_ref[...])
        return x, ple, _rms(x, gpre_ref[...]).astype(BF16)

    def conv(u, cols):
        prev = carry_ref[:, cols]
        back1 = jnp.where(first, prev[2 * S - 1:2 * S], pltpu.roll(u[rows - S:], 1, axis=0))
        back2 = jnp.where(first, prev[S - 1:S], pltpu.roll(u[rows - 2 * S:rows - S], 1, axis=0))
        u1 = jnp.concatenate([back1, u[:rows - S]], axis=0)
        u2 = jnp.concatenate([back2, back1, u[:rows - 2 * S]], axis=0)
        carry_ref[:, cols] = u[rows - 2 * S:]
        cw = cw_ref[:, cols]
        return cb_ref[:, cols] + u2 * cw[0:1] + u1 * cw[1:2] + u * cw[2:3]

    def up(xn, j):
        return [_dot(xn, wup_ref[:, cols]) for cols in _ffn_chunk_cols(j, hidden)]

    def gate_of(n, x):
        out = []
        for part in range(2):
            r = slice(part * (rows // 2), (part + 1) * (rows // 2))
            x1 = x[r] + _rms(acc_ref[n, r, :], gpost_ref[...])
            out.append((x1, _dot(_rms(x1, gple_ref[...]).astype(BF16), wgate_ref[...])))
        return out

    def store_tile(n, gated, ple):
        for part, (x1, gate) in enumerate(gated):
            r = slice(part * (rows // 2), (part + 1) * (rows // 2))
            res = x1 + _sigmoid(gate) * ple[r]
            for c in range(nx):
                xp_ref[n, c, r, :] = res[:, c * LANES:(c + 1) * LANES]
        for i in range(G):
            dst = slice(n * rows + S * i, n * rows + S * (i + 1))
            for c in range(nx):
                out_ref[dst, c * LANES:(c + 1) * LANES] = xp_ref[n, c, interleaved(i), :]

    tiles = [load_tile(n) for n in range(ntile)]
    tail = None
    gated = None
    for n in range(ntile):
        x, ple, xn = tiles[n]
        u = up(xn, 0)
        pending = []
        for j in range(nch):
            u_next = up(xn, j + 1) if j + 1 < nch else None
            if tail is not None and j == 2:
                gated = gate_of(tail[0], tail[1])
            if tail is not None and j == 5:
                store_tile(tail[0], gated, tail[2])
                tail = None
            gcols, vcols = _ffn_chunk_cols(j, hidden)
            cg = conv(u[0], gcols)
            pending.append((cg * _sigmoid(cg) * conv(u[1], vcols)).astype(BF16))
            if len(pending) == FFN_DOWN_GROUP or j == nch - 1:
                lo = (j + 1 - len(pending)) * FFN_CHUNK
                d = _dot(jnp.concatenate(pending, axis=1), wdown_ref[lo:(j + 1) * FFN_CHUNK, :])
                if lo == 0:
                    acc_ref[n] = d
                else:
                    acc_ref[n] += d
                pending = []
            u = u_next
        tail = (n, x, ple)
    store_tile(tail[0], gate_of(tail[0], tail[1]), tail[2])

    @pl.when(t == pl.num_programs(1) - 1)
    def _():
        conv_ref[0, 0:1, :] = carry_ref[S - 1:S, :]
        conv_ref[0, 1:2, :] = carry_ref[2 * S - 1:2 * S, :]


def _ffn_sample_kernel(x_ref, p_ref, prev0_ref, prev1_ref, gpre_ref, gpost_ref, gple_ref, wup_ref, cw_ref,
                       cb_ref, wdown_ref, wgate_ref, wproj_ref, out_ref, u_ref, acc_ref):
    hidden = wdown_ref.shape[0]
    x = x_ref[...]
    xn = _rms(x, gpre_ref[...]).astype(BF16)

    def conv(cols):
        u = _dot(xn, wup_ref[:, cols])
        u_ref[:, cols] = u
        cw = cw_ref[:, cols]
        return cb_ref[:, cols] + prev0_ref[:, cols] * cw[0:1] + prev1_ref[:, cols] * cw[1:2] + u * cw[2:3]

    for j in range(hidden // FFN_CHUNK):
        gcols, vcols = _ffn_chunk_cols(j, hidden)
        cg = conv(gcols)
        h = cg * _sigmoid(cg) * conv(vcols)
        d = _dot(h.astype(BF16), wdown_ref[j * FFN_CHUNK:(j + 1) * FFN_CHUNK, :])
        if j == 0:
            acc_ref[...] = d
        else:
            acc_ref[...] += d
    out_ref[...] = _ffn_tail(x, acc_ref[...], p_ref[...], gpost_ref, gple_ref, wgate_ref, wproj_ref)


_FFN_KEYS = ("g_pre_ffn", "g_post_ffn", "g_ple", "ffn_wup", "ffn_cw", "ffn_cb", "ffn_wdown", "ple_wgate",
             "ple_wproj")


def _ffn_prompt(h, p_all, w, layer):
    b, t, d = h.shape
    f2 = w["ffn_wup"].shape[-1]
    step = FFN_ROWS * FFN_TILES
    nt = t // step
    assert t % step == 0 and (FFN_ROWS // SUBLANES) % SUBLANES == 0
    out, conv = pl.pallas_call(
        _ffn_prompt_kernel,
        grid=(b, nt),
        in_specs=[pl.BlockSpec((step, d), lambda i, j: (i * nt + j, 0)),
                  pl.BlockSpec((step, PLE_DIM), lambda i, j: ((layer * b + i) * nt + j, 0))]
        + [_layer(w[k], layer) for k in _FFN_KEYS],
        out_specs=[pl.BlockSpec((step, d), lambda i, j: (i * nt + j, 0)),
                   pl.BlockSpec((1, 2, f2), lambda i, j: (i, 0, 0))],
        out_shape=[jax.ShapeDtypeStruct((b * t, d), F32),
                   jax.ShapeDtypeStruct((b, 2, f2), F32)],
        scratch_shapes=[pltpu.VMEM((2 * SUBLANES, f2), F32),
                        pltpu.VMEM((FFN_TILES, d // LANES, FFN_ROWS, LANES), F32),
                        pltpu.VMEM((FFN_TILES, PLE_DIM // LANES, FFN_ROWS, LANES), F32),
                        pltpu.VMEM((FFN_TILES, FFN_ROWS, d), F32)],
        compiler_params=_params("arbitrary", "arbitrary"),
        name="ffn_prompt",
    )(h.reshape(b * t, d), p_all.reshape(-1, PLE_DIM), *[w[k] for k in _FFN_KEYS])
    return out.reshape(b, t, d), conv


def _ffn_sample(h, p, prev0_all, prev1_all, w, layer):
    b, d = h.shape
    f2 = w["ffn_wup"].shape[-1]
    return pl.pallas_call(
        _ffn_sample_kernel,
        grid=(1,),
        in_specs=[_resident(h.shape), _resident(p.shape), _layer(prev0_all, layer), _layer(prev1_all, layer)]
        + [_layer(w[k], layer) for k in _FFN_KEYS],
        out_specs=[_whole((b, d)), _whole((b, f2))],
        out_shape=[jax.ShapeDtypeStruct((b, d), F32), jax.ShapeDtypeStruct((b, f2), F32)],
        scratch_shapes=[pltpu.VMEM((b, d), F32)],
        compiler_params=_params("arbitrary"),
        name="ffn_sample",
    )(h, p, prev0_all, prev1_all, *[w[k] for k in _FFN_KEYS])


_ML_KEYS = ("g_pre_mix", "g_post_mix", "ml_win", "ml_wgc", "ml_bgc", "ml_wgr", "ml_bgr", "ml_wout")


def _mlstm_prompt_kernel(x_ref, gpre_ref, gpost_ref, win_ref, wgc_ref, bgc_ref, wgr_ref, bgr_ref, wout_ref,
                         out_ref, c_out_ref, n_out_ref, m_out_ref, c_ref, n_ref, m_ref):
    t = pl.program_id(1)
    L = ML_CHUNK_ROWS
    nchunk = x_ref.shape[0] // L
    heads = range(ML_HEADS)

    @pl.when(t == 0)
    def _():
        c_ref[...] = jnp.zeros_like(c_ref)
        n_ref[...] = jnp.zeros_like(n_ref)
        m_ref[...] = jnp.zeros_like(m_ref)

    causal = lax.broadcasted_iota(jnp.int32, (L, L), 0) >= lax.broadcasted_iota(jnp.int32, (L, L), 1)

    def project_qk(i):
        r = slice(i * L, (i + 1) * L)
        xn = _rms(x_ref[r, :], gpre_ref[...]).astype(BF16)
        gc = _dot(xn, wgc_ref[...]) + bgc_ref[...]
        gr = _dot_nt(wgr_ref[...], xn) + bgr_ref[:, 0:1]
        proj = _dot(xn, win_ref[:, :2 * ML_HK])
        is_forget_r = lax.broadcasted_iota(jnp.int32, gr.shape, 0) >= ML_HEADS
        bc = _cumsum(_log_sigmoid(gc), 0)
        br = _cumsum(jnp.where(is_forget_r, _log_sigmoid(gr), 0.0), 1)
        q = [proj[:, h * ML_DK:(h + 1) * ML_DK] * (ML_DK ** -0.5) for h in heads]
        k = [proj[:, ML_HK + h * ML_DK:ML_HK + (h + 1) * ML_DK] for h in heads]
        return dict(xn=xn, gc=gc, gr=gr, bc=bc, br=br, q=q, k=k)

    def head_scores(p):
        qb = [a.astype(BF16) for a in p["q"]]
        qk = [_dot_nt(qb[h], p["k"][h].astype(BF16)) for h in heads]
        qc = [_dot(qb[h], c_ref[h].astype(BF16)) for h in heads]
        return qk, qc

    def project_vo(p):
        proj = _dot(p["xn"], win_ref[:, 2 * ML_HK:])
        return [proj[:, h * ML_DV:(h + 1) * ML_DV].astype(BF16) for h in heads], proj[:, ML_HEADS * ML_DV:]

    def gating(p, qk):
        out = []
        for h in heads:
            ig_r = p["gr"][h:h + 1, :]
            b_r = p["br"][ML_HEADS + h:ML_HEADS + h + 1, :]
            ig_c = p["gc"][:, h:h + 1]
            b_c = p["bc"][:, ML_HEADS + h:ML_HEADS + h + 1]
            m_prev = m_ref[h:h + 1, 0:1]
            n_prev = n_ref[h:h + 1, :]

            d = jnp.where(causal, b_c + (ig_r - b_r), -jnp.inf)
            a = b_c + m_prev
            m_t = jnp.maximum(a, jnp.max(d, axis=1, keepdims=True))
            w_inter = jnp.exp(a - m_t)
            s = qk[h] * jnp.exp(d - m_t)
            qn = w_inter * jnp.sum(p["q"][h] * n_prev, axis=1, keepdims=True) + jnp.sum(s, axis=1, keepdims=True)
            inv_den = 1.0 / jnp.maximum(jnp.abs(qn), jnp.exp(-m_t))

            b_last = b_r[:, L - 1:L]
            g_r = b_last - b_r + ig_r
            g_c = b_last - b_c + ig_c
            m_new = jnp.maximum(b_last + m_prev, jnp.max(g_r, axis=1, keepdims=True))
            w_old = jnp.exp(b_last + m_prev - m_new)
            wk = jnp.exp(g_c - m_new) * p["k"][h]
            out.append(dict(s=s.astype(BF16), wk=wk.astype(BF16), w_inter=w_inter, inv_den=inv_den, w_old=w_old,
                            n_new=w_old * n_prev + jnp.sum(wk, axis=0, keepdims=True), m_new=m_new))
        return out

    def finish(i, hidden, o):
        y = _dot((_sigmoid(o) * hidden).astype(BF16), wout_ref[...])
        r = slice(i * L, (i + 1) * L)
        out_ref[r, :] = x_ref[r, :] + _rms(y, gpost_ref[...])

    p = project_qk(0)
    qk, qc = head_scores(p)
    vb, o = project_vo(p)
    for i in range(nchunk):
        p_next = project_qk(i + 1) if i + 1 < nchunk else None
        g = gating(p, qk)
        sv = [_dot(g[h]["s"], vb[h]) for h in heads]
        kv = [_dot_tn(g[h]["wk"], vb[h]) for h in heads]
        hidden = jnp.concatenate([(g[h]["w_inter"] * qc[h] + sv[h]) * g[h]["inv_den"] for h in heads], axis=1)
        for h in heads:
            c_ref[h] = g[h]["w_old"] * c_ref[h] + kv[h]
            n_ref[h:h + 1, :] = g[h]["n_new"]
            m_ref[h:h + 1, :] = jnp.broadcast_to(g[h]["m_new"], (1, LANES))
        o_done = o
        if p_next is not None:
            p = p_next
            qk, qc = head_scores(p)
            vb, o = project_vo(p)
        finish(i, hidden, o_done)

    @pl.when(t == pl.num_programs(1) - 1)
    def _():
        c_out_ref[0] = c_ref[...]
        n_out_ref[0] = n_ref[0:ML_HEADS, :]
        m_out_ref[0] = m_ref[...]


def _ml_layers(layer):
    return [2 * layer, 2 * layer] + [layer] * (len(_ML_KEYS) - 2)


def _mlstm_prompt(h, w, layer):
    b, t, d = h.shape
    L = ML_STEP_ROWS
    nt = t // L
    assert t % L == 0 and L % ML_CHUNK_ROWS == 0
    out, c, n, m = pl.pallas_call(
        _mlstm_prompt_kernel,
        grid=(b, nt),
        in_specs=[pl.BlockSpec((L, d), lambda i, j: (i * nt + j, 0))]
        + [_layer(w[k], l) for k, l in zip(_ML_KEYS, _ml_layers(layer))],
        out_specs=[pl.BlockSpec((L, d), lambda i, j: (i * nt + j, 0)),
                   pl.BlockSpec((1, ML_HEADS, ML_DK, ML_DV), lambda i, j: (i, 0, 0, 0)),
                   pl.BlockSpec((1, ML_HEADS, ML_DK), lambda i, j: (i, 0, 0)),
                   pl.BlockSpec((1, SUBLANES, LANES), lambda i, j: (i, 0, 0))],
        out_shape=[jax.ShapeDtypeStruct((b * t, d), F32),
                   jax.ShapeDtypeStruct((b, ML_HEADS, ML_DK, ML_DV), F32),
                   jax.ShapeDtypeStruct((b, ML_HEADS, ML_DK), F32),
                   jax.ShapeDtypeStruct((b, SUBLANES, LANES), F32)],
        scratch_shapes=[pltpu.VMEM((ML_HEADS, ML_DK, ML_DV), F32), pltpu.VMEM((SUBLANES, ML_DK), F32),
                        pltpu.VMEM((SUBLANES, LANES), F32)],
        compiler_params=_params("arbitrary", "arbitrary"),
        name="mlstm_prompt",
    )(h.reshape(b * t, d), *[w[k] for k in _ML_KEYS])
    return out.reshape(b, t, d), c, n, m[:, :ML_HEADS, 0]


def _mlstm_sample_pre_kernel(x_ref, n_ref, m_ref, gpre_ref, win_ref, wgc_ref, bgc_ref,
                             q_ref, kw_ref, v_ref, o_ref, wold_ref, scal_ref, nnew_ref):
    H = ML_HEADS
    xn = _rms(x_ref[...], gpre_ref[...]).astype(BF16)
    proj = _dot(xn, win_ref[...])
    gates = _dot(xn, wgc_ref[...]) + bgc_ref[...]
    v_ref[...] = proj[:, 2 * ML_HK:2 * ML_HK + H * ML_DV]
    o_ref[...] = proj[:, 2 * ML_HK + H * ML_DV:]
    scal_ref[...] = jnp.zeros_like(scal_ref)
    for h in range(H):
        q = proj[:, h * ML_DK:(h + 1) * ML_DK] * (ML_DK ** -0.5)
        k = proj[:, ML_HK + h * ML_DK:ML_HK + (h + 1) * ML_DK]
        n_prev = n_ref[:, h * ML_DK:(h + 1) * ML_DK]
        ig = gates[:, h:h + 1]
        lf = _log_sigmoid(gates[:, H + h:H + h + 1])
        a = lf + m_ref[:, h:h + 1]
        m_t = jnp.maximum(a, ig)
        w_old = jnp.exp(a - m_t)
        w_new = jnp.exp(ig - m_t)
        s = jnp.sum(q * k, axis=1, keepdims=True) * w_new
        qn = w_old * jnp.sum(q * n_prev, axis=1, keepdims=True) + s
        q_ref[:, h * ML_DK:(h + 1) * ML_DK] = q
        kw_ref[:, h * ML_DK:(h + 1) * ML_DK] = w_new * k
        nnew_ref[:, h * ML_DK:(h + 1) * ML_DK] = w_old * n_prev + w_new * k
        wold_ref[:, h * ML_DV:(h + 1) * ML_DV] = jnp.broadcast_to(w_old, (w_old.shape[0], ML_DV))
        scal_ref[:, h:h + 1] = w_old
        scal_ref[:, H + h:H + h + 1] = s
        scal_ref[:, 2 * H + h:2 * H + h + 1] = jnp.maximum(jnp.abs(qn), jnp.exp(-m_t))
        scal_ref[:, 3 * H + h:3 * H + h + 1] = m_t


def _mlstm_sample_state_kernel(c_ref, qt_ref, kt_ref, v_ref, wold_ref, *rest):
    cnew_ref, qc_ref = rest[-2:]
    for b in range(c_ref.shape[0]):
        for h in range(ML_HEADS):
            c = c_ref[b, h]
            qc = qt_ref[0, h * ML_DK:(h + 1) * ML_DK, b:b + 1]
            kc = kt_ref[0, h * ML_DK:(h + 1) * ML_DK, b:b + 1]
            vr = v_ref[b:b + 1, h * ML_DV:(h + 1) * ML_DV]
            wo = wold_ref[b:b + 1, h * ML_DV:(h + 1) * ML_DV]
            qc_ref[b:b + 1, h * ML_DV:(h + 1) * ML_DV] = jnp.sum(qc * c, axis=0, keepdims=True)
            cnew_ref[b, h] = c * wo + kc * vr


def _mlstm_sample_post_kernel(x_ref, qc_ref, v_ref, o_ref, scal_ref, gpost_ref, wout_ref, out_ref, h_ref):
    H = ML_HEADS
    for h in range(H):
        sl = slice(h * ML_DV, (h + 1) * ML_DV)
        num = scal_ref[:, h:h + 1] * qc_ref[:, sl] + scal_ref[:, H + h:H + h + 1] * v_ref[:, sl]
        h_ref[:, sl] = num / scal_ref[:, 2 * H + h:2 * H + h + 1]
    y = _dot((_sigmoid(o_ref[...]) * h_ref[...]).astype(BF16), wout_ref[...])
    out_ref[...] = x_ref[...] + _rms(y, gpost_ref[...])


def _aliased(stack_prev, args, in_specs):
    if stack_prev is None:
        return {}
    args.append(stack_prev)
    in_specs.append(pl.BlockSpec(memory_space=pl.ANY))
    return {len(args) - 1: 0}


def _mlstm_sample(h, c_all, n, m, w, layer, c_stack):
    b, d = h.shape
    H = ML_HEADS
    bb = SAMPLE_STATE_BATCH
    nb = b // bb
    assert b % bb == 0
    pre_in = [h, n.reshape(b, H * ML_DK), jnp.pad(m, ((0, 0), (0, LANES - H)))]
    pre_keys = ("g_pre_mix", "ml_win", "ml_wgc", "ml_bgc")
    pre_layers = (2 * layer, layer, layer, layer)
    pre_out = [(b, ML_HK), (b, ML_HK), (b, H * ML_DV), (b, H * ML_DV), (b, H * ML_DV), (b, LANES), (b, ML_HK)]
    q, kw, v, o, wold, scal, nnew = pl.pallas_call(
        _mlstm_sample_pre_kernel,
        grid=(1,),
        in_specs=[_resident(a.shape) for a in pre_in] + [_layer(w[k], l) for k, l in zip(pre_keys, pre_layers)],
        out_specs=[_whole(s) for s in pre_out],
        out_shape=[jax.ShapeDtypeStruct(s, F32) for s in pre_out],
        compiler_params=_params("arbitrary"),
        name="mlstm_sample_pre",
    )(*pre_in, *[w[k] for k in pre_keys])

    def cols(a):
        return a.reshape(nb, bb, ML_HK).transpose(0, 2, 1)

    state_args = [c_all, cols(q), cols(kw), v, wold]
    state_specs = [pl.BlockSpec((None, bb, H, ML_DK, ML_DV), lambda i: (layer, i, 0, 0, 0)),
                   pl.BlockSpec((1, ML_HK, bb), lambda i: (i, 0, 0)),
                   pl.BlockSpec((1, ML_HK, bb), lambda i: (i, 0, 0)),
                   pl.BlockSpec((bb, H * ML_DV), lambda i: (i, 0)),
                   pl.BlockSpec((bb, H * ML_DV), lambda i: (i, 0))]
    aliases = _aliased(c_stack, state_args, state_specs)
    c_stack, qc = pl.pallas_call(
        _mlstm_sample_state_kernel,
        grid=(nb,),
        in_specs=state_specs,
        out_specs=[pl.BlockSpec((None, bb, H, ML_DK, ML_DV), lambda i: (layer, i, 0, 0, 0)),
                   pl.BlockSpec((bb, H * ML_DV), lambda i: (i, 0))],
        out_shape=[jax.ShapeDtypeStruct(c_all.shape, F32), jax.ShapeDtypeStruct((b, H * ML_DV), F32)],
        input_output_aliases=aliases,
        compiler_params=_params("arbitrary"),
        name="mlstm_sample_state",
    )(*state_args)

    post_in = [h, qc, v, o, scal]
    out = pl.pallas_call(
        _mlstm_sample_post_kernel,
        grid=(1,),
        in_specs=[_resident(a.shape) for a in post_in]
        + [_layer(w["g_post_mix"], 2 * layer), _layer(w["ml_wout"], layer)],
        out_specs=_whole((b, d)),
        out_shape=jax.ShapeDtypeStruct((b, d), F32),
        scratch_shapes=[pltpu.VMEM((b, H * ML_DV), F32)],
        compiler_params=_params("arbitrary"),
        name="mlstm_sample_post",
    )(*post_in, w["g_post_mix"], w["ml_wout"])
    return out, c_stack, nnew.reshape(b, H, ML_DK), scal[:, 3 * H:4 * H]


def _rope_tables(pos):
    half = SW_HD // 2
    inv = ROPE_THETA ** (-jnp.arange(half, dtype=F32) / half)
    ang = pos[:, None] * inv[None, :]
    cos = jnp.tile(jnp.cos(ang), (1, LANES // half))
    sin = jnp.tile(jnp.concatenate([-jnp.sin(ang), jnp.sin(ang)], axis=1), (1, LANES // SW_HD))
    return cos, sin


def _first_half_lanes(rows):
    lane = lax.broadcasted_iota(jnp.int32, (rows, LANES), 1)
    return jnp.bitwise_and(lane, SW_HD - 1) < SW_HD // 2


def _rope_block(x, cos, sin, first_half):
    partner = jnp.where(first_half, pltpu.roll(x, LANES - SW_HD // 2, axis=1), pltpu.roll(x, SW_HD // 2, axis=1))
    return x * cos + partner * sin


def _pair_heads(a, axis):
    shp = a.shape
    a = a.reshape(shp[:axis] + (2, 2, SW_GROUP) + shp[axis + 1:])
    a = jnp.swapaxes(a, axis + 1, axis + 2)
    return a.reshape(shp)


_SW_KEYS = ("g_pre_mix", "g_post_mix", "sw_win", "sw_wout")


def _swa_prompt_kernel(sink_ref, x_ref, cos_ref, sin_ref, gpre_ref, gpost_ref, win_ref, wout_ref,
                       out_ref, kout_ref, vout_ref, k_ref, v_ref, kf_ref, vf_ref):
    t = pl.program_id(1)
    rows = x_ref.shape[0]
    nsub = rows // WINDOW
    W = WINDOW
    G = SW_GROUP
    K2 = 2 * W
    pairs = range(SW_KV_HEADS // 2)

    @pl.when(t == 0)
    def _():
        k_ref[0:W, :] = jnp.zeros((W, SW_NKV), BF16)
        v_ref[0:W, :] = jnp.zeros((W, SW_NKV), BF16)

    first_half = _first_half_lanes(SW_UNIT * W)
    qi = jnp.bitwise_and(lax.broadcasted_iota(jnp.int32, (G * W, K2), 0), W - 1)
    ki = lax.broadcasted_iota(jnp.int32, (G * W, K2), 1)
    band = (ki >= qi) & (ki <= qi + W)
    lane = lax.broadcasted_iota(jnp.int32, (1, LANES), 1)
    low = lane < SW_HD
    high = jnp.logical_not(low)
    zeros = jnp.zeros((1, LANES), BF16)
    ones_low = jnp.where(low, 1.0, 0.0).astype(BF16)
    ones_high = jnp.where(high, 1.0, 0.0).astype(BF16)
    member = lax.shift_right_logical(lax.broadcasted_iota(jnp.int32, (G * W, 1), 0),
                                     jnp.int32(W.bit_length() - 1))

    def sink_column(pair, half):
        col = jnp.full((G * W, 1), sink_ref[2 * (pair * G) + half], F32)
        for g in range(1, G):
            col = jnp.where(member == g, sink_ref[2 * (pair * G + g) + half], col)
        return col

    sinks = [[sink_column(pair, half) for half in range(2)] for pair in pairs]

    U = SW_UNIT

    def project(u):
        r = slice(u * U * W, (u + 1) * U * W)
        xn = _rms(x_ref[r, :], gpre_ref[...]).astype(BF16)
        proj = _dot(xn, win_ref[:, :SW_NQ + SW_NKV])
        v = _dot(xn, win_ref[:, SW_NQ + SW_NKV:])
        cos = cos_ref[r, :]
        sin = sin_ref[r, :]
        q = [(_rope_block(proj[:, c * LANES:(c + 1) * LANES], cos, sin, first_half)
              * (SW_HD ** -0.5 * LOG2E)).astype(BF16) for c in range(SW_NQ // LANES)]
        for c in range(SW_NKV // LANES):
            blk = _rope_block(proj[:, SW_NQ + c * LANES:SW_NQ + (c + 1) * LANES], cos, sin, first_half)
            kf_ref[r, c * LANES:(c + 1) * LANES] = blk
            k_ref[W + u * U * W:W + (u + 1) * U * W, c * LANES:(c + 1) * LANES] = blk.astype(BF16)
        vf_ref[r, :] = v
        v_ref[W + u * U * W:W + (u + 1) * U * W, :] = v.astype(BF16)
        return [[jnp.concatenate([a[j * W:(j + 1) * W] for a in q[pair * G:(pair + 1) * G]], axis=0)
                 for pair in pairs] for j in range(U)]

    def scores(i, q4):
        out = []
        for pair in pairs:
            k2 = k_ref[i * W:(i + 2) * W, pair * LANES:(pair + 1) * LANES]
            kz = jnp.concatenate([jnp.where(low, k2, zeros), jnp.where(high, k2, zeros)], axis=0)
            out.append(_dot_nt(q4[pair], kz))
        return out

    def softmax_numerators(i, s):
        first_key = jnp.where(t * nsub + i > 0, 0, W)
        mask = band & (ki >= first_key)
        out = []
        for pair in pairs:
            probs, sink_terms = [], []
            for half in range(2):
                sh = jnp.where(mask, s[pair][:, half * K2:(half + 1) * K2], -jnp.inf)
                sk = sinks[pair][half]
                mx = jnp.maximum(jnp.max(sh, axis=1, keepdims=True), sk)
                probs.append(jnp.exp2(sh - mx).astype(BF16))
                sink_terms.append(jnp.exp2(sk - mx))
            out.append((jnp.concatenate(probs, axis=1), jnp.where(low, sink_terms[0], sink_terms[1])))
        return out

    def weighted_values(i, soft):
        out = []
        for pair in pairs:
            v2 = v_ref[i * W:(i + 2) * W, pair * LANES:(pair + 1) * LANES]
            vz = jnp.concatenate(
                [jnp.concatenate([jnp.where(low, v2, zeros), jnp.broadcast_to(ones_low, v2.shape)], axis=1),
                 jnp.concatenate([jnp.where(high, v2, zeros), jnp.broadcast_to(ones_high, v2.shape)], axis=1)],
                axis=0)
            out.append((_dot(soft[pair][0], vz), soft[pair][1]))
        return out

    def finish(u, weighted):
        rows_u = []
        for j in range(U):
            blocks = []
            for o, sink_term in weighted[j]:
                both = (o[:, :LANES] / (o[:, LANES:] + sink_term)).astype(BF16)
                blocks += [both[g * W:(g + 1) * W] for g in range(G)]
            rows_u.append(jnp.concatenate(blocks, axis=1))
        y = _dot(jnp.concatenate(rows_u, axis=0), wout_ref[...])
        r = slice(u * U * W, (u + 1) * U * W)
        out_ref[r, :] = x_ref[r, :] + _rms(y, gpost_ref[...])

    nunit = nsub // U
    q_unit = project(0)
    s = scores(0, q_unit[0])
    weighted, done = [], None
    for i in range(nsub):
        u, j = divmod(i, U)
        q_next = project(u + 1) if j == U - 1 and u + 1 < nunit else None
        soft = softmax_numerators(i, s)
        if j == 0 and done is not None:
            finish(u - 1, done)
            done = None
        weighted.append(weighted_values(i, soft))
        if j == U - 1:
            done, weighted = weighted, []
            if q_next is not None:
                q_unit = q_next
        if i + 1 < nsub:
            s = scores(i + 1, q_unit[(i + 1) % U])
    finish(nunit - 1, done)

    k_ref[0:W, :] = k_ref[rows:rows + W, :]
    v_ref[0:W, :] = v_ref[rows:rows + W, :]

    @pl.when(t == pl.num_programs(1) - 1)
    def _():
        kout_ref[0] = kf_ref[rows - W:, :].T
        vout_ref[0] = vf_ref[rows - W:, :].T


def _swa_prompt(h, w, layer):
    b, t, d = h.shape
    rows = SW_ROWS
    nt = t // rows
    assert t % rows == 0 and rows % WINDOW == 0 and t >= WINDOW
    cos, sin = _rope_tables(jnp.arange(t, dtype=F32))
    layers = (2 * layer + 1, 2 * layer + 1, layer, layer)
    out, k, v = pl.pallas_call(
        _swa_prompt_kernel,
        grid_spec=pltpu.PrefetchScalarGridSpec(
            num_scalar_prefetch=1,
            grid=(b, nt),
            in_specs=[pl.BlockSpec((rows, d), lambda i, j, s: (i * nt + j, 0)),
                      pl.BlockSpec((rows, LANES), lambda i, j, s: (j, 0)),
                      pl.BlockSpec((rows, LANES), lambda i, j, s: (j, 0))]
            + [_layer(w[k_], l) for k_, l in zip(_SW_KEYS, layers)],
            out_specs=[pl.BlockSpec((rows, d), lambda i, j, s: (i * nt + j, 0)),
                       pl.BlockSpec((1, SW_NKV, WINDOW), lambda i, j, s: (i, 0, 0)),
                       pl.BlockSpec((1, SW_NKV, WINDOW), lambda i, j, s: (i, 0, 0))],
            scratch_shapes=[pltpu.VMEM((WINDOW + rows, SW_NKV), BF16), pltpu.VMEM((WINDOW + rows, SW_NKV), BF16),
                            pltpu.VMEM((rows, SW_NKV), F32), pltpu.VMEM((rows, SW_NKV), F32)]),
        out_shape=[jax.ShapeDtypeStruct((b * t, d), F32),
                   jax.ShapeDtypeStruct((b, SW_NKV, WINDOW), F32),
                   jax.ShapeDtypeStruct((b, SW_NKV, WINDOW), F32)],
        compiler_params=_params("arbitrary", "arbitrary"),
        name="swa_prompt",
    )(w["sw_sinks_log2"][layer], h.reshape(b * t, d), cos, sin, *[w[k_] for k_ in _SW_KEYS])
    return out.reshape(b, t, d), k, v


def _swa_sample_pre_kernel(x_ref, cos_ref, sin_ref, gpre_ref, win_ref, q_ref, k_ref, v_ref, kt_ref, vt_ref):
    rows = x_ref.shape[0]
    xn = _rms(x_ref[...], gpre_ref[...]).astype(BF16)
    proj = _dot(xn, win_ref[...])
    cos = cos_ref[...]
    sin = sin_ref[...]
    first_half = _first_half_lanes(rows)
    for c in range(SW_NQ // LANES):
        blk = _rope_block(proj[:, c * LANES:(c + 1) * LANES], cos, sin, first_half)
        q_ref[:, c * LANES:(c + 1) * LANES] = blk * (SW_HD ** -0.5)
    for c in range(SW_NKV // LANES):
        k_ref[:, c * LANES:(c + 1) * LANES] = _rope_block(
            proj[:, SW_NQ + c * LANES:SW_NQ + (c + 1) * LANES], cos, sin, first_half)
    v = proj[:, SW_NQ + SW_NKV:]
    v_ref[...] = v
    kt_ref[...] = k_ref[...].T
    vt_ref[...] = v.T


def _swa_sample_attn_kernel(q_ref, kc_ref, vc_ref, knew_ref, vnew_ref, kcol_ref, vcol_ref, sink_ref, *rest):
    o_ref, kout_ref, vout_ref = rest[-3:]
    nbuf = kc_ref.shape[-1]
    pos = lax.broadcasted_iota(jnp.int32, (SW_HEADS, SW_NKV), 0)
    grp = 2 * lax.shift_right_logical(pos, jnp.int32(3)) + jnp.bitwise_and(pos, 1)
    col_kv = lax.shift_right_logical(lax.broadcasted_iota(jnp.int32, (SW_HEADS, SW_NKV), 1),
                                     jnp.int32(SW_HD.bit_length() - 1))
    own = grp == col_kv
    sk = sink_ref[:, 0:1]
    lane = lax.broadcasted_iota(jnp.int32, (1, nbuf), 1)
    dist = nbuf - lane
    valid = (dist >= 0) & (dist <= WINDOW)
    newest = lane == nbuf - 1
    for b in range(q_ref.shape[0]):
        q = q_ref[b]
        kt = kc_ref[b].reshape(SW_NKV, nbuf)
        vt = vc_ref[b].reshape(SW_NKV, nbuf)
        s = jnp.where(valid, _dot(q.astype(BF16), kt.astype(BF16)), -jnp.inf)
        s_new = jnp.sum(q * knew_ref[b], axis=1, keepdims=True)
        mx = jnp.maximum(jnp.maximum(jnp.max(s, axis=1, keepdims=True), s_new), sk)
        p = jnp.exp(s - mx)
        p_new = jnp.exp(s_new - mx)
        den = jnp.sum(p, axis=1, keepdims=True) + p_new + jnp.exp(sk - mx)
        o = _dot_nt(p.astype(BF16), vt.astype(BF16)) + p_new * vnew_ref[b]
        o = jnp.where(own, o, 0.0)
        o = o[:, :LANES] + o[:, LANES:]
        o = o + pltpu.roll(o, SW_HD, axis=1)
        o_ref[b] = (o / den)[:, :SW_HD]
        kout_ref[b] = jnp.where(newest, kcol_ref[0, :, b:b + 1],
                                pltpu.roll(kt, nbuf - 1, axis=1)).reshape(SW_KV_HEADS, SW_HD, nbuf)
        vout_ref[b] = jnp.where(newest, vcol_ref[0, :, b:b + 1],
                                pltpu.roll(vt, nbuf - 1, axis=1)).reshape(SW_KV_HEADS, SW_HD, nbuf)


def _out_proj_kernel(x_ref, a_ref, gpost_ref, wout_ref, out_ref):
    y = _dot(a_ref[...].astype(BF16), wout_ref[...])
    out_ref[...] = x_ref[...] + _rms(y, gpost_ref[...])


def _swa_sample(h, kc_all, vc_all, w, layer, k_stack, v_stack):
    b, d = h.shape
    nbuf = kc_all.shape[-1]
    bb = SAMPLE_STATE_BATCH
    nb = b // bb
    assert b % bb == 0 and nbuf == LANES
    cos, sin = _rope_tables(jnp.full((1,), PAST_LEN, F32))
    pre_in = [h, cos, sin]
    pre_out = [(b, SW_NQ), (b, SW_NKV), (b, SW_NKV), (SW_NKV, b), (SW_NKV, b)]
    q, knew, vnew, kt, vt = pl.pallas_call(
        _swa_sample_pre_kernel,
        grid=(1,),
        in_specs=[_resident(a.shape) for a in pre_in]
        + [_layer(w["g_pre_mix"], 2 * layer + 1), _layer(w["sw_win"], layer)],
        out_specs=[_whole(s) for s in pre_out],
        out_shape=[jax.ShapeDtypeStruct(s, F32) for s in pre_out],
        compiler_params=_params("arbitrary"),
        name="swa_sample_pre",
    )(*pre_in, w["g_pre_mix"], w["sw_win"])

    def cols(a):
        return a.reshape(SW_NKV, nb, bb).transpose(1, 0, 2)

    cache_spec = pl.BlockSpec((None, bb, SW_KV_HEADS, SW_HD, nbuf), lambda i: (layer, i, 0, 0, 0))
    pos = jnp.arange(SW_HEADS)
    own_kv = (2 * (pos // 8) + pos % 2)[:, None] == jnp.arange(SW_KV_HEADS)[None, :]
    q_bd = jnp.where(own_kv[None, :, :, None], q.reshape(b, SW_HEADS, 1, SW_HD), 0.0).reshape(b, SW_HEADS, SW_NKV)
    args = [q_bd, kc_all, vc_all, knew.reshape(b, 1, SW_NKV), vnew.reshape(b, 1, SW_NKV),
            cols(kt), cols(vt), jnp.broadcast_to(w["sw_sinks"][layer][:, None], (SW_HEADS, LANES))]
    specs = [pl.BlockSpec((bb, SW_HEADS, SW_NKV), lambda i: (i, 0, 0)), cache_spec, cache_spec,
             pl.BlockSpec((bb, 1, SW_NKV), lambda i: (i, 0, 0)),
             pl.BlockSpec((bb, 1, SW_NKV), lambda i: (i, 0, 0)),
             pl.BlockSpec((1, SW_NKV, bb), lambda i: (i, 0, 0)),
             pl.BlockSpec((1, SW_NKV, bb), lambda i: (i, 0, 0)),
             _resident((SW_HEADS, LANES))]
    aliases = {}
    if k_stack is not None:
        args += [k_stack, v_stack]
        specs += [pl.BlockSpec(memory_space=pl.ANY), pl.BlockSpec(memory_space=pl.ANY)]
        aliases = {len(args) - 2: 1, len(args) - 1: 2}
    o, k_stack, v_stack = pl.pallas_call(
        _swa_sample_attn_kernel,
        grid=(nb,),
        in_specs=specs,
        out_specs=[pl.BlockSpec((bb, SW_HEADS, SW_HD), lambda i: (i, 0, 0)), cache_spec, cache_spec],
        out_shape=[jax.ShapeDtypeStruct((b, SW_HEADS, SW_HD), F32),
                   jax.ShapeDtypeStruct(kc_all.shape, F32), jax.ShapeDtypeStruct(vc_all.shape, F32)],
        input_output_aliases=aliases,
        compiler_params=_params("arbitrary"),
        name="swa_sample_attn",
    )(*args)

    post_in = [h, o.reshape(b, SW_NQ)]
    out = pl.pallas_call(
        _out_proj_kernel,
        grid=(1,),
        in_specs=[_resident(a.shape) for a in post_in]
        + [_layer(w["g_post_mix"], 2 * layer + 1), _layer(w["sw_wout"], layer)],
        out_specs=_whole((b, d)),
        out_shape=jax.ShapeDtypeStruct((b, d), F32),
        compiler_params=_params("arbitrary"),
        name="swa_sample_post",
    )(*post_in, w["g_post_mix"], w["sw_wout"])
    return out, k_stack, v_stack


def _prepare(g_pre_mix, g_post_mix, g_pre_ffn, g_post_ffn, g_ple, ml_w_in, ml_b_gate, ml_w_out, sw_w_in,
             sw_sinks, sw_w_out, ffn_w_up, ffn_conv_w, ffn_conv_b, ffn_w_down, ple_w_proj, ple_w_gate):
    gate_w = ml_w_in[:, :, ML_QKVO:]
    q_cols = _pair_heads(sw_w_in[:, :, :SW_NQ].reshape(sw_w_in.shape[:2] + (SW_HEADS, SW_HD)), 2)
    sw_win = jnp.concatenate([q_cols.reshape(sw_w_in.shape[:2] + (SW_NQ,)), sw_w_in[:, :, SW_NQ:]], axis=-1)
    sw_wout = _pair_heads(sw_w_out.reshape((sw_w_out.shape[0], SW_HEADS, SW_HD, sw_w_out.shape[-1])), 1)
    sinks = _pair_heads(sw_sinks, 1)
    return dict(
        g_pre_mix=g_pre_mix[:, None], g_post_mix=g_post_mix[:, None], g_pre_ffn=g_pre_ffn[:, None],
        g_post_ffn=g_post_ffn[:, None], g_ple=g_ple[:, None],
        ffn_wup=ffn_w_up.astype(BF16), ffn_cw=ffn_conv_w, ffn_cb=ffn_conv_b[:, None],
        ffn_wdown=ffn_w_down.astype(BF16), ple_wgate=ple_w_gate.astype(BF16), ple_wproj=ple_w_proj.astype(BF16),
        ml_win=ml_w_in[:, :, :ML_QKVO].astype(BF16),
        ml_wgc=jnp.pad(gate_w, ((0, 0), (0, 0), (0, LANES - 2 * ML_HEADS))).astype(BF16),
        ml_bgc=jnp.pad(ml_b_gate, ((0, 0), (0, LANES - 2 * ML_HEADS)))[:, None],
        ml_wgr=jnp.swapaxes(gate_w, 1, 2).astype(BF16),
        ml_bgr=jnp.broadcast_to(ml_b_gate[:, :, None], ml_b_gate.shape + (LANES,)),
        ml_wout=ml_w_out.astype(BF16),
        sw_win=sw_win.astype(BF16), sw_wout=sw_wout.reshape(sw_w_out.shape).astype(BF16),
        sw_sinks=sinks, sw_sinks_log2=sinks * LOG2E)


def kernel(x_prompt, x_sample, p_prompt, p_sample, state_mlstm_C, state_mlstm_n, state_mlstm_m, cache_swa_k, cache_swa_v, state_conv, g_pre_mix, g_post_mix, g_pre_ffn, g_post_ffn, g_ple, ml_w_in, ml_b_gate, ml_w_out, sw_w_in, sw_sinks, sw_w_out, ffn_w_up, ffn_conv_w, ffn_conv_b, ffn_w_down, ple_w_proj, ple_w_gate):
    depth = g_pre_mix.shape[0]
    w = _prepare(g_pre_mix, g_post_mix, g_pre_ffn, g_post_ffn, g_ple, ml_w_in, ml_b_gate, ml_w_out, sw_w_in,
                 sw_sinks, sw_w_out, ffn_w_up, ffn_conv_w, ffn_conv_b, ffn_w_down, ple_w_proj, ple_w_gate)
    kc_all = jnp.transpose(cache_swa_k, (0, 1, 3, 4, 2))
    vc_all = jnp.transpose(cache_swa_v, (0, 1, 3, 4, 2))
    hp = x_prompt
    hs = x_sample[:, 0, :]
    cp, np_, mp, ns, ms = [], [], [], [], []
    kp, vp = [], []
    convp, convs = [], []
    conv_prev0, conv_prev1 = state_conv[:, :, 0], state_conv[:, :, 1]
    c_stack = k_stack = v_stack = None
    for i in range(depth):
        j = i // 2
        if i % 2 == 0:
            hp, c1, n1, m1 = _mlstm_prompt(hp, w, j)
            hs, c_stack, n2, m2 = _mlstm_sample(hs, state_mlstm_C, state_mlstm_n[j], state_mlstm_m[j], w, j, c_stack)
            cp.append(c1); np_.append(n1); mp.append(m1)
            ns.append(n2); ms.append(m2)
        else:
            hp, k1, v1 = _swa_prompt(hp, w, j)
            hs, k_stack, v_stack = _swa_sample(hs, kc_all, vc_all, w, j, k_stack, v_stack)
            kp.append(k1); vp.append(v1)
        hp, cvp = _ffn_prompt(hp, p_prompt, w, i)
        hs, cvs = _ffn_sample(hs, p_sample[i][:, 0, :], conv_prev0, conv_prev1, w, i)
        convp.append(cvp); convs.append(cvs)

    def cache_layout(a):
        return jnp.transpose(a, (0, 1, 4, 2, 3))

    kv_shape = (len(kp), x_prompt.shape[0], SW_KV_HEADS, SW_HD, WINDOW)
    return (hp, hs[:, None, :],
            jnp.stack(cp), jnp.stack(np_), jnp.stack(mp),
            cache_layout(jnp.stack(kp).reshape(kv_shape)), cache_layout(jnp.stack(vp).reshape(kv_shape)),
            jnp.stack(convp),
            c_stack, jnp.stack(ns), jnp.stack(ms), cache_layout(k_stack), cache_layout(v_stack),
            jnp.stack([conv_prev1, jnp.stack(convs)], axis=2))
```

```python
import math

import jax
import jax.numpy as jnp
from jax import lax
from jax.experimental import pallas as pl
from jax.experimental.pallas import tpu as pltpu

F32 = jnp.float32
BF16 = jnp.bfloat16

EPS = 1e-6
PLE_DIM = 256
ML_HEADS = 4
ML_DK = 128
ML_DV = 256
ML_HK = ML_HEADS * ML_DK
ML_QKVO = 2 * ML_HK + 2 * ML_HEADS * ML_DV
SW_HEADS = 16
SW_KV_HEADS = 4
SW_HD = 64
SW_GROUP = SW_HEADS // SW_KV_HEADS
SW_NQ = SW_HEADS * SW_HD
SW_NKV = SW_KV_HEADS * SW_HD
WINDOW = 128
ROPE_THETA = 10000.0
PAST_LEN = 8192
LOG2E = math.log2(math.e)

LANES = 128
SUBLANES = 8
FFN_CHUNK = 256
FFN_DOWN_GROUP = 4
FFN_ROWS = 512
FFN_TILES = 2
ML_CHUNK_ROWS = 256
ML_STEP_ROWS = 1024
SW_ROWS = 1024
SW_UNIT = 2
SAMPLE_STATE_BATCH = 8
VMEM_LIMIT = 60 * 1024 * 1024

assert SW_KV_HEADS == 4 and SW_GROUP == 4 and 2 * SW_HD == LANES and WINDOW == LANES


def _params(*sem):
    return pltpu.CompilerParams(dimension_semantics=sem, vmem_limit_bytes=VMEM_LIMIT)


def _resident(shape):
    zeros = (0,) * len(shape)
    return pl.BlockSpec(shape, lambda *_: zeros, pipeline_mode=pl.Buffered(1))


def _layer(a, layer):
    zeros = (0,) * (a.ndim - 1)
    return pl.BlockSpec((None,) + a.shape[1:], lambda *_: (layer,) + zeros, pipeline_mode=pl.Buffered(1))


def _whole(shape):
    zeros = (0,) * len(shape)
    return pl.BlockSpec(shape, lambda *_: zeros)


def _rms(x, g):
    return x * lax.rsqrt(jnp.mean(x * x, axis=-1, keepdims=True) + EPS) * g


def _sigmoid(x):
    return 1.0 / (1.0 + jnp.exp(-x))


def _log_sigmoid(x):
    return jnp.minimum(x, 0.0) - jnp.log1p(jnp.exp(-jnp.abs(x)))


def _dot(a, b):
    return jnp.dot(a, b, preferred_element_type=F32)


def _dot_nt(a, b):
    return lax.dot_general(a, b, (((1,), (1,)), ((), ())), preferred_element_type=F32)


def _dot_tn(a, b):
    return lax.dot_general(a, b, (((0,), (0,)), ((), ())), preferred_element_type=F32)


def _cumsum(x, axis):
    n = x.shape[axis]
    idx = lax.broadcasted_iota(jnp.int32, x.shape, axis)
    s = 1
    while s < n:
        x = x + jnp.where(idx >= s, pltpu.roll(x, s, axis=axis), 0.0)
        s *= 2
    return x


def _ffn_tail(x, f, p, gpost_ref, gple_ref, wgate_ref, wproj_ref):
    x1 = x + _rms(f, gpost_ref[...])
    gate = _sigmoid(_dot(_rms(x1, gple_ref[...]).astype(BF16), wgate_ref[...]))
    return x1 + gate * _dot(p.astype(BF16), wproj_ref[...])


def _ffn_chunk_cols(j, hidden):
    return (slice(j * FFN_CHUNK, (j + 1) * FFN_CHUNK),
            slice(hidden + j * FFN_CHUNK, hidden + (j + 1) * FFN_CHUNK))


def _ffn_prompt_kernel(x_ref, p_ref, gpre_ref, gpost_ref, gple_ref, wup_ref, cw_ref, cb_ref, wdown_ref,
                       wgate_ref, wproj_ref, out_ref, conv_ref, carry_ref, xp_ref, pp_ref, acc_ref):
    t = pl.program_id(1)
    rows = FFN_ROWS
    ntile = x_ref.shape[0] // rows
    hidden = wdown_ref.shape[0]
    nch = hidden // FFN_CHUNK
    S = SUBLANES
    G = rows // S
    nx = xp_ref.shape[1]
    npl = pp_ref.shape[1]

    @pl.when(t == 0)
    def _():
        carry_ref[...] = jnp.zeros_like(carry_ref)

    def interleaved(i):
        return pl.ds(((S * i) % G) * S + (S * i) // G, S, stride=S)

    first = lax.broadcasted_iota(jnp.int32, (S, FFN_CHUNK), 0) == 0

    def load_tile(n):
        for i in range(G):
            src = slice(n * rows + S * i, n * rows + S * (i + 1))
            for c in range(nx):
                xp_ref[n, c, interleaved(i), :] = x_ref[src, c * LANES:(c + 1) * LANES]
            for c in range(npl):
                pp_ref[n, c, interleaved(i), :] = p_ref[src, c * LANES:(c + 1) * LANES]
        x = jnp.concatenate([xp_ref[n, c] for c in range(nx)], axis=1)
        p = jnp.concatenate([pp_ref[n, c] for c in range(npl)], axis=1)
        ple = _dot(p.astype(BF16), wproj_ref[...])
        return x, ple, _rms(x, gpre_ref[...]).astype(BF16)

    def conv(u, cols):
        prev = carry_ref[:, cols]
        back1 = jnp.where(first, prev[2 * S - 1:2 * S], pltpu.roll(u[rows - S:], 1, axis=0))
        back2 = jnp.where(first, prev[S - 1:S], pltpu.roll(u[rows - 2 * S:rows - S], 1, axis=0))
        u1 = jnp.concatenate([back1, u[:rows - S]], axis=0)
        u2 = jnp.concatenate([back2, back1, u[:rows - 2 * S]], axis=0)
        carry_ref[:, cols] = u[rows - 2 * S:]
        cw = cw_ref[:, cols]
        return cb_ref[:, cols] + u2 * cw[0:1] + u1 * cw[1:2] + u * cw[2:3]

    def up(xn, j):
        return [_dot(xn, wup_ref[:, cols]) for cols in _ffn_chunk_cols(j, hidden)]

    def gate_of(n, x):
        out = []
        for part in range(2):
            r = slice(part * (rows // 2), (part + 1) * (rows // 2))
            x1 = x[r] + _rms(acc_ref[n, r, :], gpost_ref[...])
            out.append((x1, _dot(_rms(x1, gple_ref[...]).astype(BF16), wgate_ref[...])))
        return out

    def store_tile(n, gated, ple):
        for part, (x1, gate) in enumerate(gated):
            r = slice(part * (rows // 2), (part + 1) * (rows // 2))
            res = x1 + _sigmoid(gate) * ple[r]
            for c in range(nx):
                xp_ref[n, c, r, :] = res[:, c * LANES:(c + 1) * LANES]
        for i in range(G):
            dst = slice(n * rows + S * i, n * rows + S * (i + 1))
            for c in range(nx):
                out_ref[dst, c * LANES:(c + 1) * LANES] = xp_ref[n, c, interleaved(i), :]

    tiles = [load_tile(n) for n in range(ntile)]
    tail = None
    gated = None
    for n in range(ntile):
        x, ple, xn = tiles[n]
        u = up(xn, 0)
        pending = []
        for j in range(nch):
            u_next = up(xn, j + 1) if j + 1 < nch else None
            if tail is not None and j == 2:
                gated = gate_of(tail[0], tail[1])
            if tail is not None and j == 5:
                store_tile(tail[0], gated, tail[2])
                tail = None
            gcols, vcols = _ffn_chunk_cols(j, hidden)
            cg = conv(u[0], gcols)
            pending.append((cg * _sigmoid(cg) * conv(u[1], vcols)).astype(BF16))
            if len(pending) == FFN_DOWN_GROUP or j == nch - 1:
                lo = (j + 1 - len(pending)) * FFN_CHUNK
                d = _dot(jnp.concatenate(pending, axis=1), wdown_ref[lo:(j + 1) * FFN_CHUNK, :])
                if lo == 0:
                    acc_ref[n] = d
                else:
                    acc_ref[n] += d
                pending = []
            u = u_next
        tail = (n, x, ple)
    store_tile(tail[0], gate_of(tail[0], tail[1]), tail[2])

    @pl.when(t == pl.num_programs(1) - 1)
    def _():
        conv_ref[0, 0:1, :] = carry_ref[S - 1:S, :]
        conv_ref[0, 1:2, :] = carry_ref[2 * S - 1:2 * S, :]


def _ffn_sample_kernel(x_ref, p_ref, prev0_ref, prev1_ref, gpre_ref, gpost_ref, gple_ref, wup_ref, cw_ref,
                       cb_ref, wdown_ref, wgate_ref, wproj_ref, out_ref, u_ref, acc_ref):
    hidden = wdown_ref.shape[0]
    x = x_ref[...]
    xn = _rms(x, gpre_ref[...]).astype(BF16)

    def conv(cols):
        u = _dot(xn, wup_ref[:, cols])
        u_ref[:, cols] = u
        cw = cw_ref[:, cols]
        return cb_ref[:, cols] + prev0_ref[:, cols] * cw[0:1] + prev1_ref[:, cols] * cw[1:2] + u * cw[2:3]

    for j in range(hidden // FFN_CHUNK):
        gcols, vcols = _ffn_chunk_cols(j, hidden)
        cg = conv(gcols)
        h = cg * _sigmoid(cg) * conv(vcols)
        d = _dot(h.astype(BF16), wdown_ref[j * FFN_CHUNK:(j + 1) * FFN_CHUNK, :])
        if j == 0:
            acc_ref[...] = d
        else:
            acc_ref[...] += d
    out_ref[...] = _ffn_tail(x, acc_ref[...], p_ref[...], gpost_ref, gple_ref, wgate_ref, wproj_ref)


_FFN_KEYS = ("g_pre_ffn", "g_post_ffn", "g_ple", "ffn_wup", "ffn_cw", "ffn_cb", "ffn_wdown", "ple_wgate",
             "ple_wproj")


def _ffn_prompt(h, p_all, w, layer):
    b, t, d = h.shape
    f2 = w["ffn_wup"].shape[-1]
    step = FFN_ROWS * FFN_TILES
    nt = t // step
    assert t % step == 0 and (FFN_ROWS // SUBLANES) % SUBLANES == 0
    out, conv = pl.pallas_call(
        _ffn_prompt_kernel,
        grid=(b, nt),
        in_specs=[pl.BlockSpec((step, d), lambda i, j: (i * nt + j, 0)),
                  pl.BlockSpec((step, PLE_DIM), lambda i, j: ((layer * b + i) * nt + j, 0))]
        + [_layer(w[k], layer) for k in _FFN_KEYS],
        out_specs=[pl.BlockSpec((step, d), lambda i, j: (i * nt + j, 0)),
                   pl.BlockSpec((1, 2, f2), lambda i, j: (i, 0, 0))],
        out_shape=[jax.ShapeDtypeStruct((b * t, d), F32),
                   jax.ShapeDtypeStruct((b, 2, f2), F32)],
        scratch_shapes=[pltpu.VMEM((2 * SUBLANES, f2), F32),
                        pltpu.VMEM((FFN_TILES, d // LANES, FFN_ROWS, LANES), F32),
                        pltpu.VMEM((FFN_TILES, PLE_DIM // LANES, FFN_ROWS, LANES), F32),
                        pltpu.VMEM((FFN_TILES, FFN_ROWS, d), F32)],
        compiler_params=_params("arbitrary", "arbitrary"),
        name="ffn_prompt",
    )(h.reshape(b * t, d), p_all.reshape(-1, PLE_DIM), *[w[k] for k in _FFN_KEYS])
    return out.reshape(b, t, d), conv


def _ffn_sample(h, p, prev0_all, prev1_all, w, layer):
    b, d = h.shape
    f2 = w["ffn_wup"].shape[-1]
    return pl.pallas_call(
        _ffn_sample_kernel,
        grid=(1,),
        in_specs=[_resident(h.shape), _resident(p.shape), _layer(prev0_all, layer), _layer(prev1_all, layer)]
        + [_layer(w[k], layer) for k in _FFN_KEYS],
        out_specs=[_whole((b, d)), _whole((b, f2))],
        out_shape=[jax.ShapeDtypeStruct((b, d), F32), jax.ShapeDtypeStruct((b, f2), F32)],
        scratch_shapes=[pltpu.VMEM((b, d), F32)],
        compiler_params=_params("arbitrary"),
        name="ffn_sample",
    )(h, p, prev0_all, prev1_all, *[w[k] for k in _FFN_KEYS])


_ML_KEYS = ("g_pre_mix", "g_post_mix", "ml_win", "ml_wgc", "ml_bgc", "ml_wgr", "ml_bgr", "ml_wout")


def _mlstm_prompt_kernel(x_ref, gpre_ref, gpost_ref, win_ref, wgc_ref, bgc_ref, wgr_ref, bgr_ref, wout_ref,
                         out_ref, c_out_ref, n_out_ref, m_out_ref, c_ref, n_ref, m_ref):
    t = pl.program_id(1)
    L = ML_CHUNK_ROWS
    nchunk = x_ref.shape[0] // L
    heads = range(ML_HEADS)

    @pl.when(t == 0)
    def _():
        c_ref[...] = jnp.zeros_like(c_ref)
        n_ref[...] = jnp.zeros_like(n_ref)
        m_ref[...] = jnp.zeros_like(m_ref)

    causal = lax.broadcasted_iota(jnp.int32, (L, L), 0) >= lax.broadcasted_iota(jnp.int32, (L, L), 1)

    def project_qk(i):
        r = slice(i * L, (i + 1) * L)
        xn = _rms(x_ref[r, :], gpre_ref[...]).astype(BF16)
        gc = _dot(xn, wgc_ref[...]) + bgc_ref[...]
        gr = _dot_nt(wgr_ref[...], xn) + bgr_ref[:, 0:1]
        proj = _dot(xn, win_ref[:, :2 * ML_HK])
        is_forget_r = lax.broadcasted_iota(jnp.int32, gr.shape, 0) >= ML_HEADS
        bc = _cumsum(_log_sigmoid(gc), 0)
        br = _cumsum(jnp.where(is_forget_r, _log_sigmoid(gr), 0.0), 1)
        q = [proj[:, h * ML_DK:(h + 1) * ML_DK] * (ML_DK ** -0.5) for h in heads]
        k = [proj[:, ML_HK + h * ML_DK:ML_HK + (h + 1) * ML_DK] for h in heads]
        return dict(xn=xn, gc=gc, gr=gr, bc=bc, br=br, q=q, k=k)

    def head_scores(p):
        qb = [a.astype(BF16) for a in p["q"]]
        qk = [_dot_nt(qb[h], p["k"][h].astype(BF16)) for h in heads]
        qc = [_dot(qb[h], c_ref[h].astype(BF16)) for h in heads]
        return qk, qc

    def project_vo(p):
        proj = _dot(p["xn"], win_ref[:, 2 * ML_HK:])
        return [proj[:, h * ML_DV:(h + 1) * ML_DV].astype(BF16) for h in heads], proj[:, ML_HEADS * ML_DV:]

    def gating(p, qk):
        out = []
        for h in heads:
            ig_r = p["gr"][h:h + 1, :]
            b_r = p["br"][ML_HEADS + h:ML_HEADS + h + 1, :]
            ig_c = p["gc"][:, h:h + 1]
            b_c = p["bc"][:, ML_HEADS + h:ML_HEADS + h + 1]
            m_prev = m_ref[h:h + 1, 0:1]
            n_prev = n_ref[h:h + 1, :]

            d = jnp.where(causal, b_c + (ig_r - b_r), -jnp.inf)
            a = b_c + m_prev
            m_t = jnp.maximum(a, jnp.max(d, axis=1, keepdims=True))
            w_inter = jnp.exp(a - m_t)
            s = qk[h] * jnp.exp(d - m_t)
            qn = w_inter * jnp.sum(p["q"][h] * n_prev, axis=1, keepdims=True) + jnp.sum(s, axis=1, keepdims=True)
            inv_den = 1.0 / jnp.maximum(jnp.abs(qn), jnp.exp(-m_t))

            b_last = b_r[:, L - 1:L]
            g_r = b_last - b_r + ig_r
            g_c = b_last - b_c + ig_c
            m_new = jnp.maximum(b_last + m_prev, jnp.max(g_r, axis=1, keepdims=True))
            w_old = jnp.exp(b_last + m_prev - m_new)
            wk = jnp.exp(g_c - m_new) * p["k"][h]
            out.append(dict(s=s.astype(BF16), wk=wk.astype(BF16), w_inter=w_inter, inv_den=inv_den, w_old=w_old,
                            n_new=w_old * n_prev + jnp.sum(wk, axis=0, keepdims=True), m_new=m_new))
        return out

    def finish(i, hidden, o):
        y = _dot((_sigmoid(o) * hidden).astype(BF16), wout_ref[...])
        r = slice(i * L, (i + 1) * L)
        out_ref[r, :] = x_ref[r, :] + _rms(y, gpost_ref[...])

    p = project_qk(0)
    qk, qc = head_scores(p)
    vb, o = project_vo(p)
    for i in range(nchunk):
        p_next = project_qk(i + 1) if i + 1 < nchunk else None
        g = gating(p, qk)
        sv = [_dot(g[h]["s"], vb[h]) for h in heads]
        kv = [_dot_tn(g[h]["wk"], vb[h]) for h in heads]
        hidden = jnp.concatenate([(g[h]["w_inter"] * qc[h] + sv[h]) * g[h]["inv_den"] for h in heads], axis=1)
        for h in heads:
            c_ref[h] = g[h]["w_old"] * c_ref[h] + kv[h]
            n_ref[h:h + 1, :] = g[h]["n_new"]
            m_ref[h:h + 1, :] = jnp.broadcast_to(g[h]["m_new"], (1, LANES))
        o_done = o
        if p_next is not None:
            p = p_next
            qk, qc = head_scores(p)
            vb, o = project_vo(p)
        finish(i, hidden, o_done)

    @pl.when(t == pl.num_programs(1) - 1)
    def _():
        c_out_ref[0] = c_ref[...]
        n_out_ref[0] = n_ref[0:ML_HEADS, :]
        m_out_ref[0] = m_ref[...]


def _ml_layers(layer):
    return [2 * layer, 2 * layer] + [layer] * (len(_ML_KEYS) - 2)


def _mlstm_prompt(h, w, layer):
    b, t, d = h.shape
    L = ML_STEP_ROWS
    nt = t // L
    assert t % L == 0 and L % ML_CHUNK_ROWS == 0
    out, c, n, m = pl.pallas_call(
        _mlstm_prompt_kernel,
        grid=(b, nt),
        in_specs=[pl.BlockSpec((L, d), lambda i, j: (i * nt + j, 0))]
        + [_layer(w[k], l) for k, l in zip(_ML_KEYS, _ml_layers(layer))],
        out_specs=[pl.BlockSpec((L, d), lambda i, j: (i * nt + j, 0)),
                   pl.BlockSpec((1, ML_HEADS, ML_DK, ML_DV), lambda i, j: (i, 0, 0, 0)),
                   pl.BlockSpec((1, ML_HEADS, ML_DK), lambda i, j: (i, 0, 0)),
                   pl.BlockSpec((1, SUBLANES, LANES), lambda i, j: (i, 0, 0))],
        out_shape=[jax.ShapeDtypeStruct((b * t, d), F32),
                   jax.ShapeDtypeStruct((b, ML_HEADS, ML_DK, ML_DV), F32),
                   jax.ShapeDtypeStruct((b, ML_HEADS, ML_DK), F32),
                   jax.ShapeDtypeStruct((b, SUBLANES, LANES), F32)],
        scratch_shapes=[pltpu.VMEM((ML_HEADS, ML_DK, ML_DV), F32), pltpu.VMEM((SUBLANES, ML_DK), F32),
                        pltpu.VMEM((SUBLANES, LANES), F32)],
        compiler_params=_params("arbitrary", "arbitrary"),
        name="mlstm_prompt",
    )(h.reshape(b * t, d), *[w[k] for k in _ML_KEYS])
    return out.reshape(b, t, d), c, n, m[:, :ML_HEADS, 0]


def _mlstm_sample_pre_kernel(x_ref, n_ref, m_ref, gpre_ref, win_ref, wgc_ref, bgc_ref,
                             q_ref, kw_ref, v_ref, o_ref, wold_ref, scal_ref, nnew_ref):
    H = ML_HEADS
    xn = _rms(x_ref[...], gpre_ref[...]).astype(BF16)
    proj = _dot(xn, win_ref[...])
    gates = _dot(xn, wgc_ref[...]) + bgc_ref[...]
    v_ref[...] = proj[:, 2 * ML_HK:2 * ML_HK + H * ML_DV]
    o_ref[...] = proj[:, 2 * ML_HK + H * ML_DV:]
    scal_ref[...] = jnp.zeros_like(scal_ref)
    for h in range(H):
        q = proj[:, h * ML_DK:(h + 1) * ML_DK] * (ML_DK ** -0.5)
        k = proj[:, ML_HK + h * ML_DK:ML_HK + (h + 1) * ML_DK]
        n_prev = n_ref[:, h * ML_DK:(h + 1) * ML_DK]
        ig = gates[:, h:h + 1]
        lf = _log_sigmoid(gates[:, H + h:H + h + 1])
        a = lf + m_ref[:, h:h + 1]
        m_t = jnp.maximum(a, ig)
        w_old = jnp.exp(a - m_t)
        w_new = jnp.exp(ig - m_t)
        s = jnp.sum(q * k, axis=1, keepdims=True) * w_new
        qn = w_old * jnp.sum(q * n_prev, axis=1, keepdims=True) + s
        q_ref[:, h * ML_DK:(h + 1) * ML_DK] = q
        kw_ref[:, h * ML_DK:(h + 1) * ML_DK] = w_new * k
        nnew_ref[:, h * ML_DK:(h + 1) * ML_DK] = w_old * n_prev + w_new * k
        wold_ref[:, h * ML_DV:(h + 1) * ML_DV] = jnp.broadcast_to(w_old, (w_old.shape[0], ML_DV))
        scal_ref[:, h:h + 1] = w_old
        scal_ref[:, H + h:H + h + 1] = s
        scal_ref[:, 2 * H + h:2 * H + h + 1] = jnp.maximum(jnp.abs(qn), jnp.exp(-m_t))
        scal_ref[:, 3 * H + h:3 * H + h + 1] = m_t


def _mlstm_sample_state_kernel(c_ref, qt_ref, kt_ref, v_ref, wold_ref, *rest):
    cnew_ref, qc_ref = rest[-2:]
    for b in range(c_ref.shape[0]):
        for h in range(ML_HEADS):
            c = c_ref[b, h]
            qc = qt_ref[0, h * ML_DK:(h + 1) * ML_DK, b:b + 1]
            kc = kt_ref[0, h * ML_DK:(h + 1) * ML_DK, b:b + 1]
            vr = v_ref[b:b + 1, h * ML_DV:(h + 1) * ML_DV]
            wo = wold_ref[b:b + 1, h * ML_DV:(h + 1) * ML_DV]
            qc_ref[b:b + 1, h * ML_DV:(h + 1) * ML_DV] = jnp.sum(qc * c, axis=0, keepdims=True)
            cnew_ref[b, h] = c * wo + kc * vr


def _mlstm_sample_post_kernel(x_ref, qc_ref, v_ref, o_ref, scal_ref, gpost_ref, wout_ref, out_ref, h_ref):
    H = ML_HEADS
    for h in range(H):
        sl = slice(h * ML_DV, (h + 1) * ML_DV)
        num = scal_ref[:, h:h + 1] * qc_ref[:, sl] + scal_ref[:, H + h:H + h + 1] * v_ref[:, sl]
        h_ref[:, sl] = num / scal_ref[:, 2 * H + h:2 * H + h + 1]
    y = _dot((_sigmoid(o_ref[...]) * h_ref[...]).astype(BF16), wout_ref[...])
    out_ref[...] = x_ref[...] + _rms(y, gpost_ref[...])


def _aliased(stack_prev, args, in_specs):
    if stack_prev is None:
        return {}
    args.append(stack_prev)
    in_specs.append(pl.BlockSpec(memory_space=pl.ANY))
    return {len(args) - 1: 0}


def _mlstm_sample(h, c_all, n, m, w, layer, c_stack):
    b, d = h.shape
    H = ML_HEADS
    bb = SAMPLE_STATE_BATCH
    nb = b // bb
    assert b % bb == 0
    pre_in = [h, n.reshape(b, H * ML_DK), jnp.pad(m, ((0, 0), (0, LANES - H)))]
    pre_keys = ("g_pre_mix", "ml_win", "ml_wgc", "ml_bgc")
    pre_layers = (2 * layer, layer, layer, layer)
    pre_out = [(b, ML_HK), (b, ML_HK), (b, H * ML_DV), (b, H * ML_DV), (b, H * ML_DV), (b, LANES), (b, ML_HK)]
    q, kw, v, o, wold, scal, nnew = pl.pallas_call(
        _mlstm_sample_pre_kernel,
        grid=(1,),
        in_specs=[_resident(a.shape) for a in pre_in] + [_layer(w[k], l) for k, l in zip(pre_keys, pre_layers)],
        out_specs=[_whole(s) for s in pre_out],
        out_shape=[jax.ShapeDtypeStruct(s, F32) for s in pre_out],
        compiler_params=_params("arbitrary"),
        name="mlstm_sample_pre",
    )(*pre_in, *[w[k] for k in pre_keys])

    def cols(a):
        return a.reshape(nb, bb, ML_HK).transpose(0, 2, 1)

    state_args = [c_all, cols(q), cols(kw), v, wold]
    state_specs = [pl.BlockSpec((None, bb, H, ML_DK, ML_DV), lambda i: (layer, i, 0, 0, 0)),
                   pl.BlockSpec((1, ML_HK, bb), lambda i: (i, 0, 0)),
                   pl.BlockSpec((1, ML_HK, bb), lambda i: (i, 0, 0)),
                   pl.BlockSpec((bb, H * ML_DV), lambda i: (i, 0)),
                   pl.BlockSpec((bb, H * ML_DV), lambda i: (i, 0))]
    aliases = _aliased(c_stack, state_args, state_specs)
    c_stack, qc = pl.pallas_call(
        _mlstm_sample_state_kernel,
        grid=(nb,),
        in_specs=state_specs,
        out_specs=[pl.BlockSpec((None, bb, H, ML_DK, ML_DV), lambda i: (layer, i, 0, 0, 0)),
                   pl.BlockSpec((bb, H * ML_DV), lambda i: (i, 0))],
        out_shape=[jax.ShapeDtypeStruct(c_all.shape, F32), jax.ShapeDtypeStruct((b, H * ML_DV), F32)],
        input_output_aliases=aliases,
        compiler_params=_params("arbitrary"),
        name="mlstm_sample_state",
    )(*state_args)

    post_in = [h, qc, v, o, scal]
    out = pl.pallas_call(
        _mlstm_sample_post_kernel,
        grid=(1,),
        in_specs=[_resident(a.shape) for a in post_in]
        + [_layer(w["g_post_mix"], 2 * layer), _layer(w["ml_wout"], layer)],
        out_specs=_whole((b, d)),
        out_shape=jax.ShapeDtypeStruct((b, d), F32),
        scratch_shapes=[pltpu.VMEM((b, H * ML_DV), F32)],
        compiler_params=_params("arbitrary"),
        name="mlstm_sample_post",
    )(*post_in, w["g_post_mix"], w["ml_wout"])
    return out, c_stack, nnew.reshape(b, H, ML_DK), scal[:, 3 * H:4 * H]


def _rope_tables(pos):
    half = SW_HD // 2
    inv = ROPE_THETA ** (-jnp.arange(half, dtype=F32) / half)
    ang = pos[:, None] * inv[None, :]
    cos = jnp.tile(jnp.cos(ang), (1, LANES // half))
    sin = jnp.tile(jnp.concatenate([-jnp.sin(ang), jnp.sin(ang)], axis=1), (1, LANES // SW_HD))
    return cos, sin


def _first_half_lanes(rows):
    lane = lax.broadcasted_iota(jnp.int32, (rows, LANES), 1)
    return jnp.bitwise_and(lane, SW_HD - 1) < SW_HD // 2


def _rope_block(x, cos, sin, first_half):
    partner = jnp.where(first_half, pltpu.roll(x, LANES - SW_HD // 2, axis=1), pltpu.roll(x, SW_HD // 2, axis=1))
    return x * cos + partner * sin


def _pair_heads(a, axis):
    shp = a.shape
    a = a.reshape(shp[:axis] + (2, 2, SW_GROUP) + shp[axis + 1:])
    a = jnp.swapaxes(a, axis + 1, axis + 2)
    return a.reshape(shp)


_SW_KEYS = ("g_pre_mix", "g_post_mix", "sw_win", "sw_wout")


def _swa_prompt_kernel(sink_ref, x_ref, cos_ref, sin_ref, gpre_ref, gpost_ref, win_ref, wout_ref,
                       out_ref, kout_ref, vout_ref, k_ref, v_ref, kf_ref, vf_ref):
    t = pl.program_id(1)
    rows = x_ref.shape[0]
    nsub = rows // WINDOW
    W = WINDOW
    G = SW_GROUP
    K2 = 2 * W
    pairs = range(SW_KV_HEADS // 2)

    @pl.when(t == 0)
    def _():
        k_ref[0:W, :] = jnp.zeros((W, SW_NKV), BF16)
        v_ref[0:W, :] = jnp.zeros((W, SW_NKV), BF16)

    first_half = _first_half_lanes(SW_UNIT * W)
    qi = jnp.bitwise_and(lax.broadcasted_iota(jnp.int32, (G * W, K2), 0), W - 1)
    ki = lax.broadcasted_iota(jnp.int32, (G * W, K2), 1)
    band = (ki >= qi) & (ki <= qi + W)
    lane = lax.broadcasted_iota(jnp.int32, (1, LANES), 1)
    low = lane < SW_HD
    high = jnp.logical_not(low)
    zeros = jnp.zeros((1, LANES), BF16)
    ones_low = jnp.where(low, 1.0, 0.0).astype(BF16)
    ones_high = jnp.where(high, 1.0, 0.0).astype(BF16)
    member = lax.shift_right_logical(lax.broadcasted_iota(jnp.int32, (G * W, 1), 0),
                                     jnp.int32(W.bit_length() - 1))

    def sink_column(pair, half):
        col = jnp.full((G * W, 1), sink_ref[2 * (pair * G) + half], F32)
        for g in range(1, G):
            col = jnp.where(member == g, sink_ref[2 * (pair * G + g) + half], col)
        return col

    sinks = [[sink_column(pair, half) for half in range(2)] for pair in pairs]

    U = SW_UNIT

    def project(u):
        r = slice(u * U * W, (u + 1) * U * W)
        xn = _rms(x_ref[r, :], gpre_ref[...]).astype(BF16)
        proj = _dot(xn, win_ref[:, :SW_NQ + SW_NKV])
        v = _dot(xn, win_ref[:, SW_NQ + SW_NKV:])
        cos = cos_ref[r, :]
        sin = sin_ref[r, :]
        q = [(_rope_block(proj[:, c * LANES:(c + 1) * LANES], cos, sin, first_half)
              * (SW_HD ** -0.5 * LOG2E)).astype(BF16) for c in range(SW_NQ // LANES)]
        for c in range(SW_NKV // LANES):
            blk = _rope_block(proj[:, SW_NQ + c * LANES:SW_NQ + (c + 1) * LANES], cos, sin, first_half)
            kf_ref[r, c * LANES:(c + 1) * LANES] = blk
            k_ref[W + u * U * W:W + (u + 1) * U * W, c * LANES:(c + 1) * LANES] = blk.astype(BF16)
        vf_ref[r, :] = v
        v_ref[W + u * U * W:W + (u + 1) * U * W, :] = v.astype(BF16)
        return [[jnp.concatenate([a[j * W:(j + 1) * W] for a in q[pair * G:(pair + 1) * G]], axis=0)
                 for pair in pairs] for j in range(U)]

    def scores(i, q4):
        out = []
        for pair in pairs:
            k2 = k_ref[i * W:(i + 2) * W, pair * LANES:(pair + 1) * LANES]
            kz = jnp.concatenate([jnp.where(low, k2, zeros), jnp.where(high, k2, zeros)], axis=0)
            out.append(_dot_nt(q4[pair], kz))
        return out

    def softmax_numerators(i, s):
        first_key = jnp.where(t * nsub + i > 0, 0, W)
        mask = band & (ki >= first_key)
        out = []
        for pair in pairs:
            probs, sink_terms = [], []
            for half in range(2):
                sh = jnp.where(mask, s[pair][:, half * K2:(half + 1) * K2], -jnp.inf)
                sk = sinks[pair][half]
                mx = jnp.maximum(jnp.max(sh, axis=1, keepdims=True), sk)
                probs.append(jnp.exp2(sh - mx).astype(BF16))
                sink_terms.append(jnp.exp2(sk - mx))
            out.append((jnp.concatenate(probs, axis=1), jnp.where(low, sink_terms[0], sink_terms[1])))
        return out

    def weighted_values(i, soft):
        out = []
        for pair in pairs:
            v2 = v_ref[i * W:(i + 2) * W, pair * LANES:(pair + 1) * LANES]
            vz = jnp.concatenate(
                [jnp.concatenate([jnp.where(low, v2, zeros), jnp.broadcast_to(ones_low, v2.shape)], axis=1),
                 jnp.concatenate([jnp.where(high, v2, zeros), jnp.broadcast_to(ones_high, v2.shape)], axis=1)],
                axis=0)
            out.append((_dot(soft[pair][0], vz), soft[pair][1]))
        return out

    def finish(u, weighted):
        rows_u = []
        for j in range(U):
            blocks = []
            for o, sink_term in weighted[j]:
                both = (o[:, :LANES] / (o[:, LANES:] + sink_term)).astype(BF16)
                blocks += [both[g * W:(g + 1) * W] for g in range(G)]
            rows_u.append(jnp.concatenate(blocks, axis=1))
        y = _dot(jnp.concatenate(rows_u, axis=0), wout_ref[...])
        r = slice(u * U * W, (u + 1) * U * W)
        out_ref[r, :] = x_ref[r, :] + _rms(y, gpost_ref[...])

    nunit = nsub // U
    q_unit = project(0)
    s = scores(0, q_unit[0])
    weighted, done = [], None
    for i in range(nsub):
        u, j = divmod(i, U)
        q_next = project(u + 1) if j == U - 1 and u + 1 < nunit else None
        soft = softmax_numerators(i, s)
        if j == 0 and done is not None:
            finish(u - 1, done)
            done = None
        weighted.append(weighted_values(i, soft))
        if j == U - 1:
            done, weighted = weighted, []
            if q_next is not None:
                q_unit = q_next
        if i + 1 < nsub:
            s = scores(i + 1, q_unit[(i + 1) % U])
    finish(nunit - 1, done)

    k_ref[0:W, :] = k_ref[rows:rows + W, :]
    v_ref[0:W, :] = v_ref[rows:rows + W, :]

    @pl.when(t == pl.num_programs(1) - 1)
    def _():
        kout_ref[0] = kf_ref[rows - W:, :].T
        vout_ref[0] = vf_ref[rows - W:, :].T


def _swa_prompt(h, w, layer):
    b, t, d = h.shape
    rows = SW_ROWS
    nt = t // rows
    assert t % rows == 0 and rows % WINDOW == 0 and t >= WINDOW
    cos, sin = _rope_tables(jnp.arange(t, dtype=F32))
    layers = (2 * layer + 1, 2 * layer + 1, layer, layer)
    out, k, v = pl.pallas_call(
        _swa_prompt_kernel,
        grid_spec=pltpu.PrefetchScalarGridSpec(
            num_scalar_prefetch=1,
            grid=(b, nt),
            in_specs=[pl.BlockSpec((rows, d), lambda i, j, s: (i * nt + j, 0)),
                      pl.BlockSpec((rows, LANES), lambda i, j, s: (j, 0)),
                      pl.BlockSpec((rows, LANES), lambda i, j, s: (j, 0))]
            + [_layer(w[k_], l) for k_, l in zip(_SW_KEYS, layers)],
            out_specs=[pl.BlockSpec((rows, d), lambda i, j, s: (i * nt + j, 0)),
                       pl.BlockSpec((1, SW_NKV, WINDOW), lambda i, j, s: (i, 0, 0)),
                       pl.BlockSpec((1, SW_NKV, WINDOW), lambda i, j, s: (i, 0, 0))],
            scratch_shapes=[pltpu.VMEM((WINDOW + rows, SW_NKV), BF16), pltpu.VMEM((WINDOW + rows, SW_NKV), BF16),
                            pltpu.VMEM((rows, SW_NKV), F32), pltpu.VMEM((rows, SW_NKV), F32)]),
        out_shape=[jax.ShapeDtypeStruct((b * t, d), F32),
                   jax.ShapeDtypeStruct((b, SW_NKV, WINDOW), F32),
                   jax.ShapeDtypeStruct((b, SW_NKV, WINDOW), F32)],
        compiler_params=_params("arbitrary", "arbitrary"),
        name="swa_prompt",
    )(w["sw_sinks_log2"][layer], h.reshape(b * t, d), cos, sin, *[w[k_] for k_ in _SW_KEYS])
    return out.reshape(b, t, d), k, v


def _swa_sample_pre_kernel(x_ref, cos_ref, sin_ref, gpre_ref, win_ref, q_ref, k_ref, v_ref, kt_ref, vt_ref):
    rows = x_ref.shape[0]
    xn = _rms(x_ref[...], gpre_ref[...]).astype(BF16)
    proj = _dot(xn, win_ref[...])
    cos = cos_ref[...]
    sin = sin_ref[...]
    first_half = _first_half_lanes(rows)
    for c in range(SW_NQ // LANES):
        blk = _rope_block(proj[:, c * LANES:(c + 1) * LANES], cos, sin, first_half)
        q_ref[:, c * LANES:(c + 1) * LANES] = blk * (SW_HD ** -0.5)
    for c in range(SW_NKV // LANES):
        k_ref[:, c * LANES:(c + 1) * LANES] = _rope_block(
            proj[:, SW_NQ + c * LANES:SW_NQ + (c + 1) * LANES], cos, sin, first_half)
    v = proj[:, SW_NQ + SW_NKV:]
    v_ref[...] = v
    kt_ref[...] = k_ref[...].T
    vt_ref[...] = v.T


def _swa_sample_attn_kernel(q_ref, kc_ref, vc_ref, knew_ref, vnew_ref, kcol_ref, vcol_ref, sink_ref, *rest):
    o_ref, kout_ref, vout_ref = rest[-3:]
    nbuf = kc_ref.shape[-1]
    pos = lax.broadcasted_iota(jnp.int32, (SW_HEADS, SW_NKV), 0)
    grp = 2 * lax.shift_right_logical(pos, jnp.int32(3)) + jnp.bitwise_and(pos, 1)
    col_kv = lax.shift_right_logical(lax.broadcasted_iota(jnp.int32, (SW_HEADS, SW_NKV), 1),
                                     jnp.int32(SW_HD.bit_length() - 1))
    own = grp == col_kv
    sk = sink_ref[:, 0:1]
    lane = lax.broadcasted_iota(jnp.int32, (1, nbuf), 1)
    dist = nbuf - lane
    valid = (dist >= 0) & (dist <= WINDOW)
    newest = lane == nbuf - 1
    for b in range(q_ref.shape[0]):
        q = q_ref[b]
        kt = kc_ref[b].reshape(SW_NKV, nbuf)
        vt = vc_ref[b].reshape(SW_NKV, nbuf)
        s = jnp.where(valid, _dot(q.astype(BF16), kt.astype(BF16)), -jnp.inf)
        s_new = jnp.sum(q * knew_ref[b], axis=1, keepdims=True)
        mx = jnp.maximum(jnp.maximum(jnp.max(s, axis=1, keepdims=True), s_new), sk)
        p = jnp.exp(s - mx)
        p_new = jnp.exp(s_new - mx)
        den = jnp.sum(p, axis=1, keepdims=True) + p_new + jnp.exp(sk - mx)
        o = _dot_nt(p.astype(BF16), vt.astype(BF16)) + p_new * vnew_ref[b]
        o = jnp.where(own, o, 0.0)
        o = o[:, :LANES] + o[:, LANES:]
        o = o + pltpu.roll(o, SW_HD, axis=1)
        o_ref[b] = (o / den)[:, :SW_HD]
        kout_ref[b] = jnp.where(newest, kcol_ref[0, :, b:b + 1],
                                pltpu.roll(kt, nbuf - 1, axis=1)).reshape(SW_KV_HEADS, SW_HD, nbuf)
        vout_ref[b] = jnp.where(newest, vcol_ref[0, :, b:b + 1],
                                pltpu.roll(vt, nbuf - 1, axis=1)).reshape(SW_KV_HEADS, SW_HD, nbuf)


def _out_proj_kernel(x_ref, a_ref, gpost_ref, wout_ref, out_ref):
    y = _dot(a_ref[...].astype(BF16), wout_ref[...])
    out_ref[...] = x_ref[...] + _rms(y, gpost_ref[...])


def _swa_sample(h, kc_all, vc_all, w, layer, k_stack, v_stack):
    b, d = h.shape
    nbuf = kc_all.shape[-1]
    bb = SAMPLE_STATE_BATCH
    nb = b // bb
    assert b % bb == 0 and nbuf == LANES
    cos, sin = _rope_tables(jnp.full((1,), PAST_LEN, F32))
    pre_in = [h, cos, sin]
    pre_out = [(b, SW_NQ), (b, SW_NKV), (b, SW_NKV), (SW_NKV, b), (SW_NKV, b)]
    q, knew, vnew, kt, vt = pl.pallas_call(
        _swa_sample_pre_kernel,
        grid=(1,),
        in_specs=[_resident(a.shape) for a in pre_in]
        + [_layer(w["g_pre_mix"], 2 * layer + 1), _layer(w["sw_win"], layer)],
        out_specs=[_whole(s) for s in pre_out],
        out_shape=[jax.ShapeDtypeStruct(s, F32) for s in pre_out],
        compiler_params=_params("arbitrary"),
        name="swa_sample_pre",
    )(*pre_in, w["g_pre_mix"], w["sw_win"])

    def cols(a):
        return a.reshape(SW_NKV, nb, bb).transpose(1, 0, 2)

    cache_spec = pl.BlockSpec((None, bb, SW_KV_HEADS, SW_HD, nbuf), lambda i: (layer, i, 0, 0, 0))
    pos = jnp.arange(SW_HEADS)
    own_kv = (2 * (pos // 8) + pos % 2)[:, None] == jnp.arange(SW_KV_HEADS)[None, :]
    q_bd = jnp.where(own_kv[None, :, :, None], q.reshape(b, SW_HEADS, 1, SW_HD), 0.0).reshape(b, SW_HEADS, SW_NKV)
    args = [q_bd, kc_all, vc_all, knew.reshape(b, 1, SW_NKV), vnew.reshape(b, 1, SW_NKV),
            cols(kt), cols(vt), jnp.broadcast_to(w["sw_sinks"][layer][:, None], (SW_HEADS, LANES))]
    specs = [pl.BlockSpec((bb, SW_HEADS, SW_NKV), lambda i: (i, 0, 0)), cache_spec, cache_spec,
             pl.BlockSpec((bb, 1, SW_NKV), lambda i: (i, 0, 0)),
             pl.BlockSpec((bb, 1, SW_NKV), lambda i: (i, 0, 0)),
             pl.BlockSpec((1, SW_NKV, bb), lambda i: (i, 0, 0)),
             pl.BlockSpec((1, SW_NKV, bb), lambda i: (i, 0, 0)),
             _resident((SW_HEADS, LANES))]
    aliases = {}
    if k_stack is not None:
        args += [k_stack, v_stack]
        specs += [pl.BlockSpec(memory_space=pl.ANY), pl.BlockSpec(memory_space=pl.ANY)]
        aliases = {len(args) - 2: 1, len(args) - 1: 2}
    o, k_stack, v_stack = pl.pallas_call(
        _swa_sample_attn_kernel,
        grid=(nb,),
        in_specs=specs,
        out_specs=[pl.BlockSpec((bb, SW_HEADS, SW_HD), lambda i: (i, 0, 0)), cache_spec, cache_spec],
        out_shape=[jax.ShapeDtypeStruct((b, SW_HEADS, SW_HD), F32),
                   jax.ShapeDtypeStruct(kc_all.shape, F32), jax.ShapeDtypeStruct(vc_all.shape, F32)],
        input_output_aliases=aliases,
        compiler_params=_params("arbitrary"),
        name="swa_sample_attn",
    )(*args)

    post_in = [h, o.reshape(b, SW_NQ)]
    out = pl.pallas_call(
        _out_proj_kernel,
        grid=(1,),
        in_specs=[_resident(a.shape) for a in post_in]
        + [_layer(w["g_post_mix"], 2 * layer + 1), _layer(w["sw_wout"], layer)],
        out_specs=_whole((b, d)),
        out_shape=jax.ShapeDtypeStruct((b, d), F32),
        compiler_params=_params("arbitrary"),
        name="swa_sample_post",
    )(*post_in, w["g_post_mix"], w["sw_wout"])
    return out, k_stack, v_stack


def _transpose_cast_kernel(wt_ref, out_ref):
    out_ref[...] = wt_ref[...].T.astype(BF16)


def _transpose_cast(wt, rows):
    layers, _, d = wt.shape
    blk = 4 * LANES
    assert rows % blk == 0
    return pl.pallas_call(
        _transpose_cast_kernel,
        grid=(layers, rows // blk),
        in_specs=[pl.BlockSpec((None, blk, d), lambda l, j: (l, j, 0))],
        out_specs=pl.BlockSpec((None, d, blk), lambda l, j: (l, 0, j)),
        out_shape=jax.ShapeDtypeStruct((layers, d, rows), BF16),
        compiler_params=_params("arbitrary", "arbitrary"),
        name="transpose_cast",
    )(wt)


def _prepare(g_pre_mix, g_post_mix, g_pre_ffn, g_post_ffn, g_ple, ml_w_in, ml_b_gate, ml_w_out, sw_w_in,
             sw_sinks, sw_w_out, ffn_w_up, ffn_conv_w, ffn_conv_b, ffn_w_down, ple_w_proj, ple_w_gate):
    gate_w = ml_w_in[:, :, ML_QKVO:]
    q_cols = _pair_heads(sw_w_in[:, :, :SW_NQ].reshape(sw_w_in.shape[:2] + (SW_HEADS, SW_HD)), 2)
    sw_win = jnp.concatenate([q_cols.reshape(sw_w_in.shape[:2] + (SW_NQ,)), sw_w_in[:, :, SW_NQ:]], axis=-1)
    sw_wout = _pair_heads(sw_w_out.reshape((sw_w_out.shape[0], SW_HEADS, SW_HD, sw_w_out.shape[-1])), 1)
    sinks = _pair_heads(sw_sinks, 1)
    return dict(
        g_pre_mix=g_pre_mix[:, None], g_post_mix=g_post_mix[:, None], g_pre_ffn=g_pre_ffn[:, None],
        g_post_ffn=g_post_ffn[:, None], g_ple=g_ple[:, None],
        ffn_wup=ffn_w_up.astype(BF16), ffn_cw=ffn_conv_w, ffn_cb=ffn_conv_b[:, None],
        ffn_wdown=ffn_w_down.astype(BF16), ple_wgate=ple_w_gate.astype(BF16), ple_wproj=ple_w_proj.astype(BF16),
        ml_win=_transpose_cast(jnp.swapaxes(ml_w_in, 1, 2), ML_QKVO),
        ml_wgc=jnp.pad(gate_w, ((0, 0), (0, 0), (0, LANES - 2 * ML_HEADS))).astype(BF16),
        ml_bgc=jnp.pad(ml_b_gate, ((0, 0), (0, LANES - 2 * ML_HEADS)))[:, None],
        ml_wgr=jnp.swapaxes(gate_w, 1, 2).astype(BF16),
        ml_bgr=jnp.broadcast_to(ml_b_gate[:, :, None], ml_b_gate.shape + (LANES,)),
        ml_wout=ml_w_out.astype(BF16),
        sw_win=sw_win.astype(BF16), sw_wout=sw_wout.reshape(sw_w_out.shape).astype(BF16),
        sw_sinks=sinks, sw_sinks_log2=sinks * LOG2E)


def kernel(x_prompt, x_sample, p_prompt, p_sample, state_mlstm_C, state_mlstm_n, state_mlstm_m, cache_swa_k, cache_swa_v, state_conv, g_pre_mix, g_post_mix, g_pre_ffn, g_post_ffn, g_ple, ml_w_in, ml_b_gate, ml_w_out, sw_w_in, sw_sinks, sw_w_out, ffn_w_up, ffn_conv_w, ffn_conv_b, ffn_w_down, ple_w_proj, ple_w_gate):
    depth = g_pre_mix.shape[0]
    w = _prepare(g_pre_mix, g_post_mix, g_pre_ffn, g_post_ffn, g_ple, ml_w_in, ml_b_gate, ml_w_out, sw_w_in,
                 sw_sinks, sw_w_out, ffn_w_up, ffn_conv_w, ffn_conv_b, ffn_w_down, ple_w_proj, ple_w_gate)
    kc_all = jnp.transpose(cache_swa_k, (0, 1, 3, 4, 2))
    vc_all = jnp.transpose(cache_swa_v, (0, 1, 3, 4, 2))
    hp = x_prompt
    hs = x_sample[:, 0, :]
    cp, np_, mp, ns, ms = [], [], [], [], []
    kp, vp = [], []
    convp, convs = [], []
    conv_prev0, conv_prev1 = state_conv[:, :, 0], state_conv[:, :, 1]
    c_stack = k_stack = v_stack = None
    for i in range(depth):
        j = i // 2
        if i % 2 == 0:
            hp, c1, n1, m1 = _mlstm_prompt(hp, w, j)
            hs, c_stack, n2, m2 = _mlstm_sample(hs, state_mlstm_C, state_mlstm_n[j], state_mlstm_m[j], w, j, c_stack)
            cp.append(c1); np_.append(n1); mp.append(m1)
            ns.append(n2); ms.append(m2)
        else:
            hp, k1, v1 = _swa_prompt(hp, w, j)
            hs, k_stack, v_stack = _swa_sample(hs, kc_all, vc_all, w, j, k_stack, v_stack)
            kp.append(k1); vp.append(v1)
        hp, cvp = _ffn_prompt(hp, p_prompt, w, i)
        hs, cvs = _ffn_sample(hs, p_sample[i][:, 0, :], conv_prev0, conv_prev1, w, i)
        convp.append(cvp); convs.append(cvs)

    def cache_layout(a):
        return jnp.transpose(a, (0, 1, 4, 2, 3))

    kv_shape = (len(kp), x_prompt.shape[0], SW_KV_HEADS, SW_HD, WINDOW)
    return (hp, hs[:, None, :],
            jnp.stack(cp), jnp.stack(np_), jnp.stack(mp),
            cache_layout(jnp.stack(kp).reshape(kv_shape)), cache_layout(jnp.stack(vp).reshape(kv_shape)),
            jnp.stack(convp),
            c_stack, jnp.stack(ns), jnp.stack(ms), cache_layout(k_stack), cache_layout(v_stack),
            jnp.stack([conv_prev1, jnp.stack(convs)], axis=2))
```

```python
import math

import jax
import jax.numpy as jnp
from jax import lax
from jax.experimental import pallas as pl
from jax.experimental.pallas import tpu as pltpu

F32 = jnp.float32
BF16 = jnp.bfloat16

EPS = 1e-6
PLE_DIM = 256
ML_HEADS = 4
ML_DK = 128
ML_DV = 256
ML_HK = ML_HEADS * ML_DK
ML_QKVO = 2 * ML_HK + 2 * ML_HEADS * ML_DV
SW_HEADS = 16
SW_KV_HEADS = 4
SW_HD = 64
SW_GROUP = SW_HEADS // SW_KV_HEADS
SW_NQ = SW_HEADS * SW_HD
SW_NKV = SW_KV_HEADS * SW_HD
WINDOW = 128
ROPE_THETA = 10000.0
PAST_LEN = 8192
LOG2E = math.log2(math.e)

LANES = 128
SUBLANES = 8
FFN_CHUNK = 256
FFN_DOWN_GROUP = 4
FFN_ROWS = 512
FFN_TILES = 2
ML_CHUNK_ROWS = 256
ML_STEP_ROWS = 1024
SW_ROWS = 1024
SW_UNIT = 2
SAMPLE_STATE_BATCH = 8
VMEM_LIMIT = 60 * 1024 * 1024

assert SW_KV_HEADS == 4 and SW_GROUP == 4 and 2 * SW_HD == LANES and WINDOW == LANES


def _params(*sem):
    return pltpu.CompilerParams(dimension_semantics=sem, vmem_limit_bytes=VMEM_LIMIT)


def _resident(shape):
    zeros = (0,) * len(shape)
    return pl.BlockSpec(shape, lambda *_: zeros, pipeline_mode=pl.Buffered(1))


def _layer(a, layer):
    zeros = (0,) * (a.ndim - 1)
    return pl.BlockSpec((None,) + a.shape[1:], lambda *_: (layer,) + zeros, pipeline_mode=pl.Buffered(1))


def _whole(shape):
    zeros = (0,) * len(shape)
    return pl.BlockSpec(shape, lambda *_: zeros)


def _rms(x, g):
    return x * lax.rsqrt(jnp.mean(x * x, axis=-1, keepdims=True) + EPS) * g


def _sigmoid(x):
    return 1.0 / (1.0 + jnp.exp(-x))


def _log_sigmoid(x):
    return jnp.minimum(x, 0.0) - jnp.log1p(jnp.exp(-jnp.abs(x)))


def _dot(a, b):
    return jnp.dot(a, b, preferred_element_type=F32)


def _dot_nt(a, b):
    return lax.dot_general(a, b, (((1,), (1,)), ((), ())), preferred_element_type=F32)


def _dot_tn(a, b):
    return lax.dot_general(a, b, (((0,), (0,)), ((), ())), preferred_element_type=F32)


def _cumsum(x, axis):
    n = x.shape[axis]
    idx = lax.broadcasted_iota(jnp.int32, x.shape, axis)
    s = 1
    while s < n:
        x = x + jnp.where(idx >= s, pltpu.roll(x, s, axis=axis), 0.0)
        s *= 2
    return x


def _ffn_tail(x, f, p, gpost_ref, gple_ref, wgate_ref, wproj_ref):
    x1 = x + _rms(f, gpost_ref[...])
    gate = _sigmoid(_dot(_rms(x1, gple_ref[...]).astype(BF16), wgate_ref[...]))
    return x1 + gate * _dot(p.astype(BF16), wproj_ref[...])


def _ffn_chunk_cols(j, hidden):
    return (slice(j * FFN_CHUNK, (j + 1) * FFN_CHUNK),
            slice(hidden + j * FFN_CHUNK, hidden + (j + 1) * FFN_CHUNK))


def _ffn_prompt_kernel(x_ref, p_ref, gpre_ref, gpost_ref, gple_ref, wup_ref, cw_ref, cb_ref, wdown_ref,
                       wgate_ref, wproj_ref, out_ref, conv_ref, carry_ref, xp_ref, pp_ref, acc_ref):
    t = pl.program_id(1)
    rows = FFN_ROWS
    ntile = x_ref.shape[0] // rows
    hidden = wdown_ref.shape[0]
    nch = hidden // FFN_CHUNK
    S = SUBLANES
    G = rows // S
    nx = xp_ref.shape[1]
    npl = pp_ref.shape[1]

    @pl.when(t == 0)
    def _():
        carry_ref[...] = jnp.zeros_like(carry_ref)

    def interleaved(i):
        return pl.ds(((S * i) % G) * S + (S * i) // G, S, stride=S)

    first = lax.broadcasted_iota(jnp.int32, (S, FFN_CHUNK), 0) == 0

    def load_tile(n):
        for i in range(G):
            src = slice(n * rows + S * i, n * rows + S * (i + 1))
            for c in range(nx):
                xp_ref[n, c, interleaved(i), :] = x_ref[src, c * LANES:(c + 1) * LANES]
            for c in range(npl):
                pp_ref[n, c, interleaved(i), :] = p_ref[src, c * LANES:(c + 1) * LANES]
        x = jnp.concatenate([xp_ref[n, c] for c in range(nx)], axis=1)
        p = jnp.concatenate([pp_ref[n, c] for c in range(npl)], axis=1)
        ple = _dot(p.astype(BF16), wproj_ref[...])
        return x, ple, _rms(x, gpre_ref[...]).astype(BF16)

    def conv(u, cols):
        prev = carry_ref[:, cols]
        back1 = jnp.where(first, prev[2 * S - 1:2 * S], pltpu.roll(u[rows - S:], 1, axis=0))
        back2 = jnp.where(first, prev[S - 1:S], pltpu.roll(u[rows - 2 * S:rows - S], 1, axis=0))
        u1 = jnp.concatenate([back1, u[:rows - S]], axis=0)
        u2 = jnp.concatenate([back2, back1, u[:rows - 2 * S]], axis=0)
        carry_ref[:, cols] = u[rows - 2 * S:]
        cw = cw_ref[:, cols]
        return cb_ref[:, cols] + u2 * cw[0:1] + u1 * cw[1:2] + u * cw[2:3]

    def up(xn, j):
        return [_dot(xn, wup_ref[:, cols]) for cols in _ffn_chunk_cols(j, hidden)]

    def gate_of(n, x):
        out = []
        for part in range(2):
            r = slice(part * (rows // 2), (part + 1) * (rows // 2))
            x1 = x[r] + _rms(acc_ref[n, r, :], gpost_ref[...])
            out.append((x1, _dot(_rms(x1, gple_ref[...]).astype(BF16), wgate_ref[...])))
        return out

    def store_tile(n, gated, ple):
        for part, (x1, gate) in enumerate(gated):
            r = slice(part * (rows // 2), (part + 1) * (rows // 2))
            res = x1 + _sigmoid(gate) * ple[r]
            for c in range(nx):
                xp_ref[n, c, r, :] = res[:, c * LANES:(c + 1) * LANES]
        for i in range(G):
            dst = slice(n * rows + S * i, n * rows + S * (i + 1))
            for c in range(nx):
                out_ref[dst, c * LANES:(c + 1) * LANES] = xp_ref[n, c, interleaved(i), :]

    tiles = [load_tile(n) for n in range(ntile)]
    tail = None
    gated = None
    for n in range(ntile):
        x, ple, xn = tiles[n]
        u = up(xn, 0)
        pending = []
        for j in range(nch):
            u_next = up(xn, j + 1) if j + 1 < nch else None
            if tail is not None and j == 2:
                gated = gate_of(tail[0], tail[1])
            if tail is not None and j == 5:
                store_tile(tail[0], gated, tail[2])
                tail = None
            gcols, vcols = _ffn_chunk_cols(j, hidden)
            cg = conv(u[0], gcols)
            pending.append((cg * _sigmoid(cg) * conv(u[1], vcols)).astype(BF16))
            if len(pending) == FFN_DOWN_GROUP or j == nch - 1:
                lo = (j + 1 - len(pending)) * FFN_CHUNK
                d = _dot(jnp.concatenate(pending, axis=1), wdown_ref[lo:(j + 1) * FFN_CHUNK, :])
                if lo == 0:
                    acc_ref[n] = d
                else:
                    acc_ref[n] += d
                pending = []
            u = u_next
        tail = (n, x, ple)
    store_tile(tail[0], gate_of(tail[0], tail[1]), tail[2])

    @pl.when(t == pl.num_programs(1) - 1)
    def _():
        conv_ref[0, 0:1, :] = carry_ref[S - 1:S, :]
        conv_ref[0, 1:2, :] = carry_ref[2 * S - 1:2 * S, :]


def _ffn_sample_kernel(x_ref, p_ref, prev0_ref, prev1_ref, gpre_ref, gpost_ref, gple_ref, wup_ref, cw_ref,
                       cb_ref, wdown_ref, wgate_ref, wproj_ref, out_ref, u_ref, acc_ref):
    hidden = wdown_ref.shape[0]
    x = x_ref[...]
    xn = _rms(x, gpre_ref[...]).astype(BF16)

    def conv(cols):
        u = _dot(xn, wup_ref[:, cols])
        u_ref[:, cols] = u
        cw = cw_ref[:, cols]
        return cb_ref[:, cols] + prev0_ref[:, cols] * cw[0:1] + prev1_ref[:, cols] * cw[1:2] + u * cw[2:3]

    for j in range(hidden // FFN_CHUNK):
        gcols, vcols = _ffn_chunk_cols(j, hidden)
        cg = conv(gcols)
        h = cg * _sigmoid(cg) * conv(vcols)
        d = _dot(h.astype(BF16), wdown_ref[j * FFN_CHUNK:(j + 1) * FFN_CHUNK, :])
        if j == 0:
            acc_ref[...] = d
        else:
            acc_ref[...] += d
    out_ref[...] = _ffn_tail(x, acc_ref[...], p_ref[...], gpost_ref, gple_ref, wgate_ref, wproj_ref)


_FFN_KEYS = ("g_pre_ffn", "g_post_ffn", "g_ple", "ffn_wup", "ffn_cw", "ffn_cb", "ffn_wdown", "ple_wgate",
             "ple_wproj")


def _ffn_prompt(h, p_all, w, layer):
    b, t, d = h.shape
    f2 = w["ffn_wup"].shape[-1]
    step = FFN_ROWS * FFN_TILES
    nt = t // step
    assert t % step == 0 and (FFN_ROWS // SUBLANES) % SUBLANES == 0
    out, conv = pl.pallas_call(
        _ffn_prompt_kernel,
        grid=(b, nt),
        in_specs=[pl.BlockSpec((step, d), lambda i, j: (i * nt + j, 0)),
                  pl.BlockSpec((step, PLE_DIM), lambda i, j: ((layer * b + i) * nt + j, 0))]
        + [_layer(w[k], layer) for k in _FFN_KEYS],
        out_specs=[pl.BlockSpec((step, d), lambda i, j: (i * nt + j, 0)),
                   pl.BlockSpec((1, 2, f2), lambda i, j: (i, 0, 0))],
        out_shape=[jax.ShapeDtypeStruct((b * t, d), F32),
                   jax.ShapeDtypeStruct((b, 2, f2), F32)],
        scratch_shapes=[pltpu.VMEM((2 * SUBLANES, f2), F32),
                        pltpu.VMEM((FFN_TILES, d // LANES, FFN_ROWS, LANES), F32),
                        pltpu.VMEM((FFN_TILES, PLE_DIM // LANES, FFN_ROWS, LANES), F32),
                        pltpu.VMEM((FFN_TILES, FFN_ROWS, d), F32)],
        compiler_params=_params("arbitrary", "arbitrary"),
        name="ffn_prompt",
    )(h.reshape(b * t, d), p_all.reshape(-1, PLE_DIM), *[w[k] for k in _FFN_KEYS])
    return out.reshape(b, t, d), conv


def _ffn_sample(h, p, prev0_all, prev1_all, w, layer):
    b, d = h.shape
    f2 = w["ffn_wup"].shape[-1]
    return pl.pallas_call(
        _ffn_sample_kernel,
        grid=(1,),
        in_specs=[_resident(h.shape), _resident(p.shape), _layer(prev0_all, layer), _layer(prev1_all, layer)]
        + [_layer(w[k], layer) for k in _FFN_KEYS],
        out_specs=[_whole((b, d)), _whole((b, f2))],
        out_shape=[jax.ShapeDtypeStruct((b, d), F32), jax.ShapeDtypeStruct((b, f2), F32)],
        scratch_shapes=[pltpu.VMEM((b, d), F32)],
        compiler_params=_params("arbitrary"),
        name="ffn_sample",
    )(h, p, prev0_all, prev1_all, *[w[k] for k in _FFN_KEYS])


_ML_KEYS = ("g_pre_mix", "g_post_mix", "ml_win", "ml_wgc", "ml_bgc", "ml_wgr", "ml_bgr", "ml_wout")


def _mlstm_prompt_kernel(x_ref, gpre_ref, gpost_ref, win_ref, wgc_ref, bgc_ref, wgr_ref, bgr_ref, wout_ref,
                         out_ref, c_out_ref, n_out_ref, m_out_ref, c_ref, n_ref, m_ref):
    t = pl.program_id(1)
    L = ML_CHUNK_ROWS
    nchunk = x_ref.shape[0] // L
    heads = range(ML_HEADS)

    @pl.when(t == 0)
    def _():
        c_ref[...] = jnp.zeros_like(c_ref)
        n_ref[...] = jnp.zeros_like(n_ref)
        m_ref[...] = jnp.zeros_like(m_ref)

    causal = lax.broadcasted_iota(jnp.int32, (L, L), 0) >= lax.broadcasted_iota(jnp.int32, (L, L), 1)

    def project_qk(i):
        r = slice(i * L, (i + 1) * L)
        xn = _rms(x_ref[r, :], gpre_ref[...]).astype(BF16)
        gc = _dot(xn, wgc_ref[...]) + bgc_ref[...]
        gr = _dot_nt(wgr_ref[...], xn) + bgr_ref[:, 0:1]
        proj = _dot(xn, win_ref[:, :2 * ML_HK])
        is_forget_r = lax.broadcasted_iota(jnp.int32, gr.shape, 0) >= ML_HEADS
        bc = _cumsum(_log_sigmoid(gc), 0)
        br = _cumsum(jnp.where(is_forget_r, _log_sigmoid(gr), 0.0), 1)
        q = [proj[:, h * ML_DK:(h + 1) * ML_DK] * (ML_DK ** -0.5) for h in heads]
        k = [proj[:, ML_HK + h * ML_DK:ML_HK + (h + 1) * ML_DK] for h in heads]
        return dict(xn=xn, gc=gc, gr=gr, bc=bc, br=br, q=q, k=k)

    def head_scores(p):
        qb = [a.astype(BF16) for a in p["q"]]
        qk = [_dot_nt(qb[h], p["k"][h].astype(BF16)) for h in heads]
        qc = [_dot(qb[h], c_ref[h].astype(BF16)) for h in heads]
        return qk, qc

    def project_vo(p):
        proj = _dot(p["xn"], win_ref[:, 2 * ML_HK:])
        return [proj[:, h * ML_DV:(h + 1) * ML_DV].astype(BF16) for h in heads], proj[:, ML_HEADS * ML_DV:]

    def gating(p, qk):
        out = []
        for h in heads:
            ig_r = p["gr"][h:h + 1, :]
            b_r = p["br"][ML_HEADS + h:ML_HEADS + h + 1, :]
            ig_c = p["gc"][:, h:h + 1]
            b_c = p["bc"][:, ML_HEADS + h:ML_HEADS + h + 1]
            m_prev = m_ref[h:h + 1, 0:1]
            n_prev = n_ref[h:h + 1, :]

            d = jnp.where(causal, b_c + (ig_r - b_r), -jnp.inf)
            a = b_c + m_prev
            m_t = jnp.maximum(a, jnp.max(d, axis=1, keepdims=True))
            w_inter = jnp.exp(a - m_t)
            s = qk[h] * jnp.exp(d - m_t)
            qn = w_inter * jnp.sum(p["q"][h] * n_prev, axis=1, keepdims=True) + jnp.sum(s, axis=1, keepdims=True)
            inv_den = 1.0 / jnp.maximum(jnp.abs(qn), jnp.exp(-m_t))

            b_last = b_r[:, L - 1:L]
            g_r = b_last - b_r + ig_r
            g_c = b_last - b_c + ig_c
            m_new = jnp.maximum(b_last + m_prev, jnp.max(g_r, axis=1, keepdims=True))
            w_old = jnp.exp(b_last + m_prev - m_new)
            wk = jnp.exp(g_c - m_new) * p["k"][h]
            out.append(dict(s=s.astype(BF16), wk=wk.astype(BF16), w_inter=w_inter, inv_den=inv_den, w_old=w_old,
                            n_new=w_old * n_prev + jnp.sum(wk, axis=0, keepdims=True), m_new=m_new))
        return out

    def finish(i, hidden, o):
        y = _dot((_sigmoid(o) * hidden).astype(BF16), wout_ref[...])
        r = slice(i * L, (i + 1) * L)
        out_ref[r, :] = x_ref[r, :] + _rms(y, gpost_ref[...])

    p = project_qk(0)
    qk, qc = head_scores(p)
    vb, o = project_vo(p)
    for i in range(nchunk):
        p_next = project_qk(i + 1) if i + 1 < nchunk else None
        g = gating(p, qk)
        sv = [_dot(g[h]["s"], vb[h]) for h in heads]
        kv = [_dot_tn(g[h]["wk"], vb[h]) for h in heads]
        hidden = jnp.concatenate([(g[h]["w_inter"] * qc[h] + sv[h]) * g[h]["inv_den"] for h in heads], axis=1)
        for h in heads:
            c_ref[h] = g[h]["w_old"] * c_ref[h] + kv[h]
            n_ref[h:h + 1, :] = g[h]["n_new"]
            m_ref[h:h + 1, :] = jnp.broadcast_to(g[h]["m_new"], (1, LANES))
        o_done = o
        if p_next is not None:
            p = p_next
            qk, qc = head_scores(p)
            vb, o = project_vo(p)
        finish(i, hidden, o_done)

    @pl.when(t == pl.num_programs(1) - 1)
    def _():
        c_out_ref[0] = c_ref[...]
        n_out_ref[0] = n_ref[0:ML_HEADS, :]
        m_out_ref[0] = m_ref[...]


def _ml_layers(layer):
    return [2 * layer, 2 * layer] + [layer] * (len(_ML_KEYS) - 2)


def _mlstm_prompt(h, w, layer):
    b, t, d = h.shape
    L = ML_STEP_ROWS
    nt = t // L
    assert t % L == 0 and L % ML_CHUNK_ROWS == 0
    out, c, n, m = pl.pallas_call(
        _mlstm_prompt_kernel,
        grid=(b, nt),
        in_specs=[pl.BlockSpec((L, d), lambda i, j: (i * nt + j, 0))]
        + [_layer(w[k], l) for k, l in zip(_ML_KEYS, _ml_layers(layer))],
        out_specs=[pl.BlockSpec((L, d), lambda i, j: (i * nt + j, 0)),
                   pl.BlockSpec((1, ML_HEADS, ML_DK, ML_DV), lambda i, j: (i, 0, 0, 0)),
                   pl.BlockSpec((1, ML_HEADS, ML_DK), lambda i, j: (i, 0, 0)),
                   pl.BlockSpec((1, SUBLANES, LANES), lambda i, j: (i, 0, 0))],
        out_shape=[jax.ShapeDtypeStruct((b * t, d), F32),
                   jax.ShapeDtypeStruct((b, ML_HEADS, ML_DK, ML_DV), F32),
                   jax.ShapeDtypeStruct((b, ML_HEADS, ML_DK), F32),
                   jax.ShapeDtypeStruct((b, SUBLANES, LANES), F32)],
        scratch_shapes=[pltpu.VMEM((ML_HEADS, ML_DK, ML_DV), F32), pltpu.VMEM((SUBLANES, ML_DK), F32),
                        pltpu.VMEM((SUBLANES, LANES), F32)],
        compiler_params=_params("arbitrary", "arbitrary"),
        name="mlstm_prompt",
    )(h.reshape(b * t, d), *[w[k] for k in _ML_KEYS])
    return out.reshape(b, t, d), c, n, m[:, :ML_HEADS, 0]


def _mlstm_sample_pre_kernel(x_ref, n_ref, m_ref, gpre_ref, win_ref, wgc_ref, bgc_ref,
                             q_ref, kw_ref, v_ref, o_ref, wold_ref, scal_ref, nnew_ref):
    H = ML_HEADS
    xn = _rms(x_ref[...], gpre_ref[...]).astype(BF16)
    proj = _dot(xn, win_ref[...])
    gates = _dot(xn, wgc_ref[...]) + bgc_ref[...]
    v_ref[...] = proj[:, 2 * ML_HK:2 * ML_HK + H * ML_DV]
    o_ref[...] = proj[:, 2 * ML_HK + H * ML_DV:]
    scal_ref[...] = jnp.zeros_like(scal_ref)
    for h in range(H):
        q = proj[:, h * ML_DK:(h + 1) * ML_DK] * (ML_DK ** -0.5)
        k = proj[:, ML_HK + h * ML_DK:ML_HK + (h + 1) * ML_DK]
        n_prev = n_ref[:, h * ML_DK:(h + 1) * ML_DK]
        ig = gates[:, h:h + 1]
        lf = _log_sigmoid(gates[:, H + h:H + h + 1])
        a = lf + m_ref[:, h:h + 1]
        m_t = jnp.maximum(a, ig)
        w_old = jnp.exp(a - m_t)
        w_new = jnp.exp(ig - m_t)
        s = jnp.sum(q * k, axis=1, keepdims=True) * w_new
        qn = w_old * jnp.sum(q * n_prev, axis=1, keepdims=True) + s
        q_ref[:, h * ML_DK:(h + 1) * ML_DK] = q
        kw_ref[:, h * ML_DK:(h + 1) * ML_DK] = w_new * k
        nnew_ref[:, h * ML_DK:(h + 1) * ML_DK] = w_old * n_prev + w_new * k
        wold_ref[:, h * ML_DV:(h + 1) * ML_DV] = jnp.broadcast_to(w_old, (w_old.shape[0], ML_DV))
        scal_ref[:, h:h + 1] = w_old
        scal_ref[:, H + h:H + h + 1] = s
        scal_ref[:, 2 * H + h:2 * H + h + 1] = jnp.maximum(jnp.abs(qn), jnp.exp(-m_t))
        scal_ref[:, 3 * H + h:3 * H + h + 1] = m_t


def _mlstm_sample_state_kernel(c_ref, qt_ref, kt_ref, v_ref, wold_ref, *rest):
    cnew_ref, qc_ref = rest[-2:]
    for b in range(c_ref.shape[0]):
        for h in range(ML_HEADS):
            c = c_ref[b, h]
            qc = qt_ref[0, h * ML_DK:(h + 1) * ML_DK, b:b + 1]
            kc = kt_ref[0, h * ML_DK:(h + 1) * ML_DK, b:b + 1]
            vr = v_ref[b:b + 1, h * ML_DV:(h + 1) * ML_DV]
            wo = wold_ref[b:b + 1, h * ML_DV:(h + 1) * ML_DV]
            qc_ref[b:b + 1, h * ML_DV:(h + 1) * ML_DV] = jnp.sum(qc * c, axis=0, keepdims=True)
            cnew_ref[b, h] = c * wo + kc * vr


def _mlstm_sample_post_kernel(x_ref, qc_ref, v_ref, o_ref, scal_ref, gpost_ref, wout_ref, out_ref, h_ref):
    H = ML_HEADS
    for h in range(H):
        sl = slice(h * ML_DV, (h + 1) * ML_DV)
        num = scal_ref[:, h:h + 1] * qc_ref[:, sl] + scal_ref[:, H + h:H + h + 1] * v_ref[:, sl]
        h_ref[:, sl] = num / scal_ref[:, 2 * H + h:2 * H + h + 1]
    y = _dot((_sigmoid(o_ref[...]) * h_ref[...]).astype(BF16), wout_ref[...])
    out_ref[...] = x_ref[...] + _rms(y, gpost_ref[...])


def _aliased(stack_prev, args, in_specs):
    if stack_prev is None:
        return {}
    args.append(stack_prev)
    in_specs.append(pl.BlockSpec(memory_space=pl.ANY))
    return {len(args) - 1: 0}


def _mlstm_sample(h, c_all, n, m, w, layer, c_stack):
    b, d = h.shape
    H = ML_HEADS
    bb = SAMPLE_STATE_BATCH
    nb = b // bb
    assert b % bb == 0
    pre_in = [h, n.reshape(b, H * ML_DK), jnp.pad(m, ((0, 0), (0, LANES - H)))]
    pre_keys = ("g_pre_mix", "ml_win", "ml_wgc", "ml_bgc")
    pre_layers = (2 * layer, layer, layer, layer)
    pre_out = [(b, ML_HK), (b, ML_HK), (b, H * ML_DV), (b, H * ML_DV), (b, H * ML_DV), (b, LANES), (b, ML_HK)]
    q, kw, v, o, wold, scal, nnew = pl.pallas_call(
        _mlstm_sample_pre_kernel,
        grid=(1,),
        in_specs=[_resident(a.shape) for a in pre_in] + [_layer(w[k], l) for k, l in zip(pre_keys, pre_layers)],
        out_specs=[_whole(s) for s in pre_out],
        out_shape=[jax.ShapeDtypeStruct(s, F32) for s in pre_out],
        compiler_params=_params("arbitrary"),
        name="mlstm_sample_pre",
    )(*pre_in, *[w[k] for k in pre_keys])

    def cols(a):
        return a.reshape(nb, bb, ML_HK).transpose(0, 2, 1)

    state_args = [c_all, cols(q), cols(kw), v, wold]
    state_specs = [pl.BlockSpec((None, bb, H, ML_DK, ML_DV), lambda i: (layer, i, 0, 0, 0)),
                   pl.BlockSpec((1, ML_HK, bb), lambda i: (i, 0, 0)),
                   pl.BlockSpec((1, ML_HK, bb), lambda i: (i, 0, 0)),
                   pl.BlockSpec((bb, H * ML_DV), lambda i: (i, 0)),
                   pl.BlockSpec((bb, H * ML_DV), lambda i: (i, 0))]
    aliases = _aliased(c_stack, state_args, state_specs)
    c_stack, qc = pl.pallas_call(
        _mlstm_sample_state_kernel,
        grid=(nb,),
        in_specs=state_specs,
        out_specs=[pl.BlockSpec((None, bb, H, ML_DK, ML_DV), lambda i: (layer, i, 0, 0, 0)),
                   pl.BlockSpec((bb, H * ML_DV), lambda i: (i, 0))],
        out_shape=[jax.ShapeDtypeStruct(c_all.shape, F32), jax.ShapeDtypeStruct((b, H * ML_DV), F32)],
        input_output_aliases=aliases,
        compiler_params=_params("arbitrary"),
        name="mlstm_sample_state",
    )(*state_args)

    post_in = [h, qc, v, o, scal]
    out = pl.pallas_call(
        _mlstm_sample_post_kernel,
        grid=(1,),
        in_specs=[_resident(a.shape) for a in post_in]
        + [_layer(w["g_post_mix"], 2 * layer), _layer(w["ml_wout"], layer)],
        out_specs=_whole((b, d)),
        out_shape=jax.ShapeDtypeStruct((b, d), F32),
        scratch_shapes=[pltpu.VMEM((b, H * ML_DV), F32)],
        compiler_params=_params("arbitrary"),
        name="mlstm_sample_post",
    )(*post_in, w["g_post_mix"], w["ml_wout"])
    return out, c_stack, nnew.reshape(b, H, ML_DK), scal[:, 3 * H:4 * H]


def _rope_tables(pos):
    half = SW_HD // 2
    inv = ROPE_THETA ** (-jnp.arange(half, dtype=F32) / half)
    ang = pos[:, None] * inv[None, :]
    cos = jnp.tile(jnp.cos(ang), (1, LANES // half))
    sin = jnp.tile(jnp.concatenate([-jnp.sin(ang), jnp.sin(ang)], axis=1), (1, LANES // SW_HD))
    return cos, sin


def _first_half_lanes(rows):
    lane = lax.broadcasted_iota(jnp.int32, (rows, LANES), 1)
    return jnp.bitwise_and(lane, SW_HD - 1) < SW_HD // 2


def _rope_block(x, cos, sin, first_half):
    partner = jnp.where(first_half, pltpu.roll(x, LANES - SW_HD // 2, axis=1), pltpu.roll(x, SW_HD // 2, axis=1))
    return x * cos + partner * sin


def _pair_heads(a, axis):
    shp = a.shape
    a = a.reshape(shp[:axis] + (2, 2, SW_GROUP) + shp[axis + 1:])
    a = jnp.swapaxes(a, axis + 1, axis + 2)
    return a.reshape(shp)


_SW_KEYS = ("g_pre_mix", "g_post_mix", "sw_win", "sw_wout")


def _swa_prompt_kernel(sink_ref, x_ref, cos_ref, sin_ref, gpre_ref, gpost_ref, win_ref, wout_ref,
                       out_ref, kout_ref, vout_ref, k_ref, v_ref, kf_ref, vf_ref):
    t = pl.program_id(1)
    rows = x_ref.shape[0]
    nsub = rows // WINDOW
    W = WINDOW
    G = SW_GROUP
    K2 = 2 * W
    pairs = range(SW_KV_HEADS // 2)

    @pl.when(t == 0)
    def _():
        k_ref[0:W, :] = jnp.zeros((W, SW_NKV), BF16)
        v_ref[0:W, :] = jnp.zeros((W, SW_NKV), BF16)

    first_half = _first_half_lanes(SW_UNIT * W)
    qi = jnp.bitwise_and(lax.broadcasted_iota(jnp.int32, (G * W, K2), 0), W - 1)
    ki = lax.broadcasted_iota(jnp.int32, (G * W, K2), 1)
    band = (ki >= qi) & (ki <= qi + W)
    lane = lax.broadcasted_iota(jnp.int32, (1, LANES), 1)
    low = lane < SW_HD
    high = jnp.logical_not(low)
    zeros = jnp.zeros((1, LANES), BF16)
    ones_low = jnp.where(low, 1.0, 0.0).astype(BF16)
    ones_high = jnp.where(high, 1.0, 0.0).astype(BF16)
    member = lax.shift_right_logical(lax.broadcasted_iota(jnp.int32, (G * W, 1), 0),
                                     jnp.int32(W.bit_length() - 1))

    def sink_column(pair, half):
        col = jnp.full((G * W, 1), sink_ref[2 * (pair * G) + half], F32)
        for g in range(1, G):
            col = jnp.where(member == g, sink_ref[2 * (pair * G + g) + half], col)
        return col

    sinks = [[sink_column(pair, half) for half in range(2)] for pair in pairs]

    U = SW_UNIT

    def project(u):
        r = slice(u * U * W, (u + 1) * U * W)
        xn = _rms(x_ref[r, :], gpre_ref[...]).astype(BF16)
        proj = _dot(xn, win_ref[:, :SW_NQ + SW_NKV])
        v = _dot(xn, win_ref[:, SW_NQ + SW_NKV:])
        cos = cos_ref[r, :]
        sin = sin_ref[r, :]
        q = [(_rope_block(proj[:, c * LANES:(c + 1) * LANES], cos, sin, first_half)
              * (SW_HD ** -0.5 * LOG2E)).astype(BF16) for c in range(SW_NQ // LANES)]
        for c in range(SW_NKV // LANES):
            blk = _rope_block(proj[:, SW_NQ + c * LANES:SW_NQ + (c + 1) * LANES], cos, sin, first_half)
            kf_ref[r, c * LANES:(c + 1) * LANES] = blk
            k_ref[W + u * U * W:W + (u + 1) * U * W, c * LANES:(c + 1) * LANES] = blk.astype(BF16)
        vf_ref[r, :] = v
        v_ref[W + u * U * W:W + (u + 1) * U * W, :] = v.astype(BF16)
        return [[jnp.concatenate([a[j * W:(j + 1) * W] for a in q[pair * G:(pair + 1) * G]], axis=0)
                 for pair in pairs] for j in range(U)]

    def scores(i, q4):
        out = []
        for pair in pairs:
            k2 = k_ref[i * W:(i + 2) * W, pair * LANES:(pair + 1) * LANES]
            kz = jnp.concatenate([jnp.where(low, k2, zeros), jnp.where(high, k2, zeros)], axis=0)
            out.append(_dot_nt(q4[pair], kz))
        return out

    def softmax_numerators(i, s):
        first_key = jnp.where(t * nsub + i > 0, 0, W)
        mask = band & (ki >= first_key)
        out = []
        for pair in pairs:
            probs, sink_terms = [], []
            for half in range(2):
                sh = jnp.where(mask, s[pair][:, half * K2:(half + 1) * K2], -jnp.inf)
                sk = sinks[pair][half]
                mx = jnp.maximum(jnp.max(sh, axis=1, keepdims=True), sk)
                probs.append(jnp.exp2(sh - mx).astype(BF16))
                sink_terms.append(jnp.exp2(sk - mx))
            out.append((jnp.concatenate(probs, axis=1), jnp.where(low, sink_terms[0], sink_terms[1])))
        return out

    def weighted_values(i, soft):
        out = []
        for pair in pairs:
            v2 = v_ref[i * W:(i + 2) * W, pair * LANES:(pair + 1) * LANES]
            vz = jnp.concatenate(
                [jnp.concatenate([jnp.where(low, v2, zeros), jnp.broadcast_to(ones_low, v2.shape)], axis=1),
                 jnp.concatenate([jnp.where(high, v2, zeros), jnp.broadcast_to(ones_high, v2.shape)], axis=1)],
                axis=0)
            out.append((_dot(soft[pair][0], vz), soft[pair][1]))
        return out

    def finish(u, weighted):
        rows_u = []
        for j in range(U):
            blocks = []
            for o, sink_term in weighted[j]:
                both = (o[:, :LANES] / (o[:, LANES:] + sink_term)).astype(BF16)
                blocks += [both[g * W:(g + 1) * W] for g in range(G)]
            rows_u.append(jnp.concatenate(blocks, axis=1))
        y = _dot(jnp.concatenate(rows_u, axis=0), wout_ref[...])
        r = slice(u * U * W, (u + 1) * U * W)
        out_ref[r, :] = x_ref[r, :] + _rms(y, gpost_ref[...])

    nunit = nsub // U
    q_unit = project(0)
    s = scores(0, q_unit[0])
    weighted, done = [], None
    for i in range(nsub):
        u, j = divmod(i, U)
        q_next = project(u + 1) if j == U - 1 and u + 1 < nunit else None
        soft = softmax_numerators(i, s)
        if j == 0 and done is not None:
            finish(u - 1, done)
            done = None
        weighted.append(weighted_values(i, soft))
        if j == U - 1:
            done, weighted = weighted, []
            if q_next is not None:
                q_unit = q_next
        if i + 1 < nsub:
            s = scores(i + 1, q_unit[(i + 1) % U])
    finish(nunit - 1, done)

    k_ref[0:W, :] = k_ref[rows:rows + W, :]
    v_ref[0:W, :] = v_ref[rows:rows + W, :]

    @pl.when(t == pl.num_programs(1) - 1)
    def _():
        kout_ref[0] = kf_ref[rows - W:, :].T
        vout_ref[0] = vf_ref[rows - W:, :].T


def _swa_prompt(h, w, layer):
    b, t, d = h.shape
    rows = SW_ROWS
    nt = t // rows
    assert t % rows == 0 and rows % WINDOW == 0 and t >= WINDOW
    cos, sin = _rope_tables(jnp.arange(t, dtype=F32))
    layers = (2 * layer + 1, 2 * layer + 1, layer, layer)
    out, k, v = pl.pallas_call(
        _swa_prompt_kernel,
        grid_spec=pltpu.PrefetchScalarGridSpec(
            num_scalar_prefetch=1,
            grid=(b, nt),
            in_specs=[pl.BlockSpec((rows, d), lambda i, j, s: (i * nt + j, 0)),
                      pl.BlockSpec((rows, LANES), lambda i, j, s: (j, 0)),
                      pl.BlockSpec((rows, LANES), lambda i, j, s: (j, 0))]
            + [_layer(w[k_], l) for k_, l in zip(_SW_KEYS, layers)],
            out_specs=[pl.BlockSpec((rows, d), lambda i, j, s: (i * nt + j, 0)),
                       pl.BlockSpec((1, SW_NKV, WINDOW), lambda i, j, s: (i, 0, 0)),
                       pl.BlockSpec((1, SW_NKV, WINDOW), lambda i, j, s: (i, 0, 0))],
            scratch_shapes=[pltpu.VMEM((WINDOW + rows, SW_NKV), BF16), pltpu.VMEM((WINDOW + rows, SW_NKV), BF16),
                            pltpu.VMEM((rows, SW_NKV), F32), pltpu.VMEM((rows, SW_NKV), F32)]),
        out_shape=[jax.ShapeDtypeStruct((b * t, d), F32),
                   jax.ShapeDtypeStruct((b, SW_NKV, WINDOW), F32),
                   jax.ShapeDtypeStruct((b, SW_NKV, WINDOW), F32)],
        compiler_params=_params("arbitrary", "arbitrary"),
        name="swa_prompt",
    )(w["sw_sinks_log2"][layer], h.reshape(b * t, d), cos, sin, *[w[k_] for k_ in _SW_KEYS])
    return out.reshape(b, t, d), k, v


def _swa_sample_pre_kernel(x_ref, cos_ref, sin_ref, gpre_ref, win_ref, q_ref, k_ref, v_ref, kt_ref, vt_ref):
    rows = x_ref.shape[0]
    xn = _rms(x_ref[...], gpre_ref[...]).astype(BF16)
    proj = _dot(xn, win_ref[...])
    cos = cos_ref[...]
    sin = sin_ref[...]
    first_half = _first_half_lanes(rows)
    for c in range(SW_NQ // LANES):
        blk = _rope_block(proj[:, c * LANES:(c + 1) * LANES], cos, sin, first_half)
        q_ref[:, c * LANES:(c + 1) * LANES] = blk * (SW_HD ** -0.5)
    for c in range(SW_NKV // LANES):
        k_ref[:, c * LANES:(c + 1) * LANES] = _rope_block(
            proj[:, SW_NQ + c * LANES:SW_NQ + (c + 1) * LANES], cos, sin, first_half)
    v = proj[:, SW_NQ + SW_NKV:]
    v_ref[...] = v
    kt_ref[...] = k_ref[...].T
    vt_ref[...] = v.T


def _swa_sample_attn_kernel(q_ref, kc_ref, vc_ref, knew_ref, vnew_ref, kcol_ref, vcol_ref, sink_ref, *rest):
    o_ref, kout_ref, vout_ref = rest[-3:]
    nbuf = kc_ref.shape[-1]
    pos = lax.broadcasted_iota(jnp.int32, (SW_HEADS, SW_NKV), 0)
    grp = 2 * lax.shift_right_logical(pos, jnp.int32(3)) + jnp.bitwise_and(pos, 1)
    col_kv = lax.shift_right_logical(lax.broadcasted_iota(jnp.int32, (SW_HEADS, SW_NKV), 1),
                                     jnp.int32(SW_HD.bit_length() - 1))
    own = grp == col_kv
    sk = sink_ref[:, 0:1]
    lane = lax.broadcasted_iota(jnp.int32, (1, nbuf), 1)
    dist = nbuf - lane
    valid = (dist >= 0) & (dist <= WINDOW)
    newest = lane == nbuf - 1
    n = q_ref.shape[0]
    qs = [q_ref[b] for b in range(n)]
    ss = [jnp.where(valid, _dot(qs[b].astype(BF16), kc_ref[b].reshape(SW_NKV, nbuf).astype(BF16)), -jnp.inf)
          for b in range(n)]
    soft = []
    for b in range(n):
        s_new = jnp.sum(qs[b] * knew_ref[b], axis=1, keepdims=True)
        mx = jnp.maximum(jnp.maximum(jnp.max(ss[b], axis=1, keepdims=True), s_new), sk)
        p = jnp.exp(ss[b] - mx)
        p_new = jnp.exp(s_new - mx)
        den = jnp.sum(p, axis=1, keepdims=True) + p_new + jnp.exp(sk - mx)
        soft.append((p.astype(BF16), p_new, 1.0 / den))
    outs = [_dot_nt(soft[b][0], vc_ref[b].reshape(SW_NKV, nbuf).astype(BF16)) for b in range(n)]
    for b in range(n):
        o = jnp.where(own, outs[b] + soft[b][1] * vnew_ref[b], 0.0)
        o = o[:, :LANES] + o[:, LANES:]
        o = o + pltpu.roll(o, SW_HD, axis=1)
        o_ref[b] = (o * soft[b][2])[:, :SW_HD]
    for b in range(n):
        kout_ref[b] = jnp.where(newest, kcol_ref[0, :, b:b + 1],
                                pltpu.roll(kc_ref[b].reshape(SW_NKV, nbuf), nbuf - 1, axis=1)
                                ).reshape(SW_KV_HEADS, SW_HD, nbuf)
        vout_ref[b] = jnp.where(newest, vcol_ref[0, :, b:b + 1],
                                pltpu.roll(vc_ref[b].reshape(SW_NKV, nbuf), nbuf - 1, axis=1)
                                ).reshape(SW_KV_HEADS, SW_HD, nbuf)


def _out_proj_kernel(x_ref, a_ref, gpost_ref, wout_ref, out_ref):
    y = _dot(a_ref[...].astype(BF16), wout_ref[...])
    out_ref[...] = x_ref[...] + _rms(y, gpost_ref[...])


def _swa_sample(h, kc_all, vc_all, w, layer, k_stack, v_stack):
    b, d = h.shape
    nbuf = kc_all.shape[-1]
    bb = SAMPLE_STATE_BATCH
    nb = b // bb
    assert b % bb == 0 and nbuf == LANES
    cos, sin = _rope_tables(jnp.full((1,), PAST_LEN, F32))
    pre_in = [h, cos, sin]
    pre_out = [(b, SW_NQ), (b, SW_NKV), (b, SW_NKV), (SW_NKV, b), (SW_NKV, b)]
    q, knew, vnew, kt, vt = pl.pallas_call(
        _swa_sample_pre_kernel,
        grid=(1,),
        in_specs=[_resident(a.shape) for a in pre_in]
        + [_layer(w["g_pre_mix"], 2 * layer + 1), _layer(w["sw_win"], layer)],
        out_specs=[_whole(s) for s in pre_out],
        out_shape=[jax.ShapeDtypeStruct(s, F32) for s in pre_out],
        compiler_params=_params("arbitrary"),
        name="swa_sample_pre",
    )(*pre_in, w["g_pre_mix"], w["sw_win"])

    def cols(a):
        return a.reshape(SW_NKV, nb, bb).transpose(1, 0, 2)

    cache_spec = pl.BlockSpec((None, bb, SW_KV_HEADS, SW_HD, nbuf), lambda i: (layer, i, 0, 0, 0))
    pos = jnp.arange(SW_HEADS)
    own_kv = (2 * (pos // 8) + pos % 2)[:, None] == jnp.arange(SW_KV_HEADS)[None, :]
    q_bd = jnp.where(own_kv[None, :, :, None], q.reshape(b, SW_HEADS, 1, SW_HD), 0.0).reshape(b, SW_HEADS, SW_NKV)
    args = [q_bd, kc_all, vc_all, knew.reshape(b, 1, SW_NKV), vnew.reshape(b, 1, SW_NKV),
            cols(kt), cols(vt), jnp.broadcast_to(w["sw_sinks"][layer][:, None], (SW_HEADS, LANES))]
    specs = [pl.BlockSpec((bb, SW_HEADS, SW_NKV), lambda i: (i, 0, 0)), cache_spec, cache_spec,
             pl.BlockSpec((bb, 1, SW_NKV), lambda i: (i, 0, 0)),
             pl.BlockSpec((bb, 1, SW_NKV), lambda i: (i, 0, 0)),
             pl.BlockSpec((1, SW_NKV, bb), lambda i: (i, 0, 0)),
             pl.BlockSpec((1, SW_NKV, bb), lambda i: (i, 0, 0)),
             _resident((SW_HEADS, LANES))]
    aliases = {}
    if k_stack is not None:
        args += [k_stack, v_stack]
        specs += [pl.BlockSpec(memory_space=pl.ANY), pl.BlockSpec(memory_space=pl.ANY)]
        aliases = {len(args) - 2: 1, len(args) - 1: 2}
    o, k_stack, v_stack = pl.pallas_call(
        _swa_sample_attn_kernel,
        grid=(nb,),
        in_specs=specs,
        out_specs=[pl.BlockSpec((bb, SW_HEADS, SW_HD), lambda i: (i, 0, 0)), cache_spec, cache_spec],
        out_shape=[jax.ShapeDtypeStruct((b, SW_HEADS, SW_HD), F32),
                   jax.ShapeDtypeStruct(kc_all.shape, F32), jax.ShapeDtypeStruct(vc_all.shape, F32)],
        input_output_aliases=aliases,
        compiler_params=_params("arbitrary"),
        name="swa_sample_attn",
    )(*args)

    post_in = [h, o.reshape(b, SW_NQ)]
    out = pl.pallas_call(
        _out_proj_kernel,
        grid=(1,),
        in_specs=[_resident(a.shape) for a in post_in]
        + [_layer(w["g_post_mix"], 2 * layer + 1), _layer(w["sw_wout"], layer)],
        out_specs=_whole((b, d)),
        out_shape=jax.ShapeDtypeStruct((b, d), F32),
        compiler_params=_params("arbitrary"),
        name="swa_sample_post",
    )(*post_in, w["g_post_mix"], w["sw_wout"])
    return out, k_stack, v_stack


def _transpose_cast_kernel(wt_ref, out_ref):
    out_ref[...] = wt_ref[...].T.astype(BF16)


def _transpose_cast(wt, rows):
    layers, _, d = wt.shape
    blk = 4 * LANES
    assert rows % blk == 0
    return pl.pallas_call(
        _transpose_cast_kernel,
        grid=(layers, rows // blk),
        in_specs=[pl.BlockSpec((None, blk, d), lambda l, j: (l, j, 0))],
        out_specs=pl.BlockSpec((None, d, blk), lambda l, j: (l, 0, j)),
        out_shape=jax.ShapeDtypeStruct((layers, d, rows), BF16),
        compiler_params=_params("arbitrary", "arbitrary"),
        name="transpose_cast",
    )(wt)


def _prepare(g_pre_mix, g_post_mix, g_pre_ffn, g_post_ffn, g_ple, ml_w_in, ml_b_gate, ml_w_out, sw_w_in,
             sw_sinks, sw_w_out, ffn_w_up, ffn_conv_w, ffn_conv_b, ffn_w_down, ple_w_proj, ple_w_gate):
    gate_w = ml_w_in[:, :, ML_QKVO:]
    q_cols = _pair_heads(sw_w_in[:, :, :SW_NQ].reshape(sw_w_in.shape[:2] + (SW_HEADS, SW_HD)), 2)
    sw_win = jnp.concatenate([q_cols.reshape(sw_w_in.shape[:2] + (SW_NQ,)), sw_w_in[:, :, SW_NQ:]], axis=-1)
    sw_wout = _pair_heads(sw_w_out.reshape((sw_w_out.shape[0], SW_HEADS, SW_HD, sw_w_out.shape[-1])), 1)
    sinks = _pair_heads(sw_sinks, 1)
    return dict(
        g_pre_mix=g_pre_mix[:, None], g_post_mix=g_post_mix[:, None], g_pre_ffn=g_pre_ffn[:, None],
        g_post_ffn=g_post_ffn[:, None], g_ple=g_ple[:, None],
        ffn_wup=ffn_w_up.astype(BF16), ffn_cw=ffn_conv_w, ffn_cb=ffn_conv_b[:, None],
        ffn_wdown=ffn_w_down.astype(BF16), ple_wgate=ple_w_gate.astype(BF16), ple_wproj=ple_w_proj.astype(BF16),
        ml_win=_transpose_cast(jnp.swapaxes(ml_w_in, 1, 2), ML_QKVO),
        ml_wgc=jnp.pad(gate_w, ((0, 0), (0, 0), (0, LANES - 2 * ML_HEADS))).astype(BF16),
        ml_bgc=jnp.pad(ml_b_gate, ((0, 0), (0, LANES - 2 * ML_HEADS)))[:, None],
        ml_wgr=jnp.swapaxes(gate_w, 1, 2).astype(BF16),
        ml_bgr=jnp.broadcast_to(ml_b_gate[:, :, None], ml_b_gate.shape + (LANES,)),
        ml_wout=ml_w_out.astype(BF16),
        sw_win=sw_win.astype(BF16), sw_wout=sw_wout.reshape(sw_w_out.shape).astype(BF16),
        sw_sinks=sinks, sw_sinks_log2=sinks * LOG2E)


def kernel(x_prompt, x_sample, p_prompt, p_sample, state_mlstm_C, state_mlstm_n, state_mlstm_m, cache_swa_k, cache_swa_v, state_conv, g_pre_mix, g_post_mix, g_pre_ffn, g_post_ffn, g_ple, ml_w_in, ml_b_gate, ml_w_out, sw_w_in, sw_sinks, sw_w_out, ffn_w_up, ffn_conv_w, ffn_conv_b, ffn_w_down, ple_w_proj, ple_w_gate):
    depth = g_pre_mix.shape[0]
    w = _prepare(g_pre_mix, g_post_mix, g_pre_ffn, g_post_ffn, g_ple, ml_w_in, ml_b_gate, ml_w_out, sw_w_in,
                 sw_sinks, sw_w_out, ffn_w_up, ffn_conv_w, ffn_conv_b, ffn_w_down, ple_w_proj, ple_w_gate)
    kc_all = jnp.transpose(cache_swa_k, (0, 1, 3, 4, 2))
    vc_all = jnp.transpose(cache_swa_v, (0, 1, 3, 4, 2))
    hp = x_prompt
    hs = x_sample[:, 0, :]
    cp, np_, mp, ns, ms = [], [], [], [], []
    kp, vp = [], []
    convp, convs = [], []
    conv_prev0, conv_prev1 = state_conv[:, :, 0], state_conv[:, :, 1]
    c_stack = k_stack = v_stack = None
    for i in range(depth):
        j = i // 2
        if i % 2 == 0:
            hp, c1, n1, m1 = _mlstm_prompt(hp, w, j)
            hs, c_stack, n2, m2 = _mlstm_sample(hs, state_mlstm_C, state_mlstm_n[j], state_mlstm_m[j], w, j, c_stack)
            cp.append(c1); np_.append(n1); mp.append(m1)
            ns.append(n2); ms.append(m2)
        else:
            hp, k1, v1 = _swa_prompt(hp, w, j)
            hs, k_stack, v_stack = _swa_sample(hs, kc_all, vc_all, w, j, k_stack, v_stack)
            kp.append(k1); vp.append(v1)
        hp, cvp = _ffn_prompt(hp, p_prompt, w, i)
        hs, cvs = _ffn_sample(hs, p_sample[i][:, 0, :], conv_prev0, conv_prev1, w, i)
        convp.append(cvp); convs.append(cvs)

    def cache_layout(a):
        return jnp.transpose(a, (0, 1, 4, 2, 3))

    kv_shape = (len(kp), x_prompt.shape[0], SW_KV_HEADS, SW_HD, WINDOW)
    return (hp, hs[:, None, :],
            jnp.stack(cp), jnp.stack(np_), jnp.stack(mp),
            cache_layout(jnp.stack(kp).reshape(kv_shape)), cache_layout(jnp.stack(vp).reshape(kv_shape)),
            jnp.stack(convp),
            c_stack, jnp.stack(ns), jnp.stack(ms), cache_layout(k_stack), cache_layout(v_stack),
            jnp.stack([conv_prev1, jnp.stack(convs)], axis=2))
```
